```python
import math
import jax, jax.numpy as jnp
from jax import lax
import numpy as np

D_MODEL = 1024
BATCH = 4
SEQ = 8192
DEPTH = 4
DEC_BATCH = 32
DEC_SEQ = 2048
PAST_LEN = 128

GRID_W = 64
HEAD_DIM = 64
N_Q_HEADS = 8
N_KV_HEADS = 2
Q_PER_KV = N_Q_HEADS // N_KV_HEADS
Q_WIDTH = N_Q_HEADS * HEAD_DIM
KV_WIDTH = N_KV_HEADS * HEAD_DIM
SG_GROUPS = 8
SG_GROUP_DIM = 64
SG_WIDTH = SG_GROUPS * SG_GROUP_DIM
CHUNK = 128
Q_BLOCK = 128
D_FF = 2048
CONV_W = 3
ROPE_THETA = 10000.0
EPS = 1e-6
IN_WIDTH = Q_WIDTH + 2 * KV_WIDTH + 2 * SG_WIDTH + 2 * D_MODEL

kernel_name = "hybrid_gqa_gmlp_convffn_encoder"


def rmsnorm(x, g):
    xf = x.astype(jnp.float32)
    y = xf * lax.rsqrt(jnp.mean(xf * xf, axis=-1, keepdims=True) + EPS)
    return (y * g.astype(jnp.float32)).astype(x.dtype)


def axial_rope_tables(seq_len, dtype):
    t = jnp.arange(seq_len)
    row = (t // GRID_W).astype(jnp.float32)
    col = (t % GRID_W).astype(jnp.float32)
    half = HEAD_DIM // 2
    freq = ROPE_THETA ** (-jnp.arange(0, half, 2, dtype=jnp.float32) / half)
    ang_r = row[:, None] * freq[None, :]
    ang_c = col[:, None] * freq[None, :]
    return (jnp.cos(ang_r).astype(dtype), jnp.sin(ang_r).astype(dtype),
            jnp.cos(ang_c).astype(dtype), jnp.sin(ang_c).astype(dtype))


def _rot_half(x, cos, sin):
    x1, x2 = jnp.split(x, 2, axis=-1)
    c = cos[:, None, :]
    s = sin[:, None, :]
    return jnp.concatenate([x1 * c - x2 * s, x1 * s + x2 * c], axis=-1)


def apply_axial_rope(x, tabs):
    cr, sr, cc, sc = tabs
    xr, xc = jnp.split(x, 2, axis=-1)
    return jnp.concatenate([_rot_half(xr, cr, sr), _rot_half(xc, cc, sc)], axis=-1)


def gqa_attention(q, k, v):
    b, s, _, d = q.shape
    nb = s // Q_BLOCK
    scale = 1.0 / math.sqrt(HEAD_DIM)
    qb = q.reshape(b, nb, Q_BLOCK, N_KV_HEADS, Q_PER_KV, d).transpose(1, 0, 2, 3, 4, 5)

    def block(qblk):
        sc = jnp.einsum('bqgrd,bkgd->bgrqk', qblk, k).astype(jnp.float32) * scale
        p = jax.nn.softmax(sc, axis=-1).astype(v.dtype)
        return jnp.einsum('bgrqk,bkgd->bqgrd', p, v)

    o = lax.map(block, qb)
    return o.transpose(1, 0, 2, 3, 4, 5).reshape(b, s, Q_WIDTH)


def spatial_gating(u, vs, sg_norm_g, sg_w, sg_b):
    b, s, _ = u.shape
    n = s // CHUNK
    vs = rmsnorm(vs, sg_norm_g).reshape(b, n, CHUNK, SG_GROUPS, SG_GROUP_DIM)
    sp = jnp.einsum('gpq,bnqgc->bnpgc', sg_w, vs) + jnp.swapaxes(sg_b, 0, 1)[:, :, None]
    return (u.reshape(b, n, CHUNK, SG_GROUPS, SG_GROUP_DIM) * sp).reshape(b, s, SG_WIDTH)


def token_mixer(h, tabs, w_in, q_norm_g, k_norm_g, sg_norm_g, sg_w, sg_b,
                w_branch_a, w_branch_b, w_mix_out):
    b, s, _ = h.shape
    z = h @ w_in
    q, k, v, u, vs, ga, gb = jnp.split(z, np.cumsum(
        [Q_WIDTH, KV_WIDTH, KV_WIDTH, SG_WIDTH, SG_WIDTH, D_MODEL]).tolist(), axis=-1)
    q = rmsnorm(q.reshape(b, s, N_Q_HEADS, HEAD_DIM), q_norm_g)
    k = rmsnorm(k.reshape(b, s, N_KV_HEADS, HEAD_DIM), k_norm_g)
    v = v.reshape(b, s, N_KV_HEADS, HEAD_DIM)
    q = apply_axial_rope(q, tabs)
    k = apply_axial_rope(k, tabs)
    ya = gqa_attention(q, k, v) @ w_branch_a
    yb = spatial_gating(jax.nn.gelu(u), jax.nn.gelu(vs), sg_norm_g, sg_w, sg_b) @ w_branch_b
    m = jax.nn.sigmoid(ga) * ya + jax.nn.sigmoid(gb) * yb
    return m @ w_mix_out


def conv_ffn(h, w_up, conv_w, conv_b, w_down):
    a = h @ w_up
    s = a.shape[1]
    ap = jnp.pad(a, ((0, 0), (CONV_W // 2, CONV_W // 2), (0, 0)))
    c = ap[:, 0:s] * conv_w[0] + ap[:, 1:s + 1] * conv_w[1] + ap[:, 2:s + 2] * conv_w[2] + conv_b
    g, val = jnp.split(c, 2, axis=-1)
    return (jax.nn.gelu(g) * val) @ w_down


def trunk(x, attn_norm_g, w_in, q_norm_g, k_norm_g, sg_norm_g, sg_w, sg_b,
          w_branch_a, w_branch_b, w_mix_out, ffn_norm_g, w_up, conv_w, conv_b,
          w_down, final_norm_g):
    tabs = axial_rope_tables(x.shape[1], x.dtype)
    for l in range(DEPTH):
        x = x + token_mixer(rmsnorm(x, attn_norm_g[l]), tabs, w_in[l], q_norm_g[l],
                            k_norm_g[l], sg_norm_g[l], sg_w[l], sg_b[l],
                            w_branch_a[l], w_branch_b[l], w_mix_out[l])
        x = x + conv_ffn(rmsnorm(x, ffn_norm_g[l]), w_up[l], conv_w[l], conv_b[l], w_down[l])
    return rmsnorm(x, final_norm_g)


def setup_inputs(seed: int = 0) -> dict:
    key = jax.random.key(seed)
    ks = jax.random.split(key, 20)
    f32 = jnp.float32

    def nrm(k, shape, fan_in):
        return jax.random.normal(k, shape, f32) * (fan_in ** -0.5)

    def gain(k, shape):
        return 1.0 + 0.02 * jax.random.normal(k, shape, f32)

    return {
        "x_prompt": jax.random.normal(ks[0], (BATCH, SEQ, D_MODEL), f32),
        "x_sample": jax.random.normal(ks[1], (DEC_BATCH, DEC_SEQ, D_MODEL), f32),
        "attn_norm_g": gain(ks[2], (DEPTH, D_MODEL)),
        "w_in": nrm(ks[3], (DEPTH, D_MODEL, IN_WIDTH), D_MODEL),
        "q_norm_g": gain(ks[4], (DEPTH, HEAD_DIM)),
        "k_norm_g": gain(ks[5], (DEPTH, HEAD_DIM)),
        "sg_norm_g": gain(ks[6], (DEPTH, SG_WIDTH)),
        "sg_w": nrm(ks[7], (DEPTH, SG_GROUPS, CHUNK, CHUNK), CHUNK),
        "sg_b": gain(ks[8], (DEPTH, SG_GROUPS, CHUNK)),
        "w_branch_a": nrm(ks[9], (DEPTH, Q_WIDTH, D_MODEL), Q_WIDTH),
        "w_branch_b": nrm(ks[10], (DEPTH, SG_WIDTH, D_MODEL), SG_WIDTH),
        "w_mix_out": nrm(ks[11], (DEPTH, D_MODEL, D_MODEL), D_MODEL),
        "ffn_norm_g": gain(ks[12], (DEPTH, D_MODEL)),
        "w_up": nrm(ks[13], (DEPTH, D_MODEL, 2 * D_FF), D_MODEL),
        "conv_w": nrm(ks[14], (DEPTH, CONV_W, 2 * D_FF), CONV_W),
        "conv_b": 0.02 * jax.random.normal(ks[15], (DEPTH, 2 * D_FF), f32),
        "w_down": nrm(ks[16], (DEPTH, D_FF, D_MODEL), D_FF),
        "final_norm_g": gain(ks[17], (D_MODEL,)),
    }


def reference(x_prompt, x_sample, attn_norm_g, w_in, q_norm_g, k_norm_g, sg_norm_g,
              sg_w, sg_b, w_branch_a, w_branch_b, w_mix_out, ffn_norm_g, w_up, conv_w,
              conv_b, w_down, final_norm_g):
    y_prompt = trunk(x_prompt, attn_norm_g, w_in, q_norm_g, k_norm_g, sg_norm_g, sg_w, sg_b,
                     w_branch_a, w_branch_b, w_mix_out, ffn_norm_g, w_up, conv_w, conv_b,
                     w_down, final_norm_g)
    y_sample = trunk(x_sample, attn_norm_g, w_in, q_norm_g, k_norm_g, sg_norm_g, sg_w, sg_b,
                     w_branch_a, w_branch_b, w_mix_out, ffn_norm_g, w_up, conv_w, conv_b,
                     w_down, final_norm_g)
    return (y_prompt, y_sample)
```

```python
import functools
import math

import jax
import jax.numpy as jnp
import numpy as np
from jax import lax
from jax.experimental import pallas as pl
from jax.experimental.pallas import tpu as pltpu

D_MODEL = 1024
GRID_W = 64
HEAD_DIM = 64
N_Q_HEADS = 8
N_KV_HEADS = 2
Q_PER_KV = N_Q_HEADS // N_KV_HEADS
Q_WIDTH = N_Q_HEADS * HEAD_DIM
KV_WIDTH = N_KV_HEADS * HEAD_DIM
SG_GROUPS = 8
SG_GROUP_DIM = 64
SG_WIDTH = SG_GROUPS * SG_GROUP_DIM
CHUNK = 128
D_FF = 2048
CONV_W = 3
ROPE_THETA = 10000.0
EPS = 1e-6
IN_WIDTH = Q_WIDTH + 2 * KV_WIDTH + 2 * SG_WIDTH + 2 * D_MODEL

LANES = 128
BF16_SUBLANES = 16
V7X_VMEM_BYTES = 64 * 1024 * 1024

_OFF_K = Q_WIDTH
_OFF_V = _OFF_K + KV_WIDTH
_OFF_U = _OFF_V + KV_WIDTH
_OFF_VS = _OFF_U + SG_WIDTH
_OFF_GA = _OFF_VS + SG_WIDTH
_OFF_GB = _OFF_GA + D_MODEL

VT_ROWS = HEAD_DIM + BF16_SUBLANES
NEG_BIG = -1e30
HALO = BF16_SUBLANES

F32 = jnp.float32
BF16 = jnp.bfloat16


def _tiles(seq_len):
    tok = min(256, seq_len)
    tq = min(256, seq_len)
    tk = min(256, seq_len)
    return tok, tq, tk


def _vmem_limit(block_bytes):
    return int(min(V7X_VMEM_BYTES - (4 << 20), 2 * block_bytes + (24 << 20)))


def _rms(x, g):
    return x * lax.rsqrt(jnp.mean(x * x, axis=-1, keepdims=True) + EPS) * g


def _lane_lt64(shape):
    lane = lax.broadcasted_iota(jnp.int32, shape, len(shape) - 1)
    return (lane % LANES) < HEAD_DIM


def _in_proj_kernel(x_ref, ng_ref, w_ref, qg_ref, kg_ref, cos_ref, sa_ref, sb_ref,
                    bd_ref, sgg_ref, sgw_ref, sgb_ref, wb_ref,
                    q_ref, k_ref, vt_ref, sga_ref, ybg_ref, *, tk):
    rows = x_ref.shape[0]
    hb = _rms(x_ref[...], ng_ref[...]).astype(BF16)

    def proj(lo, hi):
        return jnp.dot(hb, w_ref[:, lo:hi], preferred_element_type=F32)

    cos = cos_ref[...]
    sa = sa_ref[...]
    sb = sb_ref[...]

    def head_norm_rope(z, gain, scale):
        ss = jnp.dot((z * z).astype(BF16), bd_ref[...], preferred_element_type=F32)
        zn = z * lax.rsqrt(ss * (1.0 / HEAD_DIM) + EPS) * gain
        rot = zn * cos + pltpu.roll(zn, LANES - 16, 1) * sa + pltpu.roll(zn, 16, 1) * sb
        return rot * scale if scale != 1.0 else rot

    zq = proj(0, Q_WIDTH)
    for j in range(Q_WIDTH // LANES):
        q_ref[:, j * LANES:(j + 1) * LANES] = head_norm_rope(
            zq[:, j * LANES:(j + 1) * LANES], qg_ref[...], 1.0 / math.sqrt(HEAD_DIM)).astype(BF16)

    zkv = proj(_OFF_K, _OFF_U)
    kr = head_norm_rope(zkv[:, :KV_WIDTH], kg_ref[...], 1.0)
    first = _lane_lt64(kr.shape)
    k_ref[0] = jnp.where(first, kr, 0.0).astype(BF16)
    k_ref[1] = jnp.where(first, 0.0, kr).astype(BF16)

    vt = zkv[:, KV_WIDTH:].T
    ones = jnp.ones((VT_ROWS - HEAD_DIM, tk), BF16)
    for g in range(N_KV_HEADS):
        for r in range(rows // tk):
            vt_ref[g, r, :HEAD_DIM, :] = vt[g * HEAD_DIM:(g + 1) * HEAD_DIM,
                                            r * tk:(r + 1) * tk].astype(BF16)
            vt_ref[g, r, HEAD_DIM:, :] = ones

    u = jax.nn.gelu(proj(_OFF_U, _OFF_VS))
    vs = _rms(jax.nn.gelu(proj(_OFF_VS, _OFF_GA)), sgg_ref[...]).astype(BF16)
    pair = 2 * CHUNK
    lt64 = _lane_lt64((CHUNK, 2 * LANES))
    sgu_rows = []
    for c in range(rows // pair):
        r0 = c * pair
        slabs0, slabs1 = [], []
        for s in range(SG_WIDTH // LANES):
            cols = slice(s * LANES, (s + 1) * LANES)
            rhs = jnp.concatenate([vs[r0:r0 + CHUNK, cols], vs[r0 + CHUNK:r0 + pair, cols]], axis=1)
            oa = jnp.dot(sgw_ref[2 * s], rhs, preferred_element_type=F32)
            ob = jnp.dot(sgw_ref[2 * s + 1], rhs, preferred_element_type=F32)
            bias = sgb_ref[:, cols]
            sp = jnp.where(lt64, oa, ob) + jnp.concatenate([bias, bias], axis=1)
            slabs0.append(u[r0:r0 + CHUNK, cols] * sp[:, :LANES])
            slabs1.append(u[r0 + CHUNK:r0 + pair, cols] * sp[:, LANES:])
        sgu_rows.append(jnp.concatenate(slabs0, axis=1))
        sgu_rows.append(jnp.concatenate(slabs1, axis=1))
    sgu = jnp.concatenate(sgu_rows, axis=0).astype(BF16)
    yb = jnp.dot(sgu, wb_ref[...], preferred_element_type=F32)

    sga_ref[...] = jax.nn.sigmoid(proj(_OFF_GA, _OFF_GB)).astype(BF16)
    ybg_ref[...] = (jax.nn.sigmoid(proj(_OFF_GB, IN_WIDTH)) * yb).astype(BF16)


def _in_proj(x, layer, p, seq_len):
    n = x.shape[0]
    tok, _, tk = _tiles(seq_len)
    n_pos = seq_len // tok
    const2 = lambda i: (0, 0)
    layer3 = lambda i: (layer, 0, 0)
    blocks = (tok * D_MODEL * 4 + D_MODEL * IN_WIDTH * 2 + SG_WIDTH * D_MODEL * 2
              + tok * (Q_WIDTH + 2 * LANES + 2 * VT_ROWS + 2 * D_MODEL) * 2)
    return pl.pallas_call(
        functools.partial(_in_proj_kernel, tk=tk),
        grid=(n // tok,),
        in_specs=[
            pl.BlockSpec((tok, D_MODEL), lambda i: (i, 0)),
            pl.BlockSpec((None, 1, D_MODEL), layer3),
            pl.BlockSpec((None, D_MODEL, IN_WIDTH), layer3),
            pl.BlockSpec((None, 1, LANES), layer3),
            pl.BlockSpec((None, 1, LANES), layer3),
            pl.BlockSpec((tok, LANES), lambda i: (i % n_pos, 0)),
            pl.BlockSpec((tok, LANES), lambda i: (i % n_pos, 0)),
            pl.BlockSpec((tok, LANES), lambda i: (i % n_pos, 0)),
            pl.BlockSpec((LANES, LANES), const2),
            pl.BlockSpec((None, 1, SG_WIDTH), layer3),
            pl.BlockSpec((None, SG_GROUPS, CHUNK, CHUNK), lambda i: (layer, 0, 0, 0)),
            pl.BlockSpec((None, CHUNK, SG_WIDTH), layer3),
            pl.BlockSpec((None, SG_WIDTH, D_MODEL), layer3),
        ],
        out_specs=[
            pl.BlockSpec((tok, Q_WIDTH), lambda i: (i, 0)),
            pl.BlockSpec((N_KV_HEADS, tok, LANES), lambda i: (0, i, 0)),
            pl.BlockSpec((N_KV_HEADS, tok // tk, VT_ROWS, tk), lambda i: (0, i, 0, 0)),
            pl.BlockSpec((tok, D_MODEL), lambda i: (i, 0)),
            pl.BlockSpec((tok, D_MODEL), lambda i: (i, 0)),
        ],
        out_shape=[
            jax.ShapeDtypeStruct((n, Q_WIDTH), BF16),
            jax.ShapeDtypeStruct((N_KV_HEADS, n, LANES), BF16),
            jax.ShapeDtypeStruct((N_KV_HEADS, n // tk, VT_ROWS, tk), BF16),
            jax.ShapeDtypeStruct((n, D_MODEL), BF16),
            jax.ShapeDtypeStruct((n, D_MODEL), BF16),
        ],
        compiler_params=pltpu.CompilerParams(
            dimension_semantics=("arbitrary",), vmem_limit_bytes=_vmem_limit(blocks)),
        name="in_proj",
    )(x, p["attn_norm_g"], p["w_in"], p["q_norm_g"], p["k_norm_g"],
      p["rope_cos"], p["rope_sa"], p["rope_sb"], p["head_ones"],
      p["sg_norm_g"], p["sg_w"], p["sg_bias"], p["w_branch_b"])


def _attn_mix_kernel(q_ref, k_ref, vt_ref, sga_ref, ybg_ref, x_ref, wa_ref, wm_ref,
                     o_ref, acc_ref, *, tk):
    tq = q_ref.shape[0]
    n_kv = k_ref.shape[1] // tk
    acc_ref[...] = jnp.zeros(acc_ref.shape, F32)

    def kv_step(i, m_run):
        start = pl.multiple_of(i * tk, tk)
        m_next = []
        for g in range(N_KV_HEADS):
            kb = k_ref[g, pl.ds(start, tk), :]
            vb = vt_ref[g, i]
            for j in range(Q_PER_KV):
                h = g * Q_PER_KV + j
                qs = q_ref[:, j * LANES:(j + 1) * LANES]
                s = lax.dot_general(kb, qs, (((1,), (1,)), ((), ())),
                                    preferred_element_type=F32)
                m_old = m_run[h]
                m_new = jnp.maximum(m_old, jnp.max(s, axis=0, keepdims=True))
                alpha = jnp.exp(m_old - m_new)
                pt = jnp.exp(s - m_new).astype(BF16)
                pv = jnp.dot(vb, pt, preferred_element_type=F32)
                acc_ref[h] = alpha * acc_ref[h] + pv
                m_next.append(m_new)
        return tuple(m_next)

    m0 = tuple(jnp.full((1, tq), NEG_BIG, F32) for _ in range(N_Q_HEADS))
    lax.fori_loop(0, n_kv, kv_step, m0)

    heads = []
    for h in range(N_Q_HEADS):
        a = acc_ref[h]
        heads.append(a[:HEAD_DIM] * (1.0 / a[HEAD_DIM:HEAD_DIM + 1]))
    ot = jnp.concatenate(heads, axis=0)
    ya = jnp.dot(ot.T.astype(BF16), wa_ref[...], preferred_element_type=F32)
    mix = sga_ref[...].astype(F32) * ya + ybg_ref[...].astype(F32)
    o_ref[...] = x_ref[...] + jnp.dot(mix.astype(BF16), wm_ref[...],
                                      preferred_element_type=F32)


def _attn_mix(x, q, kp, vt, sga, ybg, layer, p, batch, seq_len):
    _, tq, tk = _tiles(seq_len)
    n_kv = seq_len // tk
    n_q = seq_len // tq
    row = lambda b, i: (b * n_q + i, 0)
    blocks = (tq * (Q_WIDTH + 2 * D_MODEL) * 2 + 2 * tq * D_MODEL * 4
              + N_KV_HEADS * seq_len * (LANES + VT_ROWS) * 2
              + (Q_WIDTH + D_MODEL) * D_MODEL * 2)
    return pl.pallas_call(
        functools.partial(_attn_mix_kernel, tk=tk),
        grid=(batch, n_q),
        in_specs=[
            pl.BlockSpec((tq, Q_WIDTH), row),
            pl.BlockSpec((N_KV_HEADS, seq_len, LANES), lambda b, i: (0, b, 0)),
            pl.BlockSpec((N_KV_HEADS, n_kv, VT_ROWS, tk), lambda b, i: (0, b, 0, 0)),
            pl.BlockSpec((tq, D_MODEL), row),
            pl.BlockSpec((tq, D_MODEL), row),
            pl.BlockSpec((tq, D_MODEL), row),
            pl.BlockSpec((None, Q_WIDTH, D_MODEL), lambda b, i: (layer, 0, 0)),
            pl.BlockSpec((None, D_MODEL, D_MODEL), lambda b, i: (layer, 0, 0)),
        ],
        out_specs=pl.BlockSpec((tq, D_MODEL), row),
        out_shape=jax.ShapeDtypeStruct(x.shape, F32),
        scratch_shapes=[pltpu.VMEM((N_Q_HEADS, VT_ROWS, tq), F32)],
        compiler_params=pltpu.CompilerParams(
            dimension_semantics=("arbitrary", "arbitrary"),
            vmem_limit_bytes=_vmem_limit(blocks)),
        name="attn_mix",
    )(q, kp, vt, sga, ybg, x, p["w_branch_a"], p["w_mix_out"])


FFN_COLS = 512


def _conv_ffn_kernel(xp_ref, x_ref, xn_ref, ng_ref, wu_ref, cw_ref, cb_ref, wd_ref, fg_ref,
                     o_ref, *, final):
    i = pl.program_id(1)
    rows = x_ref.shape[0]
    x = x_ref[...]
    xp = jnp.where(i > 0, xp_ref[...], 0.0)
    xn = jnp.where(i < pl.num_programs(1) - 1, xn_ref[...], 0.0)
    hb = _rms(jnp.concatenate([xp, x, xn], axis=0), ng_ref[...]).astype(BF16)

    def conv(lo):
        a = jnp.dot(hb, wu_ref[:, lo:lo + FFN_COLS], preferred_element_type=F32)
        w = cw_ref[:, lo:lo + FFN_COLS]
        return (a[HALO - 1:HALO - 1 + rows] * w[0:1] + a[HALO:HALO + rows] * w[1:2]
                + a[HALO + 1:HALO + 1 + rows] * w[2:3] + cb_ref[:, lo:lo + FFN_COLS])

    y = x
    for c in range(D_FF // FFN_COLS):
        gate = jax.nn.gelu(conv(c * FFN_COLS)) * conv(D_FF + c * FFN_COLS)
        y = y + jnp.dot(gate.astype(BF16), wd_ref[c * FFN_COLS:(c + 1) * FFN_COLS, :],
                        preferred_element_type=F32)
    o_ref[...] = _rms(y, fg_ref[...]) if final else y


def _conv_ffn(x, layer, p, batch, seq_len, final):
    tok, _, _ = _tiles(seq_len)
    n_t = seq_len // tok
    per = tok // HALO
    n_halo = seq_len // HALO
    row = lambda b, i: (b * n_t + i, 0)
    prev = lambda b, i: (b * n_halo + jnp.maximum(i * per - 1, 0), 0)
    nxt = lambda b, i: (b * n_halo + jnp.minimum((i + 1) * per, n_halo - 1), 0)
    layer3 = lambda b, i: (layer, 0, 0)
    blocks = 2 * tok * D_MODEL * 4 + 3 * D_MODEL * D_FF * 2
    return pl.pallas_call(
        functools.partial(_conv_ffn_kernel, final=final),
        grid=(batch, n_t),
        in_specs=[
            pl.BlockSpec((HALO, D_MODEL), prev),
            pl.BlockSpec((tok, D_MODEL), row),
            pl.BlockSpec((HALO, D_MODEL), nxt),
            pl.BlockSpec((None, 1, D_MODEL), layer3),
            pl.BlockSpec((None, D_MODEL, 2 * D_FF), layer3),
            pl.BlockSpec((None, CONV_W, 2 * D_FF), layer3),
            pl.BlockSpec((None, 1, 2 * D_FF), layer3),
            pl.BlockSpec((None, D_FF, D_MODEL), layer3),
            pl.BlockSpec((1, D_MODEL), lambda b, i: (0, 0)),
        ],
        out_specs=pl.BlockSpec((tok, D_MODEL), row),
        out_shape=jax.ShapeDtypeStruct(x.shape, F32),
        compiler_params=pltpu.CompilerParams(
            dimension_semantics=("arbitrary", "arbitrary"),
            vmem_limit_bytes=_vmem_limit(blocks)),
        name="conv_ffn",
    )(x, x, x, p["ffn_norm_g"], p["w_up"], p["conv_w"], p["conv_b"], p["w_down"],
      p["final_norm_g"])


def _rope_tables(seq_len):
    t = jnp.arange(seq_len)
    half = HEAD_DIM // 2
    freq = ROPE_THETA ** (-jnp.arange(0, half, 2, dtype=F32) / half)
    ang_r = (t // GRID_W).astype(F32)[:, None] * freq[None, :]
    ang_c = (t % GRID_W).astype(F32)[:, None] * freq[None, :]
    cr, sr, cc, sc = jnp.cos(ang_r), jnp.sin(ang_r), jnp.cos(ang_c), jnp.sin(ang_c)
    zero = jnp.zeros_like(sr)
    per_head = lambda parts: jnp.tile(jnp.concatenate(parts, axis=1), (1, LANES // HEAD_DIM))
    cos = per_head([cr, cr, cc, cc])
    sa = per_head([-sr, zero, -sc, zero])
    sb = per_head([zero, sr, zero, sc])
    return cos, sa, sb


def _prepare_params(attn_norm_g, w_in, q_norm_g, k_norm_g, sg_norm_g, sg_w, sg_b,
                    w_branch_a, w_branch_b, w_mix_out, ffn_norm_g, w_up, conv_w, conv_b,
                    w_down, final_norm_g):
    depth = w_in.shape[0]
    order = np.concatenate([
        np.arange(h * HEAD_DIM, (h + 1) * HEAD_DIM)
        for j in range(Q_PER_KV) for h in (j, Q_PER_KV + j)])
    w_in_p = jnp.concatenate([w_in[:, :, order], w_in[:, :, Q_WIDTH:]], axis=2).astype(BF16)
    two_heads = lambda g: jnp.tile(g, (1, LANES // HEAD_DIM)).reshape(depth, 1, LANES)
    blk = np.arange(LANES) // HEAD_DIM
    return dict(
        attn_norm_g=attn_norm_g.reshape(depth, 1, D_MODEL),
        w_in=w_in_p,
        q_norm_g=two_heads(q_norm_g),
        k_norm_g=two_heads(k_norm_g),
        head_ones=jnp.asarray(blk[:, None] == blk[None, :], BF16),
        sg_norm_g=sg_norm_g.reshape(depth, 1, SG_WIDTH),
        sg_w=sg_w.astype(BF16),
        sg_bias=jnp.repeat(jnp.swapaxes(sg_b, 1, 2), SG_GROUP_DIM, axis=2),
        w_branch_a=w_branch_a.astype(BF16),
        w_branch_b=w_branch_b.astype(BF16),
        w_mix_out=w_mix_out.astype(BF16),
        ffn_norm_g=ffn_norm_g.reshape(depth, 1, D_MODEL),
        w_up=w_up.astype(BF16),
        conv_w=conv_w,
        conv_b=conv_b.reshape(depth, 1, 2 * D_FF),
        w_down=w_down.astype(BF16),
        final_norm_g=final_norm_g.reshape(1, D_MODEL),
    )


def _trunk(x3, p):
    batch, seq_len, _ = x3.shape
    depth = p["w_in"].shape[0]
    tok, tq, tk = _tiles(seq_len)
    assert seq_len % tok == 0 and seq_len % tq == 0 and seq_len % tk == 0
    assert tok % (2 * CHUNK) == 0 and tok % tk == 0 and tok % HALO == 0
    p = dict(p)
    p["rope_cos"], p["rope_sa"], p["rope_sb"] = _rope_tables(seq_len)
    x = x3.reshape(batch * seq_len, D_MODEL)
    for layer in range(depth):
        q, kp, vt, sga, ybg = _in_proj(x, layer, p, seq_len)
        x = _attn_mix(x, q, kp, vt, sga, ybg, layer, p, batch, seq_len)
        x = _conv_ffn(x, layer, p, batch, seq_len, final=(layer == depth - 1))
    return x.reshape(batch, seq_len, D_MODEL)


def kernel(x_prompt, x_sample, attn_norm_g, w_in, q_norm_g, k_norm_g, sg_norm_g, sg_w, sg_b,
           w_branch_a, w_branch_b, w_mix_out, ffn_norm_g, w_up, conv_w, conv_b, w_down,
           final_norm_g):
    p = _prepare_params(attn_norm_g, w_in, q_norm_g, k_norm_g, sg_norm_g, sg_w, sg_b,
                        w_branch_a, w_branch_b, w_mix_out, ffn_norm_g, w_up, conv_w, conv_b,
                        w_down, final_norm_g)
    return (_trunk(x_prompt, p), _trunk(x_sample, p))
```

```python
import functools
import math

import jax
import jax.numpy as jnp
import numpy as np
from jax import lax
from jax.experimental import pallas as pl
from jax.experimental.pallas import tpu as pltpu

D_MODEL = 1024
GRID_W = 64
HEAD_DIM = 64
N_Q_HEADS = 8
N_KV_HEADS = 2
Q_PER_KV = N_Q_HEADS // N_KV_HEADS
Q_WIDTH = N_Q_HEADS * HEAD_DIM
KV_WIDTH = N_KV_HEADS * HEAD_DIM
SG_GROUPS = 8
SG_GROUP_DIM = 64
SG_WIDTH = SG_GROUPS * SG_GROUP_DIM
CHUNK = 128
D_FF = 2048
CONV_W = 3
ROPE_THETA = 10000.0
EPS = 1e-6
IN_WIDTH = Q_WIDTH + 2 * KV_WIDTH + 2 * SG_WIDTH + 2 * D_MODEL

LANES = 128
BF16_SUBLANES = 16
V7X_VMEM_BYTES = 64 * 1024 * 1024

_OFF_K = Q_WIDTH
_OFF_V = _OFF_K + KV_WIDTH
_OFF_U = _OFF_V + KV_WIDTH
_OFF_VS = _OFF_U + SG_WIDTH
_OFF_GA = _OFF_VS + SG_WIDTH
_OFF_GB = _OFF_GA + D_MODEL

VT_ROWS = HEAD_DIM + BF16_SUBLANES
NEG_BIG = -1e30
Q_SCALE = math.log2(math.e) / math.sqrt(HEAD_DIM)
HALO = BF16_SUBLANES

F32 = jnp.float32
BF16 = jnp.bfloat16


def _tiles(seq_len):
    tok = min(256, seq_len)
    tq = min(256, seq_len)
    tk = min(256, seq_len // 2)
    return tok, tq, tk


def _vmem_limit(block_bytes, scratch_bytes=0):
    return int(min(V7X_VMEM_BYTES - (4 << 20), 2 * block_bytes + scratch_bytes + (24 << 20)))


def _rms(x, g):
    return x * lax.rsqrt(jnp.mean(x * x, axis=-1, keepdims=True) + EPS) * g


def _lane_lt64(shape):
    lane = lax.broadcasted_iota(jnp.int32, shape, len(shape) - 1)
    return (lane % LANES) < HEAD_DIM


def _in_proj_kernel(x_ref, ng_ref, w_ref, qg_ref, kg_ref, cos_ref, sa_ref, sb_ref,
                    bd_ref, sgg_ref, sgw_ref, sgb_ref, wb_ref,
                    q_ref, k_ref, vt_ref, sga_ref, ybg_ref, *, tk):
    rows = x_ref.shape[0]
    hb = _rms(x_ref[...], ng_ref[...]).astype(BF16)

    def proj(lo, hi):
        return jnp.dot(hb, w_ref[:, lo:hi], preferred_element_type=F32)

    cos = cos_ref[...]
    sa = sa_ref[...]
    sb = sb_ref[...]

    def head_norm_rope(z, gain, scale):
        ss = jnp.dot((z * z).astype(BF16), bd_ref[...], preferred_element_type=F32)
        zn = z * lax.rsqrt(ss * (1.0 / HEAD_DIM) + EPS) * gain
        rot = zn * cos + pltpu.roll(zn, LANES - 16, 1) * sa + pltpu.roll(zn, 16, 1) * sb
        return rot * scale if scale != 1.0 else rot

    zq = proj(0, Q_WIDTH)
    for j in range(Q_WIDTH // LANES):
        q_ref[:, j * LANES:(j + 1) * LANES] = head_norm_rope(
            zq[:, j * LANES:(j + 1) * LANES], qg_ref[...], Q_SCALE).astype(BF16)

    zkv = proj(_OFF_K, _OFF_U)
    kr = head_norm_rope(zkv[:, :KV_WIDTH], kg_ref[...], 1.0)
    first = _lane_lt64(kr.shape)
    k_ref[0] = jnp.where(first, kr, 0.0).astype(BF16)
    k_ref[1] = jnp.where(first, 0.0, kr).astype(BF16)

    vt = zkv[:, KV_WIDTH:].T
    ones = jnp.ones((VT_ROWS - HEAD_DIM, tk), BF16)
    for g in range(N_KV_HEADS):
        for r in range(rows // tk):
            vt_ref[g, r, :HEAD_DIM, :] = vt[g * HEAD_DIM:(g + 1) * HEAD_DIM,
                                            r * tk:(r + 1) * tk].astype(BF16)
            vt_ref[g, r, HEAD_DIM:, :] = ones

    u = jax.nn.gelu(proj(_OFF_U, _OFF_VS))
    vs = _rms(jax.nn.gelu(proj(_OFF_VS, _OFF_GA)), sgg_ref[...]).astype(BF16)
    pair = 2 * CHUNK
    lt64 = _lane_lt64((CHUNK, 2 * LANES))
    sgu_rows = []
    for c in range(rows // pair):
        r0 = c * pair
        slabs0, slabs1 = [], []
        for s in range(SG_WIDTH // LANES):
            cols = slice(s * LANES, (s + 1) * LANES)
            rhs = jnp.concatenate([vs[r0:r0 + CHUNK, cols], vs[r0 + CHUNK:r0 + pair, cols]], axis=1)
            oa = jnp.dot(sgw_ref[2 * s], rhs, preferred_element_type=F32)
            ob = jnp.dot(sgw_ref[2 * s + 1], rhs, preferred_element_type=F32)
            bias = sgb_ref[:, cols]
            sp = jnp.where(lt64, oa, ob) + jnp.concatenate([bias, bias], axis=1)
            slabs0.append(u[r0:r0 + CHUNK, cols] * sp[:, :LANES])
            slabs1.append(u[r0 + CHUNK:r0 + pair, cols] * sp[:, LANES:])
        sgu_rows.append(jnp.concatenate(slabs0, axis=1))
        sgu_rows.append(jnp.concatenate(slabs1, axis=1))
    sgu = jnp.concatenate(sgu_rows, axis=0).astype(BF16)
    yb = jnp.dot(sgu, wb_ref[...], preferred_element_type=F32)

    sga_ref[...] = jax.nn.sigmoid(proj(_OFF_GA, _OFF_GB)).astype(BF16)
    ybg_ref[...] = (jax.nn.sigmoid(proj(_OFF_GB, IN_WIDTH)) * yb).astype(BF16)


def _in_proj(x, layer, p, seq_len):
    n = x.shape[0]
    tok, _, tk = _tiles(seq_len)
    n_pos = seq_len // tok
    const2 = lambda i: (0, 0)
    layer3 = lambda i: (layer, 0, 0)
    blocks = (tok * D_MODEL * 4 + D_MODEL * IN_WIDTH * 2 + SG_WIDTH * D_MODEL * 2
              + tok * (Q_WIDTH + 2 * LANES + 2 * VT_ROWS + 2 * D_MODEL) * 2)
    return pl.pallas_call(
        functools.partial(_in_proj_kernel, tk=tk),
        grid=(n // tok,),
        in_specs=[
            pl.BlockSpec((tok, D_MODEL), lambda i: (i, 0)),
            pl.BlockSpec((None, 1, D_MODEL), layer3),
            pl.BlockSpec((None, D_MODEL, IN_WIDTH), layer3),
            pl.BlockSpec((None, 1, LANES), layer3),
            pl.BlockSpec((None, 1, LANES), layer3),
            pl.BlockSpec((tok, LANES), lambda i: (i % n_pos, 0)),
            pl.BlockSpec((tok, LANES), lambda i: (i % n_pos, 0)),
            pl.BlockSpec((tok, LANES), lambda i: (i % n_pos, 0)),
            pl.BlockSpec((LANES, LANES), const2),
            pl.BlockSpec((None, 1, SG_WIDTH), layer3),
            pl.BlockSpec((None, SG_GROUPS, CHUNK, CHUNK), lambda i: (layer, 0, 0, 0)),
            pl.BlockSpec((None, CHUNK, SG_WIDTH), layer3),
            pl.BlockSpec((None, SG_WIDTH, D_MODEL), layer3),
        ],
        out_specs=[
            pl.BlockSpec((tok, Q_WIDTH), lambda i: (i, 0)),
            pl.BlockSpec((N_KV_HEADS, tok, LANES), lambda i: (0, i, 0)),
            pl.BlockSpec((N_KV_HEADS, tok // tk, VT_ROWS, tk), lambda i: (0, i, 0, 0)),
            pl.BlockSpec((tok, D_MODEL), lambda i: (i, 0)),
            pl.BlockSpec((tok, D_MODEL), lambda i: (i, 0)),
        ],
        out_shape=[
            jax.ShapeDtypeStruct((n, Q_WIDTH), BF16),
            jax.ShapeDtypeStruct((N_KV_HEADS, n, LANES), BF16),
            jax.ShapeDtypeStruct((N_KV_HEADS, n // tk, VT_ROWS, tk), BF16),
            jax.ShapeDtypeStruct((n, D_MODEL), BF16),
            jax.ShapeDtypeStruct((n, D_MODEL), BF16),
        ],
        compiler_params=pltpu.CompilerParams(
            dimension_semantics=("arbitrary",), vmem_limit_bytes=_vmem_limit(blocks)),
        name="in_proj",
    )(x, p["attn_norm_g"], p["w_in"], p["q_norm_g"], p["k_norm_g"],
      p["rope_cos"], p["rope_sa"], p["rope_sb"], p["head_ones"],
      p["sg_norm_g"], p["sg_w"], p["sg_bias"], p["w_branch_b"])


def _attn_mix_kernel(q_ref, k_ref, vt_ref, sga_ref, ybg_ref, x_ref, wa_ref, wm_ref,
                     o_ref, acc_ref, m_ref, s0_ref, s1_ref, bm0_ref, bm1_ref, *, tk):
    tq = q_ref.shape[0]
    n_kv = k_ref.shape[1] // tk
    acc_ref[...] = jnp.zeros(acc_ref.shape, F32)
    m_ref[...] = jnp.full(m_ref.shape, NEG_BIG, F32)

    def scores(blk, s_ref, bm_ref):
        start = pl.multiple_of(blk * tk, tk)
        for g in range(N_KV_HEADS):
            kb = k_ref[g, pl.ds(start, tk), :]
            for j in range(Q_PER_KV):
                h = g * Q_PER_KV + j
                qs = q_ref[:, j * LANES:(j + 1) * LANES]
                s = lax.dot_general(kb, qs, (((1,), (1,)), ((), ())),
                                    preferred_element_type=F32)
                s_ref[h] = s
                bm_ref[h:h + 1, :] = jnp.max(s, axis=0, keepdims=True)

    def accumulate(blk, s_ref, bm_ref):
        for g in range(N_KV_HEADS):
            vb = vt_ref[g, blk]
            for j in range(Q_PER_KV):
                h = g * Q_PER_KV + j
                m_old = m_ref[h:h + 1, :]
                m_new = jnp.maximum(m_old, bm_ref[h:h + 1, :])
                alpha = jnp.exp2(m_old - m_new)
                pt = jnp.exp2(s_ref[h] - m_new).astype(BF16)
                pv = jnp.dot(vb, pt, preferred_element_type=F32)
                acc_ref[h] = alpha * acc_ref[h] + pv
                m_ref[h:h + 1, :] = m_new

    scores(0, s0_ref, bm0_ref)

    def kv_pair(t, carry):
        scores(2 * t + 1, s1_ref, bm1_ref)
        accumulate(2 * t, s0_ref, bm0_ref)
        scores(2 * t + 2, s0_ref, bm0_ref)
        accumulate(2 * t + 1, s1_ref, bm1_ref)
        return carry

    lax.fori_loop(0, n_kv // 2 - 1, kv_pair, 0)
    scores(n_kv - 1, s1_ref, bm1_ref)
    accumulate(n_kv - 2, s0_ref, bm0_ref)
    accumulate(n_kv - 1, s1_ref, bm1_ref)

    heads = []
    for h in range(N_Q_HEADS):
        a = acc_ref[h]
        heads.append(a[:HEAD_DIM] * (1.0 / a[HEAD_DIM:HEAD_DIM + 1]))
    ot = jnp.concatenate(heads, axis=0)
    ya = jnp.dot(ot.T.astype(BF16), wa_ref[...], preferred_element_type=F32)
    mix = sga_ref[...].astype(F32) * ya + ybg_ref[...].astype(F32)
    o_ref[...] = x_ref[...] + jnp.dot(mix.astype(BF16), wm_ref[...],
                                      preferred_element_type=F32)


def _attn_mix(x, q, kp, vt, sga, ybg, layer, p, batch, seq_len):
    _, tq, tk = _tiles(seq_len)
    n_kv = seq_len // tk
    n_q = seq_len // tq
    row = lambda b, i: (b * n_q + i, 0)
    blocks = (tq * (Q_WIDTH + 2 * D_MODEL) * 2 + 2 * tq * D_MODEL * 4
              + N_KV_HEADS * seq_len * (LANES + VT_ROWS) * 2
              + (Q_WIDTH + D_MODEL) * D_MODEL * 2)
    scratch = N_Q_HEADS * tq * 4 * (VT_ROWS + 2 * tk + 3)
    return pl.pallas_call(
        functools.partial(_attn_mix_kernel, tk=tk),
        grid=(batch, n_q),
        in_specs=[
            pl.BlockSpec((tq, Q_WIDTH), row),
            pl.BlockSpec((N_KV_HEADS, seq_len, LANES), lambda b, i: (0, b, 0)),
            pl.BlockSpec((N_KV_HEADS, n_kv, VT_ROWS, tk), lambda b, i: (0, b, 0, 0)),
            pl.BlockSpec((tq, D_MODEL), row),
            pl.BlockSpec((tq, D_MODEL), row),
            pl.BlockSpec((tq, D_MODEL), row),
            pl.BlockSpec((None, Q_WIDTH, D_MODEL), lambda b, i: (layer, 0, 0)),
            pl.BlockSpec((None, D_MODEL, D_MODEL), lambda b, i: (layer, 0, 0)),
        ],
        out_specs=pl.BlockSpec((tq, D_MODEL), row),
        out_shape=jax.ShapeDtypeStruct(x.shape, F32),
        scratch_shapes=[
            pltpu.VMEM((N_Q_HEADS, VT_ROWS, tq), F32),
            pltpu.VMEM((N_Q_HEADS, tq), F32),
            pltpu.VMEM((N_Q_HEADS, tk, tq), F32),
            pltpu.VMEM((N_Q_HEADS, tk, tq), F32),
            pltpu.VMEM((N_Q_HEADS, tq), F32),
            pltpu.VMEM((N_Q_HEADS, tq), F32),
        ],
        compiler_params=pltpu.CompilerParams(
            dimension_semantics=("arbitrary", "arbitrary"),
            vmem_limit_bytes=_vmem_limit(blocks, scratch)),
        name="attn_mix",
    )(q, kp, vt, sga, ybg, x, p["w_branch_a"], p["w_mix_out"])


FFN_COLS = 512


def _conv_ffn_kernel(xp_ref, x_ref, xn_ref, ng_ref, wu_ref, cw_ref, cb_ref, wd_ref, fg_ref,
                     o_ref, *, final):
    i = pl.program_id(1)
    rows = x_ref.shape[0]
    x = x_ref[...]
    xp = jnp.where(i > 0, xp_ref[...], 0.0)
    xn = jnp.where(i < pl.num_programs(1) - 1, xn_ref[...], 0.0)
    hb = _rms(jnp.concatenate([xp, x, xn], axis=0), ng_ref[...]).astype(BF16)

    def conv(lo):
        a = jnp.dot(hb, wu_ref[:, lo:lo + FFN_COLS], preferred_element_type=F32)
        w = cw_ref[:, lo:lo + FFN_COLS]
        return (a[HALO - 1:HALO - 1 + rows] * w[0:1] + a[HALO:HALO + rows] * w[1:2]
                + a[HALO + 1:HALO + 1 + rows] * w[2:3] + cb_ref[:, lo:lo + FFN_COLS])

    y = x
    for c in range(D_FF // FFN_COLS):
        gate = jax.nn.gelu(conv(c * FFN_COLS)) * conv(D_FF + c * FFN_COLS)
        y = y + jnp.dot(gate.astype(BF16), wd_ref[c * FFN_COLS:(c + 1) * FFN_COLS, :],
                        preferred_element_type=F32)
    o_ref[...] = _rms(y, fg_ref[...]) if final else y


def _conv_ffn(x, layer, p, batch, seq_len, final):
    tok, _, _ = _tiles(seq_len)
    n_t = seq_len // tok
    per = tok // HALO
    n_halo = seq_len // HALO
    row = lambda b, i: (b * n_t + i, 0)
    prev = lambda b, i: (b * n_halo + jnp.maximum(i * per - 1, 0), 0)
    nxt = lambda b, i: (b * n_halo + jnp.minimum((i + 1) * per, n_halo - 1), 0)
    layer3 = lambda b, i: (layer, 0, 0)
    blocks = 2 * tok * D_MODEL * 4 + 3 * D_MODEL * D_FF * 2
    return pl.pallas_call(
        functools.partial(_conv_ffn_kernel, final=final),
        grid=(batch, n_t),
        in_specs=[
            pl.BlockSpec((HALO, D_MODEL), prev),
            pl.BlockSpec((tok, D_MODEL), row),
            pl.BlockSpec((HALO, D_MODEL), nxt),
            pl.BlockSpec((None, 1, D_MODEL), layer3),
            pl.BlockSpec((None, D_MODEL, 2 * D_FF), layer3),
            pl.BlockSpec((None, CONV_W, 2 * D_FF), layer3),
            pl.BlockSpec((None, 1, 2 * D_FF), layer3),
            pl.BlockSpec((None, D_FF, D_MODEL), layer3),
            pl.BlockSpec((1, D_MODEL), lambda b, i: (0, 0)),
        ],
        out_specs=pl.BlockSpec((tok, D_MODEL), row),
        out_shape=jax.ShapeDtypeStruct(x.shape, F32),
        compiler_params=pltpu.CompilerParams(
            dimension_semantics=("arbitrary", "arbitrary"),
            vmem_limit_bytes=_vmem_limit(blocks)),
        name="conv_ffn",
    )(x, x, x, p["ffn_norm_g"], p["w_up"], p["conv_w"], p["conv_b"], p["w_down"],
      p["final_norm_g"])


def _rope_tables(seq_len):
    t = jnp.arange(seq_len)
    half = HEAD_DIM // 2
    freq = ROPE_THETA ** (-jnp.arange(0, half, 2, dtype=F32) / half)
    ang_r = (t // GRID_W).astype(F32)[:, None] * freq[None, :]
    ang_c = (t % GRID_W).astype(F32)[:, None] * freq[None, :]
    cr, sr, cc, sc = jnp.cos(ang_r), jnp.sin(ang_r), jnp.cos(ang_c), jnp.sin(ang_c)
    zero = jnp.zeros_like(sr)
    per_head = lambda parts: jnp.tile(jnp.concatenate(parts, axis=1), (1, LANES // HEAD_DIM))
    cos = per_head([cr, cr, cc, cc])
    sa = per_head([-sr, zero, -sc, zero])
    sb = per_head([zero, sr, zero, sc])
    return cos, sa, sb


def _prepare_params(attn_norm_g, w_in, q_norm_g, k_norm_g, sg_norm_g, sg_w, sg_b,
                    w_branch_a, w_branch_b, w_mix_out, ffn_norm_g, w_up, conv_w, conv_b,
                    w_down, final_norm_g):
    depth = w_in.shape[0]
    order = np.concatenate([
        np.arange(h * HEAD_DIM, (h + 1) * HEAD_DIM)
        for j in range(Q_PER_KV) for h in (j, Q_PER_KV + j)])
    w_in_p = jnp.concatenate([w_in[:, :, order], w_in[:, :, Q_WIDTH:]], axis=2).astype(BF16)
    two_heads = lambda g: jnp.tile(g, (1, LANES // HEAD_DIM)).reshape(depth, 1, LANES)
    blk = np.arange(LANES) // HEAD_DIM
    return dict(
        attn_norm_g=attn_norm_g.reshape(depth, 1, D_MODEL),
        w_in=w_in_p,
        q_norm_g=two_heads(q_norm_g),
        k_norm_g=two_heads(k_norm_g),
        head_ones=jnp.asarray(blk[:, None] == blk[None, :], BF16),
        sg_norm_g=sg_norm_g.reshape(depth, 1, SG_WIDTH),
        sg_w=sg_w.astype(BF16),
        sg_bias=jnp.repeat(jnp.swapaxes(sg_b, 1, 2), SG_GROUP_DIM, axis=2),
        w_branch_a=w_branch_a.astype(BF16),
        w_branch_b=w_branch_b.astype(BF16),
        w_mix_out=w_mix_out.astype(BF16),
        ffn_norm_g=ffn_norm_g.reshape(depth, 1, D_MODEL),
        w_up=w_up.astype(BF16),
        conv_w=conv_w,
        conv_b=conv_b.reshape(depth, 1, 2 * D_FF),
        w_down=w_down.astype(BF16),
        final_norm_g=final_norm_g.reshape(1, D_MODEL),
    )


def _trunk(x3, p):
    batch, seq_len, _ = x3.shape
    depth = p["w_in"].shape[0]
    tok, tq, tk = _tiles(seq_len)
    assert seq_len % tok == 0 and seq_len % tq == 0 and seq_len % (2 * tk) == 0
    assert tok % (2 * CHUNK) == 0 and tok % tk == 0 and tok % HALO == 0
    p = dict(p)
    p["rope_cos"], p["rope_sa"], p["rope_sb"] = _rope_tables(seq_len)
    x = x3.reshape(batch * seq_len, D_MODEL)
    for layer in range(depth):
        q, kp, vt, sga, ybg = _in_proj(x, layer, p, seq_len)
        x = _attn_mix(x, q, kp, vt, sga, ybg, layer, p, batch, seq_len)
        x = _conv_ffn(x, layer, p, batch, seq_len, final=(layer == depth - 1))
    return x.reshape(batch, seq_len, D_MODEL)


def kernel(x_prompt, x_sample, attn_norm_g, w_in, q_norm_g, k_norm_g, sg_norm_g, sg_w, sg_b,
           w_branch_a, w_branch_b, w_mix_out, ffn_norm_g, w_up, conv_w, conv_b, w_down,
           final_norm_g):
    p = _prepare_params(attn_norm_g, w_in, q_norm_g, k_norm_g, sg_norm_g, sg_w, sg_b,
                        w_branch_a, w_branch_b, w_mix_out, ffn_norm_g, w_up, conv_w, conv_b,
                        w_down, final_norm_g)
    return (_trunk(x_prompt, p), _trunk(x_sample, p))
```

```python
import functools
import math

import jax
import jax.numpy as jnp
import numpy as np
from jax import lax
from jax.experimental import pallas as pl
from jax.experimental.pallas import tpu as pltpu

D_MODEL = 1024
GRID_W = 64
HEAD_DIM = 64
N_Q_HEADS = 8
N_KV_HEADS = 2
Q_PER_KV = N_Q_HEADS // N_KV_HEADS
Q_WIDTH = N_Q_HEADS * HEAD_DIM
KV_WIDTH = N_KV_HEADS * HEAD_DIM
SG_GROUPS = 8
SG_GROUP_DIM = 64
SG_WIDTH = SG_GROUPS * SG_GROUP_DIM
CHUNK = 128
D_FF = 2048
CONV_W = 3
ROPE_THETA = 10000.0
EPS = 1e-6
IN_WIDTH = Q_WIDTH + 2 * KV_WIDTH + 2 * SG_WIDTH + 2 * D_MODEL

LANES = 128
BF16_SUBLANES = 16
V7X_VMEM_BYTES = 64 * 1024 * 1024

_OFF_K = Q_WIDTH
_OFF_V = _OFF_K + KV_WIDTH
_OFF_U = _OFF_V + KV_WIDTH
_OFF_VS = _OFF_U + SG_WIDTH
_OFF_GA = _OFF_VS + SG_WIDTH
_OFF_GB = _OFF_GA + D_MODEL

VT_ROWS = HEAD_DIM + BF16_SUBLANES
NEG_BIG = -1e30
Q_SCALE = math.log2(math.e) / math.sqrt(HEAD_DIM)
HALO = BF16_SUBLANES

F32 = jnp.float32
BF16 = jnp.bfloat16


def _tiles(seq_len):
    tok = min(512, seq_len)
    tq = min(256, seq_len)
    tk = min(256, seq_len // 2)
    return tok, tq, tk


def _vmem_limit(block_bytes, scratch_bytes=0):
    return int(min(V7X_VMEM_BYTES - (4 << 20), 2 * block_bytes + scratch_bytes + (24 << 20)))


def _rms(x, g):
    return x * lax.rsqrt(jnp.mean(x * x, axis=-1, keepdims=True) + EPS) * g


def _gelu(x):
    c = math.sqrt(2.0 / math.pi)
    half = 0.5 * x
    return half + half * jnp.tanh(x * (c + (c * 0.044715) * (x * x)))


def _lane_lt64(shape):
    lane = lax.broadcasted_iota(jnp.int32, shape, len(shape) - 1)
    return (lane % LANES) < HEAD_DIM


def _in_proj_kernel(x_ref, ng_ref, w_ref, qg_ref, kg_ref, cos_ref, sa_ref, sb_ref,
                    bd_ref, sgg_ref, sgw_ref, sgb_ref, wb_ref,
                    q_ref, k_ref, vt_ref, sga_ref, ybg_ref, *, tk):
    rows = x_ref.shape[0]
    hb = _rms(x_ref[...], ng_ref[...]).astype(BF16)

    def proj(lo, hi):
        return jnp.dot(hb, w_ref[:, lo:hi], preferred_element_type=F32)

    cos = cos_ref[...]
    sa = sa_ref[...]
    sb = sb_ref[...]

    def head_norm_rope(z, gain, scale):
        ss = jnp.dot((z * z).astype(BF16), bd_ref[...], preferred_element_type=F32)
        zn = z * lax.rsqrt(ss * (1.0 / HEAD_DIM) + EPS) * gain
        rot = zn * cos + pltpu.roll(zn, LANES - 16, 1) * sa + pltpu.roll(zn, 16, 1) * sb
        return rot * scale if scale != 1.0 else rot

    zq = proj(0, Q_WIDTH)
    zkv = proj(_OFF_K, _OFF_U)

    for j in range(Q_WIDTH // LANES):
        q_ref[:, j * LANES:(j + 1) * LANES] = head_norm_rope(
            zq[:, j * LANES:(j + 1) * LANES], qg_ref[...], Q_SCALE).astype(BF16)

    zu = proj(_OFF_U, _OFF_VS)

    kr = head_norm_rope(zkv[:, :KV_WIDTH], kg_ref[...], 1.0)
    first = _lane_lt64(kr.shape)
    k_ref[0] = jnp.where(first, kr, 0.0).astype(BF16)
    k_ref[1] = jnp.where(first, 0.0, kr).astype(BF16)

    vt = zkv[:, KV_WIDTH:].T
    ones = jnp.ones((VT_ROWS - HEAD_DIM, tk), BF16)
    for g in range(N_KV_HEADS):
        for r in range(rows // tk):
            vt_ref[g, r, :HEAD_DIM, :] = vt[g * HEAD_DIM:(g + 1) * HEAD_DIM,
                                            r * tk:(r + 1) * tk].astype(BF16)
            vt_ref[g, r, HEAD_DIM:, :] = ones

    zvs = proj(_OFF_VS, _OFF_GA)
    u = _gelu(zu)
    zga = proj(_OFF_GA, _OFF_GB)

    vs = _rms(_gelu(zvs), sgg_ref[...]).astype(BF16)
    zgb = proj(_OFF_GB, IN_WIDTH)
    sga_ref[...] = jax.nn.sigmoid(zga).astype(BF16)
    pair = 2 * CHUNK
    lt64 = _lane_lt64((CHUNK, 2 * LANES))
    sgu_rows = []
    for c in range(rows // pair):
        r0 = c * pair
        slabs0, slabs1 = [], []
        for s in range(SG_WIDTH // LANES):
            cols = slice(s * LANES, (s + 1) * LANES)
            rhs = jnp.concatenate([vs[r0:r0 + CHUNK, cols], vs[r0 + CHUNK:r0 + pair, cols]], axis=1)
            oa = jnp.dot(sgw_ref[2 * s], rhs, preferred_element_type=F32)
            ob = jnp.dot(sgw_ref[2 * s + 1], rhs, preferred_element_type=F32)
            bias = sgb_ref[:, cols]
            sp = jnp.where(lt64, oa, ob) + jnp.concatenate([bias, bias], axis=1)
            slabs0.append(u[r0:r0 + CHUNK, cols] * sp[:, :LANES])
            slabs1.append(u[r0 + CHUNK:r0 + pair, cols] * sp[:, LANES:])
        sgu_rows.append(jnp.concatenate(slabs0, axis=1))
        sgu_rows.append(jnp.concatenate(slabs1, axis=1))
    sgu = jnp.concatenate(sgu_rows, axis=0).astype(BF16)
    sgb_gate = jax.nn.sigmoid(zgb)
    yb = jnp.dot(sgu, wb_ref[...], preferred_element_type=F32)
    ybg_ref[...] = (sgb_gate * yb).astype(BF16)


def _in_proj(x, layer, p, seq_len):
    n = x.shape[0]
    tok, _, tk = _tiles(seq_len)
    n_pos = seq_len // tok
    const2 = lambda i: (0, 0)
    layer3 = lambda i: (layer, 0, 0)
    blocks = (tok * D_MODEL * 4 + D_MODEL * IN_WIDTH * 2 + SG_WIDTH * D_MODEL * 2
              + tok * (Q_WIDTH + 2 * LANES + 2 * VT_ROWS + 2 * D_MODEL) * 2)
    return pl.pallas_call(
        functools.partial(_in_proj_kernel, tk=tk),
        grid=(n // tok,),
        in_specs=[
            pl.BlockSpec((tok, D_MODEL), lambda i: (i, 0)),
            pl.BlockSpec((None, 1, D_MODEL), layer3),
            pl.BlockSpec((None, D_MODEL, IN_WIDTH), layer3),
            pl.BlockSpec((None, 1, LANES), layer3),
            pl.BlockSpec((None, 1, LANES), layer3),
            pl.BlockSpec((tok, LANES), lambda i: (i % n_pos, 0)),
            pl.BlockSpec((tok, LANES), lambda i: (i % n_pos, 0)),
            pl.BlockSpec((tok, LANES), lambda i: (i % n_pos, 0)),
            pl.BlockSpec((LANES, LANES), const2),
            pl.BlockSpec((None, 1, SG_WIDTH), layer3),
            pl.BlockSpec((None, SG_GROUPS, CHUNK, CHUNK), lambda i: (layer, 0, 0, 0)),
            pl.BlockSpec((None, CHUNK, SG_WIDTH), layer3),
            pl.BlockSpec((None, SG_WIDTH, D_MODEL), layer3),
        ],
        out_specs=[
            pl.BlockSpec((tok, Q_WIDTH), lambda i: (i, 0)),
            pl.BlockSpec((N_KV_HEADS, tok, LANES), lambda i: (0, i, 0)),
            pl.BlockSpec((N_KV_HEADS, tok // tk, VT_ROWS, tk), lambda i: (0, i, 0, 0)),
            pl.BlockSpec((tok, D_MODEL), lambda i: (i, 0)),
            pl.BlockSpec((tok, D_MODEL), lambda i: (i, 0)),
        ],
        out_shape=[
            jax.ShapeDtypeStruct((n, Q_WIDTH), BF16),
            jax.ShapeDtypeStruct((N_KV_HEADS, n, LANES), BF16),
            jax.ShapeDtypeStruct((N_KV_HEADS, n // tk, VT_ROWS, tk), BF16),
            jax.ShapeDtypeStruct((n, D_MODEL), BF16),
            jax.ShapeDtypeStruct((n, D_MODEL), BF16),
        ],
        compiler_params=pltpu.CompilerParams(
            dimension_semantics=("arbitrary",), vmem_limit_bytes=_vmem_limit(blocks)),
        name="in_proj",
    )(x, p["attn_norm_g"], p["w_in"], p["q_norm_g"], p["k_norm_g"],
      p["rope_cos"], p["rope_sa"], p["rope_sb"], p["head_ones"],
      p["sg_norm_g"], p["sg_w"], p["sg_bias"], p["w_branch_b"])


def _attn_mix_kernel(q_ref, k_ref, vt_ref, sga_ref, ybg_ref, x_ref, wa_ref, wm_ref,
                     o_ref, acc_ref, m_ref, s0_ref, s1_ref, bm0_ref, bm1_ref, *, tk):
    tq = q_ref.shape[0]
    n_kv = k_ref.shape[1] // tk
    acc_ref[...] = jnp.zeros(acc_ref.shape, F32)
    m_ref[...] = jnp.full(m_ref.shape, NEG_BIG, F32)

    def scores(blk, s_ref, bm_ref, h):
        g, j = divmod(h, Q_PER_KV)
        start = pl.multiple_of(blk * tk, tk)
        kb = k_ref[g, pl.ds(start, tk), :]
        qs = q_ref[:, j * LANES:(j + 1) * LANES]
        s = lax.dot_general(kb, qs, (((1,), (1,)), ((), ())),
                            preferred_element_type=F32)
        s_ref[h] = s
        bm_ref[h:h + 1, :] = jnp.max(s, axis=0, keepdims=True)

    def accumulate(blk, s_ref, bm_ref, h):
        g = h // Q_PER_KV
        vb = vt_ref[g, blk]
        m_old = m_ref[h:h + 1, :]
        m_new = jnp.maximum(m_old, bm_ref[h:h + 1, :])
        alpha = jnp.exp2(m_old - m_new)
        pt = jnp.exp2(s_ref[h] - m_new).astype(BF16)
        pv = jnp.dot(vb, pt, preferred_element_type=F32)
        acc_ref[h] = alpha * acc_ref[h] + pv
        m_ref[h:h + 1, :] = m_new

    def step(score_blk, score_refs, acc_blk, acc_refs):
        for h in range(N_Q_HEADS):
            if score_blk is not None:
                scores(score_blk, *score_refs, h)
            if acc_blk is not None:
                accumulate(acc_blk, *acc_refs, h)

    even = (s0_ref, bm0_ref)
    odd = (s1_ref, bm1_ref)
    step(0, even, None, None)

    def kv_pair(t, carry):
        step(2 * t + 1, odd, 2 * t, even)
        step(2 * t + 2, even, 2 * t + 1, odd)
        return carry

    lax.fori_loop(0, n_kv // 2 - 1, kv_pair, 0)
    step(n_kv - 1, odd, n_kv - 2, even)
    step(None, None, n_kv - 1, odd)

    heads = []
    for h in range(N_Q_HEADS):
        a = acc_ref[h]
        heads.append(a[:HEAD_DIM] * (1.0 / a[HEAD_DIM:HEAD_DIM + 1]))
    ot = jnp.concatenate(heads, axis=0)
    ya = jnp.dot(ot.T.astype(BF16), wa_ref[...], preferred_element_type=F32)
    mix = sga_ref[...].astype(F32) * ya + ybg_ref[...].astype(F32)
    o_ref[...] = x_ref[...] + jnp.dot(mix.astype(BF16), wm_ref[...],
                                      preferred_element_type=F32)


def _attn_mix(x, q, kp, vt, sga, ybg, layer, p, batch, seq_len):
    _, tq, tk = _tiles(seq_len)
    n_kv = seq_len // tk
    n_q = seq_len // tq
    row = lambda b, i: (b * n_q + i, 0)
    blocks = (tq * (Q_WIDTH + 2 * D_MODEL) * 2 + 2 * tq * D_MODEL * 4
              + N_KV_HEADS * seq_len * (LANES + VT_ROWS) * 2
              + (Q_WIDTH + D_MODEL) * D_MODEL * 2)
    scratch = N_Q_HEADS * tq * 4 * (VT_ROWS + 2 * tk + 3)
    return pl.pallas_call(
        functools.partial(_attn_mix_kernel, tk=tk),
        grid=(batch, n_q),
        in_specs=[
            pl.BlockSpec((tq, Q_WIDTH), row),
            pl.BlockSpec((N_KV_HEADS, seq_len, LANES), lambda b, i: (0, b, 0)),
            pl.BlockSpec((N_KV_HEADS, n_kv, VT_ROWS, tk), lambda b, i: (0, b, 0, 0)),
            pl.BlockSpec((tq, D_MODEL), row),
            pl.BlockSpec((tq, D_MODEL), row),
            pl.BlockSpec((tq, D_MODEL), row),
            pl.BlockSpec((None, Q_WIDTH, D_MODEL), lambda b, i: (layer, 0, 0)),
            pl.BlockSpec((None, D_MODEL, D_MODEL), lambda b, i: (layer, 0, 0)),
        ],
        out_specs=pl.BlockSpec((tq, D_MODEL), row),
        out_shape=jax.ShapeDtypeStruct(x.shape, F32),
        scratch_shapes=[
            pltpu.VMEM((N_Q_HEADS, VT_ROWS, tq), F32),
            pltpu.VMEM((N_Q_HEADS, tq), F32),
            pltpu.VMEM((N_Q_HEADS, tk, tq), F32),
            pltpu.VMEM((N_Q_HEADS, tk, tq), F32),
            pltpu.VMEM((N_Q_HEADS, tq), F32),
            pltpu.VMEM((N_Q_HEADS, tq), F32),
        ],
        compiler_params=pltpu.CompilerParams(
            dimension_semantics=("arbitrary", "arbitrary"),
            vmem_limit_bytes=_vmem_limit(blocks, scratch)),
        name="attn_mix",
    )(q, kp, vt, sga, ybg, x, p["w_branch_a"], p["w_mix_out"])


FFN_COLS = 512


def _conv_ffn_kernel(xp_ref, x_ref, xn_ref, ng_ref, wu_ref, cw_ref, cb_ref, wd_ref, fg_ref,
                     o_ref, a_ref, *, final):
    i = pl.program_id(1)
    rows = x_ref.shape[0]
    x = x_ref[...]
    xp = jnp.where(i > 0, xp_ref[...], 0.0)
    xn = jnp.where(i < pl.num_programs(1) - 1, xn_ref[...], 0.0)
    hb = _rms(jnp.concatenate([xp, x, xn], axis=0), ng_ref[...]).astype(BF16)

    n_slab = FFN_COLS // LANES

    def up(lo):
        a = jnp.dot(hb, wu_ref[:, lo:lo + FFN_COLS], preferred_element_type=F32)
        for j in range(n_slab):
            a_ref[lo // LANES + j] = a[:, j * LANES:(j + 1) * LANES]

    def conv(lo):
        outs = []
        for j in range(n_slab):
            slab = lo // LANES + j
            cols = slice(lo + j * LANES, lo + (j + 1) * LANES)
            w = cw_ref[:, cols]
            outs.append(a_ref[slab, HALO - 1:HALO - 1 + rows, :] * w[0:1]
                        + a_ref[slab, HALO:HALO + rows, :] * w[1:2]
                        + a_ref[slab, HALO + 1:HALO + 1 + rows, :] * w[2:3] + cb_ref[:, cols])
        return jnp.concatenate(outs, axis=1)

    n_chunk = D_FF // FFN_COLS
    up(0)
    up(D_FF)
    y = x
    for c in range(n_chunk):
        if c + 1 < n_chunk:
            up((c + 1) * FFN_COLS)
            up(D_FF + (c + 1) * FFN_COLS)
        gate = _gelu(conv(c * FFN_COLS)) * conv(D_FF + c * FFN_COLS)
        y = y + jnp.dot(gate.astype(BF16), wd_ref[c * FFN_COLS:(c + 1) * FFN_COLS, :],
                        preferred_element_type=F32)
    o_ref[...] = _rms(y, fg_ref[...]) if final else y


def _conv_ffn(x, layer, p, batch, seq_len, final):
    tok, _, _ = _tiles(seq_len)
    n_t = seq_len // tok
    per = tok // HALO
    n_halo = seq_len // HALO
    row = lambda b, i: (b * n_t + i, 0)
    prev = lambda b, i: (b * n_halo + jnp.maximum(i * per - 1, 0), 0)
    nxt = lambda b, i: (b * n_halo + jnp.minimum((i + 1) * per, n_halo - 1), 0)
    layer3 = lambda b, i: (layer, 0, 0)
    blocks = 2 * tok * D_MODEL * 4 + 3 * D_MODEL * D_FF * 2
    return pl.pallas_call(
        functools.partial(_conv_ffn_kernel, final=final),
        grid=(batch, n_t),
        in_specs=[
            pl.BlockSpec((HALO, D_MODEL), prev),
            pl.BlockSpec((tok, D_MODEL), row),
            pl.BlockSpec((HALO, D_MODEL), nxt),
            pl.BlockSpec((None, 1, D_MODEL), layer3),
            pl.BlockSpec((None, D_MODEL, 2 * D_FF), layer3),
            pl.BlockSpec((None, CONV_W, 2 * D_FF), layer3),
            pl.BlockSpec((None, 1, 2 * D_FF), layer3),
            pl.BlockSpec((None, D_FF, D_MODEL), layer3),
            pl.BlockSpec((1, D_MODEL), lambda b, i: (0, 0)),
        ],
        out_specs=pl.BlockSpec((tok, D_MODEL), row),
        out_shape=jax.ShapeDtypeStruct(x.shape, F32),
        scratch_shapes=[pltpu.VMEM((2 * D_FF // LANES, tok + 2 * HALO, LANES), F32)],
        compiler_params=pltpu.CompilerParams(
            dimension_semantics=("arbitrary", "arbitrary"),
            vmem_limit_bytes=_vmem_limit(blocks, 2 * D_FF * (tok + 2 * HALO) * 4)),
        name="conv_ffn",
    )(x, x, x, p["ffn_norm_g"], p["w_up"], p["conv_w"], p["conv_b"], p["w_down"],
      p["final_norm_g"])


def _rope_tables(seq_len):
    t = jnp.arange(seq_len)
    half = HEAD_DIM // 2
    freq = ROPE_THETA ** (-jnp.arange(0, half, 2, dtype=F32) / half)
    ang_r = (t // GRID_W).astype(F32)[:, None] * freq[None, :]
    ang_c = (t % GRID_W).astype(F32)[:, None] * freq[None, :]
    cr, sr, cc, sc = jnp.cos(ang_r), jnp.sin(ang_r), jnp.cos(ang_c), jnp.sin(ang_c)
    zero = jnp.zeros_like(sr)
    per_head = lambda parts: jnp.tile(jnp.concatenate(parts, axis=1), (1, LANES // HEAD_DIM))
    cos = per_head([cr, cr, cc, cc])
    sa = per_head([-sr, zero, -sc, zero])
    sb = per_head([zero, sr, zero, sc])
    return cos, sa, sb


def _prepare_params(attn_norm_g, w_in, q_norm_g, k_norm_g, sg_norm_g, sg_w, sg_b,
                    w_branch_a, w_branch_b, w_mix_out, ffn_norm_g, w_up, conv_w, conv_b,
                    w_down, final_norm_g):
    depth = w_in.shape[0]
    order = np.concatenate([
        np.arange(h * HEAD_DIM, (h + 1) * HEAD_DIM)
        for j in range(Q_PER_KV) for h in (j, Q_PER_KV + j)])
    w_in_p = jnp.concatenate([w_in[:, :, order], w_in[:, :, Q_WIDTH:]], axis=2).astype(BF16)
    two_heads = lambda g: jnp.tile(g, (1, LANES // HEAD_DIM)).reshape(depth, 1, LANES)
    blk = np.arange(LANES) // HEAD_DIM
    return dict(
        attn_norm_g=attn_norm_g.reshape(depth, 1, D_MODEL),
        w_in=w_in_p,
        q_norm_g=two_heads(q_norm_g),
        k_norm_g=two_heads(k_norm_g),
        head_ones=jnp.asarray(blk[:, None] == blk[None, :], BF16),
        sg_norm_g=sg_norm_g.reshape(depth, 1, SG_WIDTH),
        sg_w=sg_w.astype(BF16),
        sg_bias=jnp.repeat(jnp.swapaxes(sg_b, 1, 2), SG_GROUP_DIM, axis=2),
        w_branch_a=w_branch_a.astype(BF16),
        w_branch_b=w_branch_b.astype(BF16),
        w_mix_out=w_mix_out.astype(BF16),
        ffn_norm_g=ffn_norm_g.reshape(depth, 1, D_MODEL),
        w_up=w_up.astype(BF16),
        conv_w=conv_w,
        conv_b=conv_b.reshape(depth, 1, 2 * D_FF),
        w_down=w_down.astype(BF16),
        final_norm_g=final_norm_g.reshape(1, D_MODEL),
    )


def _trunk(x3, p):
    batch, seq_len, _ = x3.shape
    depth = p["w_in"].shape[0]
    tok, tq, tk = _tiles(seq_len)
    assert seq_len % tok == 0 and seq_len % tq == 0 and seq_len % (2 * tk) == 0
    assert tok % (2 * CHUNK) == 0 and tok % tk == 0 and tok % HALO == 0
    p = dict(p)
    p["rope_cos"], p["rope_sa"], p["rope_sb"] = _rope_tables(seq_len)
    x = x3.reshape(batch * seq_len, D_MODEL)
    for layer in range(depth):
        q, kp, vt, sga, ybg = _in_proj(x, layer, p, seq_len)
        x = _attn_mix(x, q, kp, vt, sga, ybg, layer, p, batch, seq_len)
        x = _conv_ffn(x, layer, p, batch, seq_len, final=(layer == depth - 1))
    return x.reshape(batch, seq_len, D_MODEL)


def kernel(x_prompt, x_sample, attn_norm_g, w_in, q_norm_g, k_norm_g, sg_norm_g, sg_w, sg_b,
           w_branch_a, w_branch_b, w_mix_out, ffn_norm_g, w_up, conv_w, conv_b, w_down,
           final_norm_g):
    p = _prepare_params(attn_norm_g, w_in, q_norm_g, k_norm_g, sg_norm_g, sg_w, sg_b,
                        w_branch_a, w_branch_b, w_mix_out, ffn_norm_g, w_up, conv_w, conv_b,
                        w_down, final_norm_g)
    return (_trunk(x_prompt, p), _trunk(x_sample, p))
```

```python
import functools
import math

import jax
import jax.numpy as jnp
import numpy as np
from jax import lax
from jax.experimental import pallas as pl
from jax.experimental.pallas import tpu as pltpu

D_MODEL = 1024
GRID_W = 64
HEAD_DIM = 64
N_Q_HEADS = 8
N_KV_HEADS = 2
Q_PER_KV = N_Q_HEADS // N_KV_HEADS
Q_WIDTH = N_Q_HEADS * HEAD_DIM
KV_WIDTH = N_KV_HEADS * HEAD_DIM
SG_GROUPS = 8
SG_GROUP_DIM = 64
SG_WIDTH = SG_GROUPS * SG_GROUP_DIM
CHUNK = 128
D_FF = 2048
CONV_W = 3
ROPE_THETA = 10000.0
EPS = 1e-6
IN_WIDTH = Q_WIDTH + 2 * KV_WIDTH + 2 * SG_WIDTH + 2 * D_MODEL

LANES = 128
F32_SUBLANES = 8
BF16_SUBLANES = 16
V7X_VMEM_BYTES = 64 * 1024 * 1024

_OFF_K = Q_WIDTH
_OFF_V = _OFF_K + KV_WIDTH
_OFF_U = _OFF_V + KV_WIDTH
_OFF_VS = _OFF_U + SG_WIDTH
_OFF_GA = _OFF_VS + SG_WIDTH
_OFF_GB = _OFF_GA + D_MODEL

VT_ROWS = HEAD_DIM + BF16_SUBLANES
NEG_BIG = -1e30
Q_SCALE = math.log2(math.e) / math.sqrt(HEAD_DIM)
HALO = F32_SUBLANES

F32 = jnp.float32
BF16 = jnp.bfloat16


def _tiles(seq_len):
    tok = min(512, seq_len)
    tq = min(512, seq_len)
    tk = min(256, seq_len // 2)
    return tok, tq, tk


def _vmem_limit(block_bytes, scratch_bytes=0):
    return int(min(V7X_VMEM_BYTES - (4 << 20), 2 * block_bytes + scratch_bytes + (24 << 20)))


def _rms(x, g):
    return x * lax.rsqrt(jnp.mean(x * x, axis=-1, keepdims=True) + EPS) * g


def _gelu(x):
    c = math.sqrt(2.0 / math.pi)
    half = 0.5 * x
    return half + half * jnp.tanh(x * (c + (c * 0.044715) * (x * x)))


def _lane_lt64(shape):
    lane = lax.broadcasted_iota(jnp.int32, shape, len(shape) - 1)
    return (lane % LANES) < HEAD_DIM


def _in_proj_kernel(x_ref, ng_ref, w_ref, qg_ref, kg_ref, cos_ref, sa_ref, sb_ref,
                    bd_ref, sgg_ref, sgw_ref, sgb_ref, wb_ref,
                    q_ref, k_ref, vt_ref, sga_ref, ybg_ref, *, tk):
    rows = x_ref.shape[0]
    hb = _rms(x_ref[...], ng_ref[...]).astype(BF16)

    def proj(lo, hi):
        return jnp.dot(hb, w_ref[:, lo:hi], preferred_element_type=F32)

    cos = cos_ref[...]
    sa = sa_ref[...]
    sb = sb_ref[...]

    def head_norm_rope(z, gain, scale):
        ss = jnp.dot((z * z).astype(BF16), bd_ref[...], preferred_element_type=F32)
        zn = z * lax.rsqrt(ss * (1.0 / HEAD_DIM) + EPS) * gain
        rot = zn * cos + pltpu.roll(zn, LANES - 16, 1) * sa + pltpu.roll(zn, 16, 1) * sb
        return rot * scale if scale != 1.0 else rot

    zq = proj(0, Q_WIDTH)
    zkv = proj(_OFF_K, _OFF_U)

    for j in range(Q_WIDTH // LANES):
        q_ref[:, j * LANES:(j + 1) * LANES] = head_norm_rope(
            zq[:, j * LANES:(j + 1) * LANES], qg_ref[...], Q_SCALE).astype(BF16)

    zu = proj(_OFF_U, _OFF_VS)

    kr = head_norm_rope(zkv[:, :KV_WIDTH], kg_ref[...], 1.0)
    first = _lane_lt64(kr.shape)
    k_ref[0] = jnp.where(first, kr, 0.0).astype(BF16)
    k_ref[1] = jnp.where(first, 0.0, kr).astype(BF16)

    vt = zkv[:, KV_WIDTH:].T
    ones = jnp.ones((VT_ROWS - HEAD_DIM, tk), BF16)
    for g in range(N_KV_HEADS):
        for r in range(rows // tk):
            vt_ref[g, r, :HEAD_DIM, :] = vt[g * HEAD_DIM:(g + 1) * HEAD_DIM,
                                            r * tk:(r + 1) * tk].astype(BF16)
            vt_ref[g, r, HEAD_DIM:, :] = ones

    zvs = proj(_OFF_VS, _OFF_GA)
    u = _gelu(zu)
    zga = proj(_OFF_GA, _OFF_GB)

    vs = _rms(_gelu(zvs), sgg_ref[...]).astype(BF16)
    zgb = proj(_OFF_GB, IN_WIDTH)
    sga_ref[...] = jax.nn.sigmoid(zga).astype(BF16)
    pair = 2 * CHUNK
    lt64 = _lane_lt64((CHUNK, 2 * LANES))
    sgu_rows = []
    for c in range(rows // pair):
        r0 = c * pair
        slabs0, slabs1 = [], []
        for s in range(SG_WIDTH // LANES):
            cols = slice(s * LANES, (s + 1) * LANES)
            rhs = jnp.concatenate([vs[r0:r0 + CHUNK, cols], vs[r0 + CHUNK:r0 + pair, cols]], axis=1)
            oa = jnp.dot(sgw_ref[2 * s], rhs, preferred_element_type=F32)
            ob = jnp.dot(sgw_ref[2 * s + 1], rhs, preferred_element_type=F32)
            bias = sgb_ref[:, cols]
            sp = jnp.where(lt64, oa, ob) + jnp.concatenate([bias, bias], axis=1)
            slabs0.append(u[r0:r0 + CHUNK, cols] * sp[:, :LANES])
            slabs1.append(u[r0 + CHUNK:r0 + pair, cols] * sp[:, LANES:])
        sgu_rows.append(jnp.concatenate(slabs0, axis=1))
        sgu_rows.append(jnp.concatenate(slabs1, axis=1))
    sgu = jnp.concatenate(sgu_rows, axis=0).astype(BF16)
    sgb_gate = jax.nn.sigmoid(zgb)
    yb = jnp.dot(sgu, wb_ref[...], preferred_element_type=F32)
    ybg_ref[...] = (sgb_gate * yb).astype(BF16)


def _in_proj(x, layer, p, seq_len):
    n = x.shape[0]
    tok, _, tk = _tiles(seq_len)
    n_pos = seq_len // tok
    const2 = lambda i: (0, 0)
    layer3 = lambda i: (layer, 0, 0)
    blocks = (tok * D_MODEL * 4 + D_MODEL * IN_WIDTH * 2 + SG_WIDTH * D_MODEL * 2
              + tok * (Q_WIDTH + 2 * LANES + 2 * VT_ROWS + 2 * D_MODEL) * 2)
    return pl.pallas_call(
        functools.partial(_in_proj_kernel, tk=tk),
        grid=(n // tok,),
        in_specs=[
            pl.BlockSpec((tok, D_MODEL), lambda i: (i, 0)),
            pl.BlockSpec((None, 1, D_MODEL), layer3),
            pl.BlockSpec((None, D_MODEL, IN_WIDTH), layer3),
            pl.BlockSpec((None, 1, LANES), layer3),
            pl.BlockSpec((None, 1, LANES), layer3),
            pl.BlockSpec((tok, LANES), lambda i: (i % n_pos, 0)),
            pl.BlockSpec((tok, LANES), lambda i: (i % n_pos, 0)),
            pl.BlockSpec((tok, LANES), lambda i: (i % n_pos, 0)),
            pl.BlockSpec((LANES, LANES), const2),
            pl.BlockSpec((None, 1, SG_WIDTH), layer3),
            pl.BlockSpec((None, SG_GROUPS, CHUNK, CHUNK), lambda i: (layer, 0, 0, 0)),
            pl.BlockSpec((None, CHUNK, SG_WIDTH), layer3),
            pl.BlockSpec((None, SG_WIDTH, D_MODEL), layer3),
        ],
        out_specs=[
            pl.BlockSpec((tok, Q_WIDTH), lambda i: (i, 0)),
            pl.BlockSpec((N_KV_HEADS, tok, LANES), lambda i: (0, i, 0)),
            pl.BlockSpec((N_KV_HEADS, tok // tk, VT_ROWS, tk), lambda i: (0, i, 0, 0)),
            pl.BlockSpec((tok, D_MODEL), lambda i: (i, 0)),
            pl.BlockSpec((tok, D_MODEL), lambda i: (i, 0)),
        ],
        out_shape=[
            jax.ShapeDtypeStruct((n, Q_WIDTH), BF16),
            jax.ShapeDtypeStruct((N_KV_HEADS, n, LANES), BF16),
            jax.ShapeDtypeStruct((N_KV_HEADS, n // tk, VT_ROWS, tk), BF16),
            jax.ShapeDtypeStruct((n, D_MODEL), BF16),
            jax.ShapeDtypeStruct((n, D_MODEL), BF16),
        ],
        compiler_params=pltpu.CompilerParams(
            dimension_semantics=("arbitrary",), vmem_limit_bytes=_vmem_limit(blocks)),
        name="in_proj",
    )(x, p["attn_norm_g"], p["w_in"], p["q_norm_g"], p["k_norm_g"],
      p["rope_cos"], p["rope_sa"], p["rope_sb"], p["head_ones"],
      p["sg_norm_g"], p["sg_w"], p["sg_bias"], p["w_branch_b"])


def _attn_mix_kernel(q_ref, k_ref, vt_ref, sga_ref, ybg_ref, x_ref, wa_ref, wm_ref,
                     o_ref, acc_ref, m_ref, s0_ref, s1_ref, bm0_ref, bm1_ref, *, tk):
    tq = q_ref.shape[0]
    n_kv = k_ref.shape[1] // tk
    acc_ref[...] = jnp.zeros(acc_ref.shape, F32)
    m_ref[...] = jnp.full(m_ref.shape, NEG_BIG, F32)

    def scores(blk, s_ref, bm_ref, h):
        g, j = divmod(h, Q_PER_KV)
        start = pl.multiple_of(blk * tk, tk)
        kb = k_ref[g, pl.ds(start, tk), :]
        qs = q_ref[:, j * LANES:(j + 1) * LANES]
        s = lax.dot_general(kb, qs, (((1,), (1,)), ((), ())),
                            preferred_element_type=F32)
        s_ref[h] = s
        bm_ref[h:h + 1, :] = jnp.max(s, axis=0, keepdims=True)

    def accumulate(blk, s_ref, bm_ref, h):
        g = h // Q_PER_KV
        vb = vt_ref[g, blk]
        m_old = m_ref[h:h + 1, :]
        m_new = jnp.maximum(m_old, bm_ref[h:h + 1, :])
        alpha = jnp.exp2(m_old - m_new)
        pt = jnp.exp2(s_ref[h] - m_new).astype(BF16)
        pv = jnp.dot(vb, pt, preferred_element_type=F32)
        acc_ref[h] = alpha * acc_ref[h] + pv
        m_ref[h:h + 1, :] = m_new

    def step(score_blk, score_refs, acc_blk, acc_refs):
        for h in range(N_Q_HEADS):
            if score_blk is not None:
                scores(score_blk, *score_refs, h)
            if acc_blk is not None:
                accumulate(acc_blk, *acc_refs, h)

    even = (s0_ref, bm0_ref)
    odd = (s1_ref, bm1_ref)
    step(0, even, None, None)

    def kv_pair(t, carry):
        step(2 * t + 1, odd, 2 * t, even)
        step(2 * t + 2, even, 2 * t + 1, odd)
        return carry

    lax.fori_loop(0, n_kv // 2 - 1, kv_pair, 0)
    step(n_kv - 1, odd, n_kv - 2, even)
    step(None, None, n_kv - 1, odd)

    heads = []
    for h in range(N_Q_HEADS):
        a = acc_ref[h]
        heads.append(a[:HEAD_DIM] * (1.0 / a[HEAD_DIM:HEAD_DIM + 1]))
    ot = jnp.concatenate(heads, axis=0)
    ya = jnp.dot(ot.T.astype(BF16), wa_ref[...], preferred_element_type=F32)
    mix = sga_ref[...].astype(F32) * ya + ybg_ref[...].astype(F32)
    o_ref[...] = x_ref[...] + jnp.dot(mix.astype(BF16), wm_ref[...],
                                      preferred_element_type=F32)


def _attn_mix(x, q, kp, vt, sga, ybg, layer, p, batch, seq_len):
    _, tq, tk = _tiles(seq_len)
    n_kv = seq_len // tk
    n_q = seq_len // tq
    row = lambda b, i: (b * n_q + i, 0)
    blocks = (tq * (Q_WIDTH + 2 * D_MODEL) * 2 + 2 * tq * D_MODEL * 4
              + N_KV_HEADS * seq_len * (LANES + VT_ROWS) * 2
              + (Q_WIDTH + D_MODEL) * D_MODEL * 2)
    scratch = N_Q_HEADS * tq * 4 * (VT_ROWS + 2 * tk + 3)
    return pl.pallas_call(
        functools.partial(_attn_mix_kernel, tk=tk),
        grid=(batch, n_q),
        in_specs=[
            pl.BlockSpec((tq, Q_WIDTH), row),
            pl.BlockSpec((N_KV_HEADS, seq_len, LANES), lambda b, i: (0, b, 0)),
            pl.BlockSpec((N_KV_HEADS, n_kv, VT_ROWS, tk), lambda b, i: (0, b, 0, 0)),
            pl.BlockSpec((tq, D_MODEL), row),
            pl.BlockSpec((tq, D_MODEL), row),
            pl.BlockSpec((tq, D_MODEL), row),
            pl.BlockSpec((None, Q_WIDTH, D_MODEL), lambda b, i: (layer, 0, 0)),
            pl.BlockSpec((None, D_MODEL, D_MODEL), lambda b, i: (layer, 0, 0)),
        ],
        out_specs=pl.BlockSpec((tq, D_MODEL), row),
        out_shape=jax.ShapeDtypeStruct(x.shape, F32),
        scratch_shapes=[
            pltpu.VMEM((N_Q_HEADS, VT_ROWS, tq), F32),
            pltpu.VMEM((N_Q_HEADS, tq), F32),
            pltpu.VMEM((N_Q_HEADS, tk, tq), F32),
            pltpu.VMEM((N_Q_HEADS, tk, tq), F32),
            pltpu.VMEM((N_Q_HEADS, tq), F32),
            pltpu.VMEM((N_Q_HEADS, tq), F32),
        ],
        compiler_params=pltpu.CompilerParams(
            dimension_semantics=("arbitrary", "arbitrary"),
            vmem_limit_bytes=_vmem_limit(blocks, scratch)),
        name="attn_mix",
    )(q, kp, vt, sga, ybg, x, p["w_branch_a"], p["w_mix_out"])


FFN_COLS = 512


def _conv_ffn_kernel(xp_ref, x_ref, xn_ref, ng_ref, wu_ref, cw_ref, cb_ref, wd_ref, fg_ref,
                     o_ref, a_ref, *, final):
    i = pl.program_id(1)
    rows = x_ref.shape[0]
    x = x_ref[...]
    xp = jnp.where(i > 0, xp_ref[...], 0.0)
    xn = jnp.where(i < pl.num_programs(1) - 1, xn_ref[...], 0.0)
    hb = _rms(jnp.concatenate([xp, x, xn], axis=0), ng_ref[...]).astype(BF16)

    n_slab = FFN_COLS // LANES

    def up(lo):
        a = jnp.dot(hb, wu_ref[:, lo:lo + FFN_COLS], preferred_element_type=F32)
        for j in range(n_slab):
            a_ref[lo // LANES + j] = a[:, j * LANES:(j + 1) * LANES]

    def conv(lo):
        outs = []
        for j in range(n_slab):
            slab = lo // LANES + j
            cols = slice(lo + j * LANES, lo + (j + 1) * LANES)
            w = cw_ref[:, cols]
            outs.append(a_ref[slab, HALO - 1:HALO - 1 + rows, :] * w[0:1]
                        + a_ref[slab, HALO:HALO + rows, :] * w[1:2]
                        + a_ref[slab, HALO + 1:HALO + 1 + rows, :] * w[2:3] + cb_ref[:, cols])
        return jnp.concatenate(outs, axis=1)

    n_chunk = D_FF // FFN_COLS
    up(0)
    up(D_FF)
    y = x
    for c in range(n_chunk):
        if c + 1 < n_chunk:
            up((c + 1) * FFN_COLS)
            up(D_FF + (c + 1) * FFN_COLS)
        gate = _gelu(conv(c * FFN_COLS)) * conv(D_FF + c * FFN_COLS)
        y = y + jnp.dot(gate.astype(BF16), wd_ref[c * FFN_COLS:(c + 1) * FFN_COLS, :],
                        preferred_element_type=F32)
    o_ref[...] = _rms(y, fg_ref[...]) if final else y


def _conv_ffn(x, layer, p, batch, seq_len, final):
    tok, _, _ = _tiles(seq_len)
    n_t = seq_len // tok
    per = tok // HALO
    n_halo = seq_len // HALO
    row = lambda b, i: (b * n_t + i, 0)
    prev = lambda b, i: (b * n_halo + jnp.maximum(i * per - 1, 0), 0)
    nxt = lambda b, i: (b * n_halo + jnp.minimum((i + 1) * per, n_halo - 1), 0)
    layer3 = lambda b, i: (layer, 0, 0)
    blocks = 2 * tok * D_MODEL * 4 + 3 * D_MODEL * D_FF * 2
    return pl.pallas_call(
        functools.partial(_conv_ffn_kernel, final=final),
        grid=(batch, n_t),
        in_specs=[
            pl.BlockSpec((HALO, D_MODEL), prev),
            pl.BlockSpec((tok, D_MODEL), row),
            pl.BlockSpec((HALO, D_MODEL), nxt),
            pl.BlockSpec((None, 1, D_MODEL), layer3),
            pl.BlockSpec((None, D_MODEL, 2 * D_FF), layer3),
            pl.BlockSpec((None, CONV_W, 2 * D_FF), layer3),
            pl.BlockSpec((None, 1, 2 * D_FF), layer3),
            pl.BlockSpec((None, D_FF, D_MODEL), layer3),
            pl.BlockSpec((1, D_MODEL), lambda b, i: (0, 0)),
        ],
        out_specs=pl.BlockSpec((tok, D_MODEL), row),
        out_shape=jax.ShapeDtypeStruct(x.shape, F32),
        scratch_shapes=[pltpu.VMEM((2 * D_FF // LANES, tok + 2 * HALO, LANES), F32)],
        compiler_params=pltpu.CompilerParams(
            dimension_semantics=("arbitrary", "arbitrary"),
            vmem_limit_bytes=_vmem_limit(blocks, 2 * D_FF * (tok + 2 * HALO) * 4)),
        name="conv_ffn",
    )(x, x, x, p["ffn_norm_g"], p["w_up"], p["conv_w"], p["conv_b"], p["w_down"],
      p["final_norm_g"])


def _rope_tables(seq_len):
    t = jnp.arange(seq_len)
    half = HEAD_DIM // 2
    freq = ROPE_THETA ** (-jnp.arange(0, half, 2, dtype=F32) / half)
    ang_r = (t // GRID_W).astype(F32)[:, None] * freq[None, :]
    ang_c = (t % GRID_W).astype(F32)[:, None] * freq[None, :]
    cr, sr, cc, sc = jnp.cos(ang_r), jnp.sin(ang_r), jnp.cos(ang_c), jnp.sin(ang_c)
    zero = jnp.zeros_like(sr)
    per_head = lambda parts: jnp.tile(jnp.concatenate(parts, axis=1), (1, LANES // HEAD_DIM))
    cos = per_head([cr, cr, cc, cc])
    sa = per_head([-sr, zero, -sc, zero])
    sb = per_head([zero, sr, zero, sc])
    return cos, sa, sb


def _prepare_params(attn_norm_g, w_in, q_norm_g, k_norm_g, sg_norm_g, sg_w, sg_b,
                    w_branch_a, w_branch_b, w_mix_out, ffn_norm_g, w_up, conv_w, conv_b,
                    w_down, final_norm_g):
    depth = w_in.shape[0]
    order = np.concatenate([
        np.arange(h * HEAD_DIM, (h + 1) * HEAD_DIM)
        for j in range(Q_PER_KV) for h in (j, Q_PER_KV + j)])
    w_in_p = jnp.concatenate([w_in[:, :, order], w_in[:, :, Q_WIDTH:]], axis=2).astype(BF16)
    two_heads = lambda g: jnp.tile(g, (1, LANES // HEAD_DIM)).reshape(depth, 1, LANES)
    blk = np.arange(LANES) // HEAD_DIM
    return dict(
        attn_norm_g=attn_norm_g.reshape(depth, 1, D_MODEL),
        w_in=w_in_p,
        q_norm_g=two_heads(q_norm_g),
        k_norm_g=two_heads(k_norm_g),
        head_ones=jnp.asarray(blk[:, None] == blk[None, :], BF16),
        sg_norm_g=sg_norm_g.reshape(depth, 1, SG_WIDTH),
        sg_w=sg_w.astype(BF16),
        sg_bias=jnp.repeat(jnp.swapaxes(sg_b, 1, 2), SG_GROUP_DIM, axis=2),
        w_branch_a=w_branch_a.astype(BF16),
        w_branch_b=w_branch_b.astype(BF16),
        w_mix_out=w_mix_out.astype(BF16),
        ffn_norm_g=ffn_norm_g.reshape(depth, 1, D_MODEL),
        w_up=w_up.astype(BF16),
        conv_w=conv_w,
        conv_b=conv_b.reshape(depth, 1, 2 * D_FF),
        w_down=w_down.astype(BF16),
        final_norm_g=final_norm_g.reshape(1, D_MODEL),
    )


def _trunk(x3, p):
    batch, seq_len, _ = x3.shape
    depth = p["w_in"].shape[0]
    tok, tq, tk = _tiles(seq_len)
    assert seq_len % tok == 0 and seq_len % tq == 0 and seq_len % (2 * tk) == 0
    assert tok % (2 * CHUNK) == 0 and tok % tk == 0 and tok % HALO == 0
    p = dict(p)
    p["rope_cos"], p["rope_sa"], p["rope_sb"] = _rope_tables(seq_len)
    x = x3.reshape(batch * seq_len, D_MODEL)
    for layer in range(depth):
        q, kp, vt, sga, ybg = _in_proj(x, layer, p, seq_len)
        x = _attn_mix(x, q, kp, vt, sga, ybg, layer, p, batch, seq_len)
        x = _conv_ffn(x, layer, p, batch, seq_len, final=(layer == depth - 1))
    return x.reshape(batch, seq_len, D_MODEL)


def kernel(x_prompt, x_sample, attn_norm_g, w_in, q_norm_g, k_norm_g, sg_norm_g, sg_w, sg_b,
           w_branch_a, w_branch_b, w_mix_out, ffn_norm_g, w_up, conv_w, conv_b, w_down,
           final_norm_g):
    p = _prepare_params(attn_norm_g, w_in, q_norm_g, k_norm_g, sg_norm_g, sg_w, sg_b,
                        w_branch_a, w_branch_b, w_mix_out, ffn_norm_g, w_up, conv_w, conv_b,
                        w_down, final_norm_g)
    return (_trunk(x_prompt, p), _trunk(x_sample, p))
```

```python
import functools
import math

import jax
import jax.numpy as jnp
import numpy as np
from jax import lax
from jax.experimental import pallas as pl
from jax.experimental.pallas import tpu as pltpu

D_MODEL = 1024
GRID_W = 64
HEAD_DIM = 64
N_Q_HEADS = 8
N_KV_HEADS = 2
Q_PER_KV = N_Q_HEADS // N_KV_HEADS
Q_WIDTH = N_Q_HEADS * HEAD_DIM
KV_WIDTH = N_KV_HEADS * HEAD_DIM
SG_GROUPS = 8
SG_GROUP_DIM = 64
SG_WIDTH = SG_GROUPS * SG_GROUP_DIM
CHUNK = 128
D_FF = 2048
CONV_W = 3
ROPE_THETA = 10000.0
EPS = 1e-6
IN_WIDTH = Q_WIDTH + 2 * KV_WIDTH + 2 * SG_WIDTH + 2 * D_MODEL

LANES = 128
F32_SUBLANES = 8
BF16_SUBLANES = 16
V7X_VMEM_BYTES = 64 * 1024 * 1024

_OFF_K = Q_WIDTH
_OFF_V = _OFF_K + KV_WIDTH
_OFF_U = _OFF_V + KV_WIDTH
_OFF_VS = _OFF_U + SG_WIDTH
_OFF_GA = _OFF_VS + SG_WIDTH
_OFF_GB = _OFF_GA + D_MODEL

VT_ROWS = HEAD_DIM + BF16_SUBLANES
NEG_BIG = -1e30
Q_SCALE = math.log2(math.e) / math.sqrt(HEAD_DIM)
HALO = F32_SUBLANES

F32 = jnp.float32
BF16 = jnp.bfloat16


def _tiles(seq_len):
    tok = min(512, seq_len)
    tq = min(512, seq_len)
    tk = min(256, seq_len // 2)
    return tok, tq, tk


def _vmem_limit(block_bytes, scratch_bytes=0):
    return int(min(V7X_VMEM_BYTES - (4 << 20), 2 * block_bytes + scratch_bytes + (24 << 20)))


def _rms(x, g):
    return x * lax.rsqrt(jnp.mean(x * x, axis=-1, keepdims=True) + EPS) * g


def _gelu(x):
    c = math.sqrt(2.0 / math.pi)
    half = 0.5 * x
    return half + half * jnp.tanh(x * (c + (c * 0.044715) * (x * x)))


def _lane_lt64(shape):
    lane = lax.broadcasted_iota(jnp.int32, shape, len(shape) - 1)
    return (lane % LANES) < HEAD_DIM


def _in_proj_kernel(x_ref, ng_ref, w_ref, qg_ref, kg_ref, cos_ref, sa_ref, sb_ref,
                    bd_ref, sgg_ref, sgw_ref, sgb_ref, wb_ref,
                    q_ref, k_ref, vt_ref, sga_ref, ybg_ref, *, tk):
    rows = x_ref.shape[0]
    hb = _rms(x_ref[...], ng_ref[...]).astype(BF16)

    def proj(lo, hi):
        return jnp.dot(hb, w_ref[:, lo:hi], preferred_element_type=F32)

    cos = cos_ref[...]
    sa = sa_ref[...]
    sb = sb_ref[...]

    def head_norm_rope(z, gain, scale):
        ss = jnp.dot((z * z).astype(BF16), bd_ref[...], preferred_element_type=F32)
        zn = z * lax.rsqrt(ss * (1.0 / HEAD_DIM) + EPS) * gain
        rot = zn * cos + pltpu.roll(zn, LANES - 16, 1) * sa + pltpu.roll(zn, 16, 1) * sb
        return rot * scale if scale != 1.0 else rot

    zq = proj(0, Q_WIDTH)
    zkv = proj(_OFF_K, _OFF_U)

    for j in range(Q_WIDTH // LANES):
        q_ref[j] = head_norm_rope(
            zq[:, j * LANES:(j + 1) * LANES], qg_ref[...], Q_SCALE).T.astype(BF16)

    zu = proj(_OFF_U, _OFF_VS)

    kr = head_norm_rope(zkv[:, :KV_WIDTH], kg_ref[...], 1.0)
    first = _lane_lt64(kr.shape)
    k_ref[0] = jnp.where(first, kr, 0.0).astype(BF16)
    k_ref[1] = jnp.where(first, 0.0, kr).astype(BF16)

    vt = zkv[:, KV_WIDTH:].T
    ones = jnp.ones((VT_ROWS - HEAD_DIM, tk), BF16)
    for g in range(N_KV_HEADS):
        for r in range(rows // tk):
            vt_ref[g, r, :HEAD_DIM, :] = vt[g * HEAD_DIM:(g + 1) * HEAD_DIM,
                                            r * tk:(r + 1) * tk].astype(BF16)
            vt_ref[g, r, HEAD_DIM:, :] = ones

    zvs = proj(_OFF_VS, _OFF_GA)
    u = _gelu(zu)
    zga = proj(_OFF_GA, _OFF_GB)

    vs = _rms(_gelu(zvs), sgg_ref[...]).astype(BF16)
    zgb = proj(_OFF_GB, IN_WIDTH)
    sga_ref[...] = jax.nn.sigmoid(zga).astype(BF16)
    pair = 2 * CHUNK
    lt64 = _lane_lt64((CHUNK, 2 * LANES))
    sgu_rows = []
    for c in range(rows // pair):
        r0 = c * pair
        slabs0, slabs1 = [], []
        for s in range(SG_WIDTH // LANES):
            cols = slice(s * LANES, (s + 1) * LANES)
            rhs = jnp.concatenate([vs[r0:r0 + CHUNK, cols], vs[r0 + CHUNK:r0 + pair, cols]], axis=1)
            oa = jnp.dot(sgw_ref[2 * s], rhs, preferred_element_type=F32)
            ob = jnp.dot(sgw_ref[2 * s + 1], rhs, preferred_element_type=F32)
            bias = sgb_ref[:, cols]
            sp = jnp.where(lt64, oa, ob) + jnp.concatenate([bias, bias], axis=1)
            slabs0.append(u[r0:r0 + CHUNK, cols] * sp[:, :LANES])
            slabs1.append(u[r0 + CHUNK:r0 + pair, cols] * sp[:, LANES:])
        sgu_rows.append(jnp.concatenate(slabs0, axis=1))
        sgu_rows.append(jnp.concatenate(slabs1, axis=1))
    sgu = jnp.concatenate(sgu_rows, axis=0).astype(BF16)
    sgb_gate = jax.nn.sigmoid(zgb)
    yb = jnp.dot(sgu, wb_ref[...], preferred_element_type=F32)
    ybg_ref[...] = (sgb_gate * yb).astype(BF16)


def _in_proj(x, layer, p, seq_len):
    n = x.shape[0]
    tok, _, tk = _tiles(seq_len)
    n_pos = seq_len // tok
    const2 = lambda i: (0, 0)
    layer3 = lambda i: (layer, 0, 0)
    blocks = (tok * D_MODEL * 4 + D_MODEL * IN_WIDTH * 2 + SG_WIDTH * D_MODEL * 2
              + tok * (Q_WIDTH + 2 * LANES + 2 * VT_ROWS + 2 * D_MODEL) * 2)
    return pl.pallas_call(
        functools.partial(_in_proj_kernel, tk=tk),
        grid=(n // tok,),
        in_specs=[
            pl.BlockSpec((tok, D_MODEL), lambda i: (i, 0)),
            pl.BlockSpec((None, 1, D_MODEL), layer3),
            pl.BlockSpec((None, D_MODEL, IN_WIDTH), layer3),
            pl.BlockSpec((None, 1, LANES), layer3),
            pl.BlockSpec((None, 1, LANES), layer3),
            pl.BlockSpec((tok, LANES), lambda i: (i % n_pos, 0)),
            pl.BlockSpec((tok, LANES), lambda i: (i % n_pos, 0)),
            pl.BlockSpec((tok, LANES), lambda i: (i % n_pos, 0)),
            pl.BlockSpec((LANES, LANES), const2),
            pl.BlockSpec((None, 1, SG_WIDTH), layer3),
            pl.BlockSpec((None, SG_GROUPS, CHUNK, CHUNK), lambda i: (layer, 0, 0, 0)),
            pl.BlockSpec((None, CHUNK, SG_WIDTH), layer3),
            pl.BlockSpec((None, SG_WIDTH, D_MODEL), layer3),
        ],
        out_specs=[
            pl.BlockSpec((Q_WIDTH // LANES, LANES, tok), lambda i: (0, 0, i)),
            pl.BlockSpec((N_KV_HEADS, tok, LANES), lambda i: (0, i, 0)),
            pl.BlockSpec((N_KV_HEADS, tok // tk, VT_ROWS, tk), lambda i: (0, i, 0, 0)),
            pl.BlockSpec((tok, D_MODEL), lambda i: (i, 0)),
            pl.BlockSpec((tok, D_MODEL), lambda i: (i, 0)),
        ],
        out_shape=[
            jax.ShapeDtypeStruct((Q_WIDTH // LANES, LANES, n), BF16),
            jax.ShapeDtypeStruct((N_KV_HEADS, n, LANES), BF16),
            jax.ShapeDtypeStruct((N_KV_HEADS, n // tk, VT_ROWS, tk), BF16),
            jax.ShapeDtypeStruct((n, D_MODEL), BF16),
            jax.ShapeDtypeStruct((n, D_MODEL), BF16),
        ],
        compiler_params=pltpu.CompilerParams(
            dimension_semantics=("arbitrary",), vmem_limit_bytes=_vmem_limit(blocks)),
        name="in_proj",
    )(x, p["attn_norm_g"], p["w_in"], p["q_norm_g"], p["k_norm_g"],
      p["rope_cos"], p["rope_sa"], p["rope_sb"], p["head_ones"],
      p["sg_norm_g"], p["sg_w"], p["sg_bias"], p["w_branch_b"])


def _attn_mix_kernel(q_ref, k_ref, vt_ref, sga_ref, ybg_ref, x_ref, wa_ref, wm_ref,
                     o_ref, acc_ref, m_ref, s0_ref, s1_ref, bm0_ref, bm1_ref, *, tk):
    tq = q_ref.shape[2]
    n_kv = k_ref.shape[1] // tk
    acc_ref[...] = jnp.zeros(acc_ref.shape, F32)
    m_ref[...] = jnp.full(m_ref.shape, NEG_BIG, F32)

    def scores(blk, s_ref, bm_ref, h):
        g, j = divmod(h, Q_PER_KV)
        start = pl.multiple_of(blk * tk, tk)
        kb = k_ref[g, pl.ds(start, tk), :]
        s = jnp.dot(kb, q_ref[j], preferred_element_type=F32)
        s_ref[h] = s
        bm_ref[h:h + 1, :] = jnp.max(s, axis=0, keepdims=True)

    def accumulate(blk, s_ref, bm_ref, h):
        g = h // Q_PER_KV
        vb = vt_ref[g, blk]
        m_old = m_ref[h:h + 1, :]
        m_new = jnp.maximum(m_old, bm_ref[h:h + 1, :])
        alpha = jnp.exp2(m_old - m_new)
        pt = jnp.exp2(s_ref[h] - m_new).astype(BF16)
        pv = jnp.dot(vb, pt, preferred_element_type=F32)
        acc_ref[h] = alpha * acc_ref[h] + pv
        m_ref[h:h + 1, :] = m_new

    def step(score_blk, score_refs, acc_blk, acc_refs):
        for h in range(N_Q_HEADS):
            if score_blk is not None:
                scores(score_blk, *score_refs, h)
            if acc_blk is not None:
                accumulate(acc_blk, *acc_refs, h)

    even = (s0_ref, bm0_ref)
    odd = (s1_ref, bm1_ref)
    step(0, even, None, None)

    def kv_pair(t, carry):
        step(2 * t + 1, odd, 2 * t, even)
        step(2 * t + 2, even, 2 * t + 1, odd)
        return carry

    lax.fori_loop(0, n_kv // 2 - 1, kv_pair, 0)
    step(n_kv - 1, odd, n_kv - 2, even)
    step(None, None, n_kv - 1, odd)

    heads = []
    for h in range(N_Q_HEADS):
        a = acc_ref[h]
        heads.append(a[:HEAD_DIM] * (1.0 / a[HEAD_DIM:HEAD_DIM + 1]))
    ot = jnp.concatenate(heads, axis=0)
    ya = jnp.dot(ot.T.astype(BF16), wa_ref[...], preferred_element_type=F32)
    mix = sga_ref[...].astype(F32) * ya + ybg_ref[...].astype(F32)
    o_ref[...] = x_ref[...] + jnp.dot(mix.astype(BF16), wm_ref[...],
                                      preferred_element_type=F32)


def _attn_mix(x, q, kp, vt, sga, ybg, layer, p, batch, seq_len):
    _, tq, tk = _tiles(seq_len)
    n_kv = seq_len // tk
    n_q = seq_len // tq
    row = lambda b, i: (b * n_q + i, 0)
    blocks = (tq * (Q_WIDTH + 2 * D_MODEL) * 2 + 2 * tq * D_MODEL * 4
              + N_KV_HEADS * seq_len * (LANES + VT_ROWS) * 2
              + (Q_WIDTH + D_MODEL) * D_MODEL * 2)
    scratch = N_Q_HEADS * tq * 4 * (VT_ROWS + 2 * tk + 3)
    return pl.pallas_call(
        functools.partial(_attn_mix_kernel, tk=tk),
        grid=(batch, n_q),
        in_specs=[
            pl.BlockSpec((Q_WIDTH // LANES, LANES, tq), lambda b, i: (0, 0, b * n_q + i)),
            pl.BlockSpec((N_KV_HEADS, seq_len, LANES), lambda b, i: (0, b, 0)),
            pl.BlockSpec((N_KV_HEADS, n_kv, VT_ROWS, tk), lambda b, i: (0, b, 0, 0)),
            pl.BlockSpec((tq, D_MODEL), row),
            pl.BlockSpec((tq, D_MODEL), row),
            pl.BlockSpec((tq, D_MODEL), row),
            pl.BlockSpec((None, Q_WIDTH, D_MODEL), lambda b, i: (layer, 0, 0)),
            pl.BlockSpec((None, D_MODEL, D_MODEL), lambda b, i: (layer, 0, 0)),
        ],
        out_specs=pl.BlockSpec((tq, D_MODEL), row),
        out_shape=jax.ShapeDtypeStruct(x.shape, F32),
        scratch_shapes=[
            pltpu.VMEM((N_Q_HEADS, VT_ROWS, tq), F32),
            pltpu.VMEM((N_Q_HEADS, tq), F32),
            pltpu.VMEM((N_Q_HEADS, tk, tq), F32),
            pltpu.VMEM((N_Q_HEADS, tk, tq), F32),
            pltpu.VMEM((N_Q_HEADS, tq), F32),
            pltpu.VMEM((N_Q_HEADS, tq), F32),
        ],
        compiler_params=pltpu.CompilerParams(
            dimension_semantics=("arbitrary", "arbitrary"),
            vmem_limit_bytes=_vmem_limit(blocks, scratch)),
        name="attn_mix",
    )(q, kp, vt, sga, ybg, x, p["w_branch_a"], p["w_mix_out"])


FFN_COLS = 512


def _conv_ffn_kernel(xp_ref, x_ref, xn_ref, ng_ref, wu_ref, cw_ref, cb_ref, wd_ref, fg_ref,
                     o_ref, a_ref, *, final):
    i = pl.program_id(1)
    rows = x_ref.shape[0]
    x = x_ref[...]
    xp = jnp.where(i > 0, xp_ref[...], 0.0)
    xn = jnp.where(i < pl.num_programs(1) - 1, xn_ref[...], 0.0)
    hb = _rms(jnp.concatenate([xp, x, xn], axis=0), ng_ref[...]).astype(BF16)

    n_slab = FFN_COLS // LANES

    def up(lo):
        a = jnp.dot(hb, wu_ref[:, lo:lo + FFN_COLS], preferred_element_type=F32)
        for j in range(n_slab):
            a_ref[lo // LANES + j] = a[:, j * LANES:(j + 1) * LANES]

    def conv(lo):
        outs = []
        for j in range(n_slab):
            slab = lo // LANES + j
            cols = slice(lo + j * LANES, lo + (j + 1) * LANES)
            w = cw_ref[:, cols]
            outs.append(a_ref[slab, HALO - 1:HALO - 1 + rows, :] * w[0:1]
                        + a_ref[slab, HALO:HALO + rows, :] * w[1:2]
                        + a_ref[slab, HALO + 1:HALO + 1 + rows, :] * w[2:3] + cb_ref[:, cols])
        return jnp.concatenate(outs, axis=1)

    n_chunk = D_FF // FFN_COLS
    up(0)
    up(D_FF)
    y = x
    for c in range(n_chunk):
        if c + 1 < n_chunk:
            up((c + 1) * FFN_COLS)
            up(D_FF + (c + 1) * FFN_COLS)
        gate = _gelu(conv(c * FFN_COLS)) * conv(D_FF + c * FFN_COLS)
        y = y + jnp.dot(gate.astype(BF16), wd_ref[c * FFN_COLS:(c + 1) * FFN_COLS, :],
                        preferred_element_type=F32)
    o_ref[...] = _rms(y, fg_ref[...]) if final else y


def _conv_ffn(x, layer, p, batch, seq_len, final):
    tok, _, _ = _tiles(seq_len)
    n_t = seq_len // tok
    per = tok // HALO
    n_halo = seq_len // HALO
    row = lambda b, i: (b * n_t + i, 0)
    prev = lambda b, i: (b * n_halo + jnp.maximum(i * per - 1, 0), 0)
    nxt = lambda b, i: (b * n_halo + jnp.minimum((i + 1) * per, n_halo - 1), 0)
    layer3 = lambda b, i: (layer, 0, 0)
    blocks = 2 * tok * D_MODEL * 4 + 3 * D_MODEL * D_FF * 2
    return pl.pallas_call(
        functools.partial(_conv_ffn_kernel, final=final),
        grid=(batch, n_t),
        in_specs=[
            pl.BlockSpec((HALO, D_MODEL), prev),
            pl.BlockSpec((tok, D_MODEL), row),
            pl.BlockSpec((HALO, D_MODEL), nxt),
            pl.BlockSpec((None, 1, D_MODEL), layer3),
            pl.BlockSpec((None, D_MODEL, 2 * D_FF), layer3),
            pl.BlockSpec((None, CONV_W, 2 * D_FF), layer3),
            pl.BlockSpec((None, 1, 2 * D_FF), layer3),
            pl.BlockSpec((None, D_FF, D_MODEL), layer3),
            pl.BlockSpec((1, D_MODEL), lambda b, i: (0, 0)),
        ],
        out_specs=pl.BlockSpec((tok, D_MODEL), row),
        out_shape=jax.ShapeDtypeStruct(x.shape, F32),
        scratch_shapes=[pltpu.VMEM((2 * D_FF // LANES, tok + 2 * HALO, LANES), F32)],
        compiler_params=pltpu.CompilerParams(
            dimension_semantics=("arbitrary", "arbitrary"),
            vmem_limit_bytes=_vmem_limit(blocks, 2 * D_FF * (tok + 2 * HALO) * 4)),
        name="conv_ffn",
    )(x, x, x, p["ffn_norm_g"], p["w_up"], p["conv_w"], p["conv_b"], p["w_down"],
      p["final_norm_g"])


def _rope_tables(seq_len):
    t = jnp.arange(seq_len)
    half = HEAD_DIM // 2
    freq = ROPE_THETA ** (-jnp.arange(0, half, 2, dtype=F32) / half)
    ang_r = (t // GRID_W).astype(F32)[:, None] * freq[None, :]
    ang_c = (t % GRID_W).astype(F32)[:, None] * freq[None, :]
    cr, sr, cc, sc = jnp.cos(ang_r), jnp.sin(ang_r), jnp.cos(ang_c), jnp.sin(ang_c)
    zero = jnp.zeros_like(sr)
    per_head = lambda parts: jnp.tile(jnp.concatenate(parts, axis=1), (1, LANES // HEAD_DIM))
    cos = per_head([cr, cr, cc, cc])
    sa = per_head([-sr, zero, -sc, zero])
    sb = per_head([zero, sr, zero, sc])
    return cos, sa, sb


def _prepare_params(attn_norm_g, w_in, q_norm_g, k_norm_g, sg_norm_g, sg_w, sg_b,
                    w_branch_a, w_branch_b, w_mix_out, ffn_norm_g, w_up, conv_w, conv_b,
                    w_down, final_norm_g):
    depth = w_in.shape[0]
    order = np.concatenate([
        np.arange(h * HEAD_DIM, (h + 1) * HEAD_DIM)
        for j in range(Q_PER_KV) for h in (j, Q_PER_KV + j)])
    w_in_p = jnp.concatenate([w_in[:, :, order], w_in[:, :, Q_WIDTH:]], axis=2).astype(BF16)
    two_heads = lambda g: jnp.tile(g, (1, LANES // HEAD_DIM)).reshape(depth, 1, LANES)
    blk = np.arange(LANES) // HEAD_DIM
    return dict(
        attn_norm_g=attn_norm_g.reshape(depth, 1, D_MODEL),
        w_in=w_in_p,
        q_norm_g=two_heads(q_norm_g),
        k_norm_g=two_heads(k_norm_g),
        head_ones=jnp.asarray(blk[:, None] == blk[None, :], BF16),
        sg_norm_g=sg_norm_g.reshape(depth, 1, SG_WIDTH),
        sg_w=sg_w.astype(BF16),
        sg_bias=jnp.repeat(jnp.swapaxes(sg_b, 1, 2), SG_GROUP_DIM, axis=2),
        w_branch_a=w_branch_a.astype(BF16),
        w_branch_b=w_branch_b.astype(BF16),
        w_mix_out=w_mix_out.astype(BF16),
        ffn_norm_g=ffn_norm_g.reshape(depth, 1, D_MODEL),
        w_up=w_up.astype(BF16),
        conv_w=conv_w,
        conv_b=conv_b.reshape(depth, 1, 2 * D_FF),
        w_down=w_down.astype(BF16),
        final_norm_g=final_norm_g.reshape(1, D_MODEL),
    )


def _trunk(x3, p):
    batch, seq_len, _ = x3.shape
    depth = p["w_in"].shape[0]
    tok, tq, tk = _tiles(seq_len)
    assert seq_len % tok == 0 and seq_len % tq == 0 and seq_len % (2 * tk) == 0
    assert tok % (2 * CHUNK) == 0 and tok % tk == 0 and tok % HALO == 0
    p = dict(p)
    p["rope_cos"], p["rope_sa"], p["rope_sb"] = _rope_tables(seq_len)
    x = x3.reshape(batch * seq_len, D_MODEL)
    for layer in range(depth):
        q, kp, vt, sga, ybg = _in_proj(x, layer, p, seq_len)
        x = _attn_mix(x, q, kp, vt, sga, ybg, layer, p, batch, seq_len)
        x = _conv_ffn(x, layer, p, batch, seq_len, final=(layer == depth - 1))
    return x.reshape(batch, seq_len, D_MODEL)


def kernel(x_prompt, x_sample, attn_norm_g, w_in, q_norm_g, k_norm_g, sg_norm_g, sg_w, sg_b,
           w_branch_a, w_branch_b, w_mix_out, ffn_norm_g, w_up, conv_w, conv_b, w_down,
           final_norm_g):
    p = _prepare_params(attn_norm_g, w_in, q_norm_g, k_norm_g, sg_norm_g, sg_w, sg_b,
                        w_branch_a, w_branch_b, w_mix_out, ffn_norm_g, w_up, conv_w, conv_b,
                        w_down, final_norm_g)
    return (_trunk(x_prompt, p), _trunk(x_sample, p))
```

```python
import functools
import math

import jax
import jax.numpy as jnp
import numpy as np
from jax import lax
from jax.experimental import pallas as pl
from jax.experimental.pallas import tpu as pltpu

D_MODEL = 1024
GRID_W = 64
HEAD_DIM = 64
N_Q_HEADS = 8
N_KV_HEADS = 2
Q_PER_KV = N_Q_HEADS // N_KV_HEADS
Q_WIDTH = N_Q_HEADS * HEAD_DIM
KV_WIDTH = N_KV_HEADS * HEAD_DIM
SG_GROUPS = 8
SG_GROUP_DIM = 64
SG_WIDTH = SG_GROUPS * SG_GROUP_DIM
CHUNK = 128
D_FF = 2048
CONV_W = 3
ROPE_THETA = 10000.0
EPS = 1e-6
IN_WIDTH = Q_WIDTH + 2 * KV_WIDTH + 2 * SG_WIDTH + 2 * D_MODEL

LANES = 128
F32_SUBLANES = 8
BF16_SUBLANES = 16
V7X_VMEM_BYTES = 64 * 1024 * 1024

_OFF_K = Q_WIDTH
_OFF_V = _OFF_K + KV_WIDTH
_OFF_U = _OFF_V + KV_WIDTH
_OFF_VS = _OFF_U + SG_WIDTH
_OFF_GA = _OFF_VS + SG_WIDTH
_OFF_GB = _OFF_GA + D_MODEL

VT_ROWS = HEAD_DIM + BF16_SUBLANES
NEG_BIG = -1e30
Q_SCALE = math.log2(math.e) / math.sqrt(HEAD_DIM)
HALO = F32_SUBLANES

F32 = jnp.float32
BF16 = jnp.bfloat16


def _tiles(seq_len):
    tok = min(512, seq_len)
    tq = min(512, seq_len)
    tk = min(256, seq_len // 2)
    return tok, tq, tk


def _vmem_limit(block_bytes, scratch_bytes=0):
    return int(min(V7X_VMEM_BYTES - (4 << 20), 2 * block_bytes + scratch_bytes + (24 << 20)))


def _rms(x, g):
    return x * lax.rsqrt(jnp.mean(x * x, axis=-1, keepdims=True) + EPS) * g


def _gelu(x):
    c = math.sqrt(2.0 / math.pi)
    half = 0.5 * x
    return half + half * jnp.tanh(x * (c + (c * 0.044715) * (x * x)))


def _lane_lt64(shape):
    lane = lax.broadcasted_iota(jnp.int32, shape, len(shape) - 1)
    return (lane % LANES) < HEAD_DIM


def _in_proj_kernel(x_ref, ng_ref, w_ref, qg_ref, kg_ref, cos_ref, sa_ref, sb_ref,
                    bd_ref, sgg_ref, sgw_ref, sgb_ref, wb_ref,
                    q_ref, k_ref, vt_ref, sga_ref, ybg_ref, *, tk):
    rows = x_ref.shape[0]
    hb = _rms(x_ref[...], ng_ref[...]).astype(BF16)

    def proj(lo, hi):
        return jnp.dot(hb, w_ref[:, lo:hi], preferred_element_type=F32)

    cos = cos_ref[...]
    sa = sa_ref[...]
    sb = sb_ref[...]

    def head_norm_rope(z, gain, scale):
        ss = jnp.dot((z * z).astype(BF16), bd_ref[...], preferred_element_type=F32)
        zn = z * lax.rsqrt(ss * (1.0 / HEAD_DIM) + EPS) * gain
        rot = zn * cos + pltpu.roll(zn, LANES - 16, 1) * sa + pltpu.roll(zn, 16, 1) * sb
        return rot * scale if scale != 1.0 else rot

    zq = proj(0, Q_WIDTH)
    zkv = proj(_OFF_K, _OFF_U)

    for j in range(Q_WIDTH // LANES):
        q_ref[j] = head_norm_rope(
            zq[:, j * LANES:(j + 1) * LANES], qg_ref[...], Q_SCALE).T.astype(BF16)

    zu = proj(_OFF_U, _OFF_VS)

    kr = head_norm_rope(zkv[:, :KV_WIDTH], kg_ref[...], 1.0)
    first = _lane_lt64(kr.shape)
    k_ref[0] = jnp.where(first, kr, 0.0).astype(BF16)
    k_ref[1] = jnp.where(first, 0.0, kr).astype(BF16)

    vt = zkv[:, KV_WIDTH:].T
    ones = jnp.ones((VT_ROWS - HEAD_DIM, tk), BF16)
    for g in range(N_KV_HEADS):
        for r in range(rows // tk):
            vt_ref[g, r, :HEAD_DIM, :] = vt[g * HEAD_DIM:(g + 1) * HEAD_DIM,
                                            r * tk:(r + 1) * tk].astype(BF16)
            vt_ref[g, r, HEAD_DIM:, :] = ones

    zvs = proj(_OFF_VS, _OFF_GA)
    u = _gelu(zu)
    zga = proj(_OFF_GA, _OFF_GB)

    vs = _rms(_gelu(zvs), sgg_ref[...]).astype(BF16)
    zgb = proj(_OFF_GB, IN_WIDTH)
    sga_ref[...] = jax.nn.sigmoid(zga).astype(BF16)
    pair = 2 * CHUNK
    lt64 = _lane_lt64((CHUNK, 2 * LANES))
    sgu_rows = []
    for c in range(rows // pair):
        r0 = c * pair
        slabs0, slabs1 = [], []
        for s in range(SG_WIDTH // LANES):
            cols = slice(s * LANES, (s + 1) * LANES)
            rhs = jnp.concatenate([vs[r0:r0 + CHUNK, cols], vs[r0 + CHUNK:r0 + pair, cols]], axis=1)
            oa = jnp.dot(sgw_ref[2 * s], rhs, preferred_element_type=F32)
            ob = jnp.dot(sgw_ref[2 * s + 1], rhs, preferred_element_type=F32)
            bias = sgb_ref[:, cols]
            sp = jnp.where(lt64, oa, ob) + jnp.concatenate([bias, bias], axis=1)
            slabs0.append(u[r0:r0 + CHUNK, cols] * sp[:, :LANES])
            slabs1.append(u[r0 + CHUNK:r0 + pair, cols] * sp[:, LANES:])
        sgu_rows.append(jnp.concatenate(slabs0, axis=1))
        sgu_rows.append(jnp.concatenate(slabs1, axis=1))
    sgu = jnp.concatenate(sgu_rows, axis=0).astype(BF16)
    sgb_gate = jax.nn.sigmoid(zgb)
    yb = jnp.dot(sgu, wb_ref[...], preferred_element_type=F32)
    ybg_ref[...] = (sgb_gate * yb).astype(BF16)


def _in_proj(x, layer, p, seq_len):
    n = x.shape[0]
    tok, _, tk = _tiles(seq_len)
    n_pos = seq_len // tok
    const2 = lambda i: (0, 0)
    layer3 = lambda i: (layer, 0, 0)
    blocks = (tok * D_MODEL * 4 + D_MODEL * IN_WIDTH * 2 + SG_WIDTH * D_MODEL * 2
              + tok * (Q_WIDTH + 2 * LANES + 2 * VT_ROWS + 2 * D_MODEL) * 2)
    return pl.pallas_call(
        functools.partial(_in_proj_kernel, tk=tk),
        grid=(n // tok,),
        in_specs=[
            pl.BlockSpec((tok, D_MODEL), lambda i: (i, 0)),
            pl.BlockSpec((None, 1, D_MODEL), layer3),
            pl.BlockSpec((None, D_MODEL, IN_WIDTH), layer3),
            pl.BlockSpec((None, 1, LANES), layer3),
            pl.BlockSpec((None, 1, LANES), layer3),
            pl.BlockSpec((tok, LANES), lambda i: (i % n_pos, 0)),
            pl.BlockSpec((tok, LANES), lambda i: (i % n_pos, 0)),
            pl.BlockSpec((tok, LANES), lambda i: (i % n_pos, 0)),
            pl.BlockSpec((LANES, LANES), const2),
            pl.BlockSpec((None, 1, SG_WIDTH), layer3),
            pl.BlockSpec((None, SG_GROUPS, CHUNK, CHUNK), lambda i: (layer, 0, 0, 0)),
            pl.BlockSpec((None, CHUNK, SG_WIDTH), layer3),
            pl.BlockSpec((None, SG_WIDTH, D_MODEL), layer3),
        ],
        out_specs=[
            pl.BlockSpec((Q_WIDTH // LANES, LANES, tok), lambda i: (0, 0, i)),
            pl.BlockSpec((N_KV_HEADS, tok, LANES), lambda i: (0, i, 0)),
            pl.BlockSpec((N_KV_HEADS, tok // tk, VT_ROWS, tk), lambda i: (0, i, 0, 0)),
            pl.BlockSpec((tok, D_MODEL), lambda i: (i, 0)),
            pl.BlockSpec((tok, D_MODEL), lambda i: (i, 0)),
        ],
        out_shape=[
            jax.ShapeDtypeStruct((Q_WIDTH // LANES, LANES, n), BF16),
            jax.ShapeDtypeStruct((N_KV_HEADS, n, LANES), BF16),
            jax.ShapeDtypeStruct((N_KV_HEADS, n // tk, VT_ROWS, tk), BF16),
            jax.ShapeDtypeStruct((n, D_MODEL), BF16),
            jax.ShapeDtypeStruct((n, D_MODEL), BF16),
        ],
        compiler_params=pltpu.CompilerParams(
            dimension_semantics=("arbitrary",), vmem_limit_bytes=_vmem_limit(blocks)),
        name="in_proj",
    )(x, p["attn_norm_g"], p["w_in"], p["q_norm_g"], p["k_norm_g"],
      p["rope_cos"], p["rope_sa"], p["rope_sb"], p["head_ones"],
      p["sg_norm_g"], p["sg_w"], p["sg_bias"], p["w_branch_b"])


def _attn_mix_kernel(q_ref, qn_ref, k_ref, vt_ref, sga_ref, ybg_ref, x_ref, wa_ref, wm_ref,
                     o_ref, acc_ref, m_ref, s0_ref, s1_ref, bm0_ref, bm1_ref, *, tk):
    tq = q_ref.shape[2]
    n_kv = k_ref.shape[1] // tk
    acc_ref[...] = jnp.zeros(acc_ref.shape, F32)
    m_ref[...] = jnp.full(m_ref.shape, NEG_BIG, F32)

    def scores(blk, s_ref, bm_ref, h, qt_ref):
        g, j = divmod(h, Q_PER_KV)
        start = pl.multiple_of(blk * tk, tk)
        kb = k_ref[g, pl.ds(start, tk), :]
        s = jnp.dot(kb, qt_ref[j], preferred_element_type=F32)
        s_ref[h] = s
        bm_ref[h:h + 1, :] = jnp.max(s, axis=0, keepdims=True)

    def accumulate(blk, s_ref, bm_ref, h):
        g = h // Q_PER_KV
        vb = vt_ref[g, blk]
        m_old = m_ref[h:h + 1, :]
        m_new = jnp.maximum(m_old, bm_ref[h:h + 1, :])
        alpha = jnp.exp2(m_old - m_new)
        pt = jnp.exp2(s_ref[h] - m_new).astype(BF16)
        pv = jnp.dot(vb, pt, preferred_element_type=F32)
        acc_ref[h] = alpha * acc_ref[h] + pv
        m_ref[h:h + 1, :] = m_new

    def step(score_blk, score_refs, acc_blk, acc_refs, qt_ref=q_ref):
        for h in range(N_Q_HEADS):
            if score_blk is not None:
                scores(score_blk, *score_refs, h, qt_ref)
            if acc_blk is not None:
                accumulate(acc_blk, *acc_refs, h)

    even = (s0_ref, bm0_ref)
    odd = (s1_ref, bm1_ref)

    @pl.when(pl.program_id(1) == 0)
    def _():
        step(0, even, None, None)

    def kv_pair(t, carry):
        step(2 * t + 1, odd, 2 * t, even)
        step(2 * t + 2, even, 2 * t + 1, odd)
        return carry

    lax.fori_loop(0, n_kv // 2 - 1, kv_pair, 0)
    step(n_kv - 1, odd, n_kv - 2, even)
    step(0, even, n_kv - 1, odd, qn_ref)

    heads = []
    for h in range(N_Q_HEADS):
        a = acc_ref[h]
        heads.append(a[:HEAD_DIM] * (1.0 / a[HEAD_DIM:HEAD_DIM + 1]))
    ot = jnp.concatenate(heads, axis=0)
    ya = jnp.dot(ot.T.astype(BF16), wa_ref[...], preferred_element_type=F32)
    mix = sga_ref[...].astype(F32) * ya + ybg_ref[...].astype(F32)
    o_ref[...] = x_ref[...] + jnp.dot(mix.astype(BF16), wm_ref[...],
                                      preferred_element_type=F32)


def _attn_mix(x, q, kp, vt, sga, ybg, layer, p, batch, seq_len):
    _, tq, tk = _tiles(seq_len)
    n_kv = seq_len // tk
    n_q = seq_len // tq
    row = lambda b, i: (b * n_q + i, 0)
    blocks = (tq * (2 * Q_WIDTH + 2 * D_MODEL) * 2 + 2 * tq * D_MODEL * 4
              + N_KV_HEADS * seq_len * (LANES + VT_ROWS) * 2
              + (Q_WIDTH + D_MODEL) * D_MODEL * 2)
    scratch = N_Q_HEADS * tq * 4 * (VT_ROWS + 2 * tk + 3)
    return pl.pallas_call(
        functools.partial(_attn_mix_kernel, tk=tk),
        grid=(batch, n_q),
        in_specs=[
            pl.BlockSpec((Q_WIDTH // LANES, LANES, tq), lambda b, i: (0, 0, b * n_q + i)),
            pl.BlockSpec((Q_WIDTH // LANES, LANES, tq),
                         lambda b, i: (0, 0, b * n_q + jnp.minimum(i + 1, n_q - 1))),
            pl.BlockSpec((N_KV_HEADS, seq_len, LANES), lambda b, i: (0, b, 0)),
            pl.BlockSpec((N_KV_HEADS, n_kv, VT_ROWS, tk), lambda b, i: (0, b, 0, 0)),
            pl.BlockSpec((tq, D_MODEL), row),
            pl.BlockSpec((tq, D_MODEL), row),
            pl.BlockSpec((tq, D_MODEL), row),
            pl.BlockSpec((None, Q_WIDTH, D_MODEL), lambda b, i: (layer, 0, 0)),
            pl.BlockSpec((None, D_MODEL, D_MODEL), lambda b, i: (layer, 0, 0)),
        ],
        out_specs=pl.BlockSpec((tq, D_MODEL), row),
        out_shape=jax.ShapeDtypeStruct(x.shape, F32),
        scratch_shapes=[
            pltpu.VMEM((N_Q_HEADS, VT_ROWS, tq), F32),
            pltpu.VMEM((N_Q_HEADS, tq), F32),
            pltpu.VMEM((N_Q_HEADS, tk, tq), F32),
            pltpu.VMEM((N_Q_HEADS, tk, tq), F32),
            pltpu.VMEM((N_Q_HEADS, tq), F32),
            pltpu.VMEM((N_Q_HEADS, tq), F32),
        ],
        compiler_params=pltpu.CompilerParams(
            dimension_semantics=("arbitrary", "arbitrary"),
            vmem_limit_bytes=_vmem_limit(blocks, scratch)),
        name="attn_mix",
    )(q, q, kp, vt, sga, ybg, x, p["w_branch_a"], p["w_mix_out"])


FFN_COLS = 512


def _conv_ffn_kernel(xp_ref, x_ref, xn_ref, ng_ref, wu_ref, cw_ref, cb_ref, wd_ref, fg_ref,
                     o_ref, a_ref, *, final):
    i = pl.program_id(1)
    rows = x_ref.shape[0]
    x = x_ref[...]
    xp = jnp.where(i > 0, xp_ref[...], 0.0)
    xn = jnp.where(i < pl.num_programs(1) - 1, xn_ref[...], 0.0)
    hb = _rms(jnp.concatenate([xp, x, xn], axis=0), ng_ref[...]).astype(BF16)

    n_slab = FFN_COLS // LANES

    def up(lo):
        a = jnp.dot(hb, wu_ref[:, lo:lo + FFN_COLS], preferred_element_type=F32)
        for j in range(n_slab):
            a_ref[lo // LANES + j] = a[:, j * LANES:(j + 1) * LANES]

    def conv(lo):
        outs = []
        for j in range(n_slab):
            slab = lo // LANES + j
            cols = slice(lo + j * LANES, lo + (j + 1) * LANES)
            w = cw_ref[:, cols]
            outs.append(a_ref[slab, HALO - 1:HALO - 1 + rows, :] * w[0:1]
                        + a_ref[slab, HALO:HALO + rows, :] * w[1:2]
                        + a_ref[slab, HALO + 1:HALO + 1 + rows, :] * w[2:3] + cb_ref[:, cols])
        return jnp.concatenate(outs, axis=1)

    n_chunk = D_FF // FFN_COLS
    up(0)
    up(D_FF)
    y = x
    for c in range(n_chunk):
        if c + 1 < n_chunk:
            up((c + 1) * FFN_COLS)
            up(D_FF + (c + 1) * FFN_COLS)
        gate = _gelu(conv(c * FFN_COLS)) * conv(D_FF + c * FFN_COLS)
        y = y + jnp.dot(gate.astype(BF16), wd_ref[c * FFN_COLS:(c + 1) * FFN_COLS, :],
                        preferred_element_type=F32)
    o_ref[...] = _rms(y, fg_ref[...]) if final else y


def _conv_ffn(x, layer, p, batch, seq_len, final):
    tok, _, _ = _tiles(seq_len)
    n_t = seq_len // tok
    per = tok // HALO
    n_halo = seq_len // HALO
    row = lambda b, i: (b * n_t + i, 0)
    prev = lambda b, i: (b * n_halo + jnp.maximum(i * per - 1, 0), 0)
    nxt = lambda b, i: (b * n_halo + jnp.minimum((i + 1) * per, n_halo - 1), 0)
    layer3 = lambda b, i: (layer, 0, 0)
    blocks = 2 * tok * D_MODEL * 4 + 3 * D_MODEL * D_FF * 2
    return pl.pallas_call(
        functools.partial(_conv_ffn_kernel, final=final),
        grid=(batch, n_t),
        in_specs=[
            pl.BlockSpec((HALO, D_MODEL), prev),
            pl.BlockSpec((tok, D_MODEL), row),
            pl.BlockSpec((HALO, D_MODEL), nxt),
            pl.BlockSpec((None, 1, D_MODEL), layer3),
            pl.BlockSpec((None, D_MODEL, 2 * D_FF), layer3),
            pl.BlockSpec((None, CONV_W, 2 * D_FF), layer3),
            pl.BlockSpec((None, 1, 2 * D_FF), layer3),
            pl.BlockSpec((None, D_FF, D_MODEL), layer3),
            pl.BlockSpec((1, D_MODEL), lambda b, i: (0, 0)),
        ],
        out_specs=pl.BlockSpec((tok, D_MODEL), row),
        out_shape=jax.ShapeDtypeStruct(x.shape, F32),
        scratch_shapes=[pltpu.VMEM((2 * D_FF // LANES, tok + 2 * HALO, LANES), F32)],
        compiler_params=pltpu.CompilerParams(
            dimension_semantics=("arbitrary", "arbitrary"),
            vmem_limit_bytes=_vmem_limit(blocks, 2 * D_FF * (tok + 2 * HALO) * 4)),
        name="conv_ffn",
    )(x, x, x, p["ffn_norm_g"], p["w_up"], p["conv_w"], p["conv_b"], p["w_down"],
      p["final_norm_g"])


def _rope_tables(seq_len):
    t = jnp.arange(seq_len)
    half = HEAD_DIM // 2
    freq = ROPE_THETA ** (-jnp.arange(0, half, 2, dtype=F32) / half)
    ang_r = (t // GRID_W).astype(F32)[:, None] * freq[None, :]
    ang_c = (t % GRID_W).astype(F32)[:, None] * freq[None, :]
    cr, sr, cc, sc = jnp.cos(ang_r), jnp.sin(ang_r), jnp.cos(ang_c), jnp.sin(ang_c)
    zero = jnp.zeros_like(sr)
    per_head = lambda parts: jnp.tile(jnp.concatenate(parts, axis=1), (1, LANES // HEAD_DIM))
    cos = per_head([cr, cr, cc, cc])
    sa = per_head([-sr, zero, -sc, zero])
    sb = per_head([zero, sr, zero, sc])
    return cos, sa, sb


def _prepare_params(attn_norm_g, w_in, q_norm_g, k_norm_g, sg_norm_g, sg_w, sg_b,
                    w_branch_a, w_branch_b, w_mix_out, ffn_norm_g, w_up, conv_w, conv_b,
                    w_down, final_norm_g):
    depth = w_in.shape[0]
    order = np.concatenate([
        np.arange(h * HEAD_DIM, (h + 1) * HEAD_DIM)
        for j in range(Q_PER_KV) for h in (j, Q_PER_KV + j)])
    w_in_p = jnp.concatenate([w_in[:, :, order], w_in[:, :, Q_WIDTH:]], axis=2).astype(BF16)
    two_heads = lambda g: jnp.tile(g, (1, LANES // HEAD_DIM)).reshape(depth, 1, LANES)
    blk = np.arange(LANES) // HEAD_DIM
    return dict(
        attn_norm_g=attn_norm_g.reshape(depth, 1, D_MODEL),
        w_in=w_in_p,
        q_norm_g=two_heads(q_norm_g),
        k_norm_g=two_heads(k_norm_g),
        head_ones=jnp.asarray(blk[:, None] == blk[None, :], BF16),
        sg_norm_g=sg_norm_g.reshape(depth, 1, SG_WIDTH),
        sg_w=sg_w.astype(BF16),
        sg_bias=jnp.repeat(jnp.swapaxes(sg_b, 1, 2), SG_GROUP_DIM, axis=2),
        w_branch_a=w_branch_a.astype(BF16),
        w_branch_b=w_branch_b.astype(BF16),
        w_mix_out=w_mix_out.astype(BF16),
        ffn_norm_g=ffn_norm_g.reshape(depth, 1, D_MODEL),
        w_up=w_up.astype(BF16),
        conv_w=conv_w,
        conv_b=conv_b.reshape(depth, 1, 2 * D_FF),
        w_down=w_down.astype(BF16),
        final_norm_g=final_norm_g.reshape(1, D_MODEL),
    )


def _trunk(x3, p):
    batch, seq_len, _ = x3.shape
    depth = p["w_in"].shape[0]
    tok, tq, tk = _tiles(seq_len)
    assert seq_len % tok == 0 and seq_len % tq == 0 and seq_len % (2 * tk) == 0
    assert tok % (2 * CHUNK) == 0 and tok % tk == 0 and tok % HALO == 0
    p = dict(p)
    p["rope_cos"], p["rope_sa"], p["rope_sb"] = _rope_tables(seq_len)
    x = x3.reshape(batch * seq_len, D_MODEL)
    for layer in range(depth):
        q, kp, vt, sga, ybg = _in_proj(x, layer, p, seq_len)
        x = _attn_mix(x, q, kp, vt, sga, ybg, layer, p, batch, seq_len)
        x = _conv_ffn(x, layer, p, batch, seq_len, final=(layer == depth - 1))
    return x.reshape(batch, seq_len, D_MODEL)


def kernel(x_prompt, x_sample, attn_norm_g, w_in, q_norm_g, k_norm_g, sg_norm_g, sg_w, sg_b,
           w_branch_a, w_branch_b, w_mix_out, ffn_norm_g, w_up, conv_w, conv_b, w_down,
           final_norm_g):
    p = _prepare_params(attn_norm_g, w_in, q_norm_g, k_norm_g, sg_norm_g, sg_w, sg_b,
                        w_branch_a, w_branch_b, w_mix_out, ffn_norm_g, w_up, conv_w, conv_b,
                        w_down, final_norm_g)
    return (_trunk(x_prompt, p), _trunk(x_sample, p))
```

```python
import functools
import math

import jax
import jax.numpy as jnp
import numpy as np
from jax import lax
from jax.experimental import pallas as pl
from jax.experimental.pallas import tpu as pltpu

D_MODEL = 1024
GRID_W = 64
HEAD_DIM = 64
N_Q_HEADS = 8
N_KV_HEADS = 2
Q_PER_KV = N_Q_HEADS // N_KV_HEADS
Q_WIDTH = N_Q_HEADS * HEAD_DIM
KV_WIDTH = N_KV_HEADS * HEAD_DIM
SG_GROUPS = 8
SG_GROUP_DIM = 64
SG_WIDTH = SG_GROUPS * SG_GROUP_DIM
CHUNK = 128
D_FF = 2048
CONV_W = 3
ROPE_THETA = 10000.0
EPS = 1e-6
IN_WIDTH = Q_WIDTH + 2 * KV_WIDTH + 2 * SG_WIDTH + 2 * D_MODEL

LANES = 128
F32_SUBLANES = 8
BF16_SUBLANES = 16
V7X_VMEM_BYTES = 64 * 1024 * 1024

_OFF_K = Q_WIDTH
_OFF_V = _OFF_K + KV_WIDTH
_OFF_U = _OFF_V + KV_WIDTH
_OFF_VS = _OFF_U + SG_WIDTH
_OFF_GA = _OFF_VS + SG_WIDTH
_OFF_GB = _OFF_GA + D_MODEL

VT_ROWS = HEAD_DIM + BF16_SUBLANES
NEG_BIG = -1e30
Q_SCALE = math.log2(math.e) / math.sqrt(HEAD_DIM)
HALO = F32_SUBLANES

F32 = jnp.float32
BF16 = jnp.bfloat16


def _tiles(seq_len):
    tok = min(512, seq_len)
    tq = min(512, seq_len)
    tk = 512 if seq_len >= 8 * 1024 else min(256, seq_len // 2)
    return tok, tq, tk


def _vmem_limit(block_bytes, scratch_bytes=0):
    return int(min(V7X_VMEM_BYTES - (4 << 20), 2 * block_bytes + scratch_bytes + (24 << 20)))


def _rms(x, g):
    return x * lax.rsqrt(jnp.mean(x * x, axis=-1, keepdims=True) + EPS) * g


def _gelu(x):
    c = math.sqrt(2.0 / math.pi)
    half = 0.5 * x
    return half + half * jnp.tanh(x * (c + (c * 0.044715) * (x * x)))


def _lane_lt64(shape):
    lane = lax.broadcasted_iota(jnp.int32, shape, len(shape) - 1)
    return (lane % LANES) < HEAD_DIM


def _in_proj_kernel(x_ref, ng_ref, w_ref, qg_ref, kg_ref, cos_ref, sa_ref, sb_ref,
                    bd_ref, sgg_ref, sgw_ref, sgb_ref, wb_ref,
                    q_ref, k_ref, vt_ref, sga_ref, ybg_ref, *, tk):
    rows = x_ref.shape[0]
    hb = _rms(x_ref[...], ng_ref[...]).astype(BF16)

    def proj(lo, hi):
        return jnp.dot(hb, w_ref[:, lo:hi], preferred_element_type=F32)

    cos = cos_ref[...]
    sa = sa_ref[...]
    sb = sb_ref[...]

    def head_norm_rope(z, gain, scale):
        ss = jnp.dot((z * z).astype(BF16), bd_ref[...], preferred_element_type=F32)
        zn = z * lax.rsqrt(ss * (1.0 / HEAD_DIM) + EPS) * gain
        rot = zn * cos + pltpu.roll(zn, LANES - 16, 1) * sa + pltpu.roll(zn, 16, 1) * sb
        return rot * scale if scale != 1.0 else rot

    zq = proj(0, Q_WIDTH)
    zkv = proj(_OFF_K, _OFF_U)

    for j in range(Q_WIDTH // LANES):
        q_ref[j] = head_norm_rope(
            zq[:, j * LANES:(j + 1) * LANES], qg_ref[...], Q_SCALE).T.astype(BF16)

    zu = proj(_OFF_U, _OFF_VS)

    kr = head_norm_rope(zkv[:, :KV_WIDTH], kg_ref[...], 1.0)
    first = _lane_lt64(kr.shape)
    k_ref[0] = jnp.where(first, kr, 0.0).astype(BF16)
    k_ref[1] = jnp.where(first, 0.0, kr).astype(BF16)

    vt = zkv[:, KV_WIDTH:].T
    ones = jnp.ones((VT_ROWS - HEAD_DIM, tk), BF16)
    for g in range(N_KV_HEADS):
        for r in range(rows // tk):
            vt_ref[g, r, :HEAD_DIM, :] = vt[g * HEAD_DIM:(g + 1) * HEAD_DIM,
                                            r * tk:(r + 1) * tk].astype(BF16)
            vt_ref[g, r, HEAD_DIM:, :] = ones

    zvs = proj(_OFF_VS, _OFF_GA)
    u = _gelu(zu)
    zga = proj(_OFF_GA, _OFF_GB)

    vs = _rms(_gelu(zvs), sgg_ref[...]).astype(BF16)
    zgb = proj(_OFF_GB, IN_WIDTH)
    sga_ref[...] = jax.nn.sigmoid(zga).astype(BF16)
    pair = 2 * CHUNK
    lt64 = _lane_lt64((CHUNK, 2 * LANES))
    sgu_rows = []
    for c in range(rows // pair):
        r0 = c * pair
        slabs0, slabs1 = [], []
        for s in range(SG_WIDTH // LANES):
            cols = slice(s * LANES, (s + 1) * LANES)
            rhs = jnp.concatenate([vs[r0:r0 + CHUNK, cols], vs[r0 + CHUNK:r0 + pair, cols]], axis=1)
            oa = jnp.dot(sgw_ref[2 * s], rhs, preferred_element_type=F32)
            ob = jnp.dot(sgw_ref[2 * s + 1], rhs, preferred_element_type=F32)
            bias = sgb_ref[:, cols]
            sp = jnp.where(lt64, oa, ob) + jnp.concatenate([bias, bias], axis=1)
            slabs0.append(u[r0:r0 + CHUNK, cols] * sp[:, :LANES])
            slabs1.append(u[r0 + CHUNK:r0 + pair, cols] * sp[:, LANES:])
        sgu_rows.append(jnp.concatenate(slabs0, axis=1))
        sgu_rows.append(jnp.concatenate(slabs1, axis=1))
    sgu = jnp.concatenate(sgu_rows, axis=0).astype(BF16)
    sgb_gate = jax.nn.sigmoid(zgb)
    yb = jnp.dot(sgu, wb_ref[...], preferred_element_type=F32)
    ybg_ref[...] = (sgb_gate * yb).astype(BF16)


def _in_proj(x, layer, p, seq_len):
    n = x.shape[0]
    tok, _, tk = _tiles(seq_len)
    n_pos = seq_len // tok
    const2 = lambda i: (0, 0)
    layer3 = lambda i: (layer, 0, 0)
    blocks = (tok * D_MODEL * 4 + D_MODEL * IN_WIDTH * 2 + SG_WIDTH * D_MODEL * 2
              + tok * (Q_WIDTH + 2 * LANES + 2 * VT_ROWS + 2 * D_MODEL) * 2)
    return pl.pallas_call(
        functools.partial(_in_proj_kernel, tk=tk),
        grid=(n // tok,),
        in_specs=[
            pl.BlockSpec((tok, D_MODEL), lambda i: (i, 0)),
            pl.BlockSpec((None, 1, D_MODEL), layer3),
            pl.BlockSpec((None, D_MODEL, IN_WIDTH), layer3),
            pl.BlockSpec((None, 1, LANES), layer3),
            pl.BlockSpec((None, 1, LANES), layer3),
            pl.BlockSpec((tok, LANES), lambda i: (i % n_pos, 0)),
            pl.BlockSpec((tok, LANES), lambda i: (i % n_pos, 0)),
            pl.BlockSpec((tok, LANES), lambda i: (i % n_pos, 0)),
            pl.BlockSpec((LANES, LANES), const2),
            pl.BlockSpec((None, 1, SG_WIDTH), layer3),
            pl.BlockSpec((None, SG_GROUPS, CHUNK, CHUNK), lambda i: (layer, 0, 0, 0)),
            pl.BlockSpec((None, CHUNK, SG_WIDTH), layer3),
            pl.BlockSpec((None, SG_WIDTH, D_MODEL), layer3),
        ],
        out_specs=[
            pl.BlockSpec((Q_WIDTH // LANES, LANES, tok), lambda i: (0, 0, i)),
            pl.BlockSpec((N_KV_HEADS, tok, LANES), lambda i: (0, i, 0)),
            pl.BlockSpec((N_KV_HEADS, tok // tk, VT_ROWS, tk), lambda i: (0, i, 0, 0)),
            pl.BlockSpec((tok, D_MODEL), lambda i: (i, 0)),
            pl.BlockSpec((tok, D_MODEL), lambda i: (i, 0)),
        ],
        out_shape=[
            jax.ShapeDtypeStruct((Q_WIDTH // LANES, LANES, n), BF16),
            jax.ShapeDtypeStruct((N_KV_HEADS, n, LANES), BF16),
            jax.ShapeDtypeStruct((N_KV_HEADS, n // tk, VT_ROWS, tk), BF16),
            jax.ShapeDtypeStruct((n, D_MODEL), BF16),
            jax.ShapeDtypeStruct((n, D_MODEL), BF16),
        ],
        compiler_params=pltpu.CompilerParams(
            dimension_semantics=("arbitrary",), vmem_limit_bytes=_vmem_limit(blocks)),
        name="in_proj",
    )(x, p["attn_norm_g"], p["w_in"], p["q_norm_g"], p["k_norm_g"],
      p["rope_cos"], p["rope_sa"], p["rope_sb"], p["head_ones"],
      p["sg_norm_g"], p["sg_w"], p["sg_bias"], p["w_branch_b"])


def _attn_mix_kernel(q_ref, qn_ref, k_ref, vt_ref, sga_ref, ybg_ref, x_ref, wa_ref, wm_ref,
                     o_ref, acc_ref, m_ref, s0_ref, s1_ref, bm0_ref, bm1_ref, *, tk):
    tq = q_ref.shape[2]
    n_kv = k_ref.shape[1] // tk
    acc_ref[...] = jnp.zeros(acc_ref.shape, F32)
    m_ref[...] = jnp.full(m_ref.shape, NEG_BIG, F32)

    def scores(blk, s_ref, bm_ref, h, qt_ref):
        g, j = divmod(h, Q_PER_KV)
        start = pl.multiple_of(blk * tk, tk)
        kb = k_ref[g, pl.ds(start, tk), :]
        s = jnp.dot(kb, qt_ref[j], preferred_element_type=F32)
        s_ref[h] = s
        bm_ref[h:h + 1, :] = jnp.max(s, axis=0, keepdims=True)

    def accumulate(blk, s_ref, bm_ref, h):
        g = h // Q_PER_KV
        vb = vt_ref[g, blk]
        m_old = m_ref[h:h + 1, :]
        m_new = jnp.maximum(m_old, bm_ref[h:h + 1, :])
        alpha = jnp.exp2(m_old - m_new)
        pt = jnp.exp2(s_ref[h] - m_new).astype(BF16)
        pv = jnp.dot(vb, pt, preferred_element_type=F32)
        acc_ref[h] = alpha * acc_ref[h] + pv
        m_ref[h:h + 1, :] = m_new

    def step(score_blk, score_refs, acc_blk, acc_refs, qt_ref=q_ref):
        for h in range(N_Q_HEADS):
            if score_blk is not None:
                scores(score_blk, *score_refs, h, qt_ref)
            if acc_blk is not None:
                accumulate(acc_blk, *acc_refs, h)

    even = (s0_ref, bm0_ref)
    odd = (s1_ref, bm1_ref)

    @pl.when(pl.program_id(1) == 0)
    def _():
        step(0, even, None, None)

    def kv_pair(t, carry):
        step(2 * t + 1, odd, 2 * t, even)
        step(2 * t + 2, even, 2 * t + 1, odd)
        return carry

    lax.fori_loop(0, n_kv // 2 - 1, kv_pair, 0)
    step(n_kv - 1, odd, n_kv - 2, even)
    step(0, even, n_kv - 1, odd, qn_ref)

    heads = []
    for h in range(N_Q_HEADS):
        a = acc_ref[h]
        heads.append(a[:HEAD_DIM] * (1.0 / a[HEAD_DIM:HEAD_DIM + 1]))
    ot = jnp.concatenate(heads, axis=0)
    ya = jnp.dot(ot.T.astype(BF16), wa_ref[...], preferred_element_type=F32)
    mix = sga_ref[...].astype(F32) * ya + ybg_ref[...].astype(F32)
    o_ref[...] = x_ref[...] + jnp.dot(mix.astype(BF16), wm_ref[...],
                                      preferred_element_type=F32)


def _attn_mix(x, q, kp, vt, sga, ybg, layer, p, batch, seq_len):
    _, tq, tk = _tiles(seq_len)
    n_kv = seq_len // tk
    n_q = seq_len // tq
    row = lambda b, i: (b * n_q + i, 0)
    blocks = (tq * (2 * Q_WIDTH + 2 * D_MODEL) * 2 + 2 * tq * D_MODEL * 4
              + N_KV_HEADS * seq_len * (LANES + VT_ROWS) * 2
              + (Q_WIDTH + D_MODEL) * D_MODEL * 2)
    scratch = N_Q_HEADS * tq * 4 * (VT_ROWS + 2 * tk + 3)
    return pl.pallas_call(
        functools.partial(_attn_mix_kernel, tk=tk),
        grid=(batch, n_q),
        in_specs=[
            pl.BlockSpec((Q_WIDTH // LANES, LANES, tq), lambda b, i: (0, 0, b * n_q + i)),
            pl.BlockSpec((Q_WIDTH // LANES, LANES, tq),
                         lambda b, i: (0, 0, b * n_q + jnp.minimum(i + 1, n_q - 1))),
            pl.BlockSpec((N_KV_HEADS, seq_len, LANES), lambda b, i: (0, b, 0)),
            pl.BlockSpec((N_KV_HEADS, n_kv, VT_ROWS, tk), lambda b, i: (0, b, 0, 0)),
            pl.BlockSpec((tq, D_MODEL), row),
            pl.BlockSpec((tq, D_MODEL), row),
            pl.BlockSpec((tq, D_MODEL), row),
            pl.BlockSpec((None, Q_WIDTH, D_MODEL), lambda b, i: (layer, 0, 0)),
            pl.BlockSpec((None, D_MODEL, D_MODEL), lambda b, i: (layer, 0, 0)),
        ],
        out_specs=pl.BlockSpec((tq, D_MODEL), row),
        out_shape=jax.ShapeDtypeStruct(x.shape, F32),
        scratch_shapes=[
            pltpu.VMEM((N_Q_HEADS, VT_ROWS, tq), F32),
            pltpu.VMEM((N_Q_HEADS, tq), F32),
            pltpu.VMEM((N_Q_HEADS, tk, tq), F32),
            pltpu.VMEM((N_Q_HEADS, tk, tq), F32),
            pltpu.VMEM((N_Q_HEADS, tq), F32),
            pltpu.VMEM((N_Q_HEADS, tq), F32),
        ],
        compiler_params=pltpu.CompilerParams(
            dimension_semantics=("arbitrary", "arbitrary"),
            vmem_limit_bytes=_vmem_limit(blocks, scratch)),
        name="attn_mix",
    )(q, q, kp, vt, sga, ybg, x, p["w_branch_a"], p["w_mix_out"])


FFN_COLS = 512


def _conv_ffn_kernel(xp_ref, x_ref, xn_ref, ng_ref, wu_ref, cw_ref, cb_ref, wd_ref, fg_ref,
                     o_ref, a_ref, *, final):
    i = pl.program_id(1)
    rows = x_ref.shape[0]
    x = x_ref[...]
    xp = jnp.where(i > 0, xp_ref[...], 0.0)
    xn = jnp.where(i < pl.num_programs(1) - 1, xn_ref[...], 0.0)
    hb = _rms(jnp.concatenate([xp, x, xn], axis=0), ng_ref[...]).astype(BF16)

    n_slab = FFN_COLS // LANES

    def up(lo):
        a = jnp.dot(hb, wu_ref[:, lo:lo + FFN_COLS], preferred_element_type=F32)
        for j in range(n_slab):
            a_ref[lo // LANES + j] = a[:, j * LANES:(j + 1) * LANES]

    def conv(lo):
        outs = []
        for j in range(n_slab):
            slab = lo // LANES + j
            cols = slice(lo + j * LANES, lo + (j + 1) * LANES)
            w = cw_ref[:, cols]
            outs.append(a_ref[slab, HALO - 1:HALO - 1 + rows, :] * w[0:1]
                        + a_ref[slab, HALO:HALO + rows, :] * w[1:2]
                        + a_ref[slab, HALO + 1:HALO + 1 + rows, :] * w[2:3] + cb_ref[:, cols])
        return jnp.concatenate(outs, axis=1)

    n_chunk = D_FF // FFN_COLS
    up(0)
    up(D_FF)
    y = x
    for c in range(n_chunk):
        if c + 1 < n_chunk:
            up((c + 1) * FFN_COLS)
            up(D_FF + (c + 1) * FFN_COLS)
        gate = _gelu(conv(c * FFN_COLS)) * conv(D_FF + c * FFN_COLS)
        y = y + jnp.dot(gate.astype(BF16), wd_ref[c * FFN_COLS:(c + 1) * FFN_COLS, :],
                        preferred_element_type=F32)
    o_ref[...] = _rms(y, fg_ref[...]) if final else y


def _conv_ffn(x, layer, p, batch, seq_len, final):
    tok, _, _ = _tiles(seq_len)
    n_t = seq_len // tok
    per = tok // HALO
    n_halo = seq_len // HALO
    row = lambda b, i: (b * n_t + i, 0)
    prev = lambda b, i: (b * n_halo + jnp.maximum(i * per - 1, 0), 0)
    nxt = lambda b, i: (b * n_halo + jnp.minimum((i + 1) * per, n_halo - 1), 0)
    layer3 = lambda b, i: (layer, 0, 0)
    blocks = 2 * tok * D_MODEL * 4 + 3 * D_MODEL * D_FF * 2
    return pl.pallas_call(
        functools.partial(_conv_ffn_kernel, final=final),
        grid=(batch, n_t),
        in_specs=[
            pl.BlockSpec((HALO, D_MODEL), prev),
            pl.BlockSpec((tok, D_MODEL), row),
            pl.BlockSpec((HALO, D_MODEL), nxt),
            pl.BlockSpec((None, 1, D_MODEL), layer3),
            pl.BlockSpec((None, D_MODEL, 2 * D_FF), layer3),
            pl.BlockSpec((None, CONV_W, 2 * D_FF), layer3),
            pl.BlockSpec((None, 1, 2 * D_FF), layer3),
            pl.BlockSpec((None, D_FF, D_MODEL), layer3),
            pl.BlockSpec((1, D_MODEL), lambda b, i: (0, 0)),
        ],
        out_specs=pl.BlockSpec((tok, D_MODEL), row),
        out_shape=jax.ShapeDtypeStruct(x.shape, F32),
        scratch_shapes=[pltpu.VMEM((2 * D_FF // LANES, tok + 2 * HALO, LANES), F32)],
        compiler_params=pltpu.CompilerParams(
            dimension_semantics=("arbitrary", "arbitrary"),
            vmem_limit_bytes=_vmem_limit(blocks, 2 * D_FF * (tok + 2 * HALO) * 4)),
        name="conv_ffn",
    )(x, x, x, p["ffn_norm_g"], p["w_up"], p["conv_w"], p["conv_b"], p["w_down"],
      p["final_norm_g"])


def _rope_tables(seq_len):
    t = jnp.arange(seq_len)
    half = HEAD_DIM // 2
    freq = ROPE_THETA ** (-jnp.arange(0, half, 2, dtype=F32) / half)
    ang_r = (t // GRID_W).astype(F32)[:, None] * freq[None, :]
    ang_c = (t % GRID_W).astype(F32)[:, None] * freq[None, :]
    cr, sr, cc, sc = jnp.cos(ang_r), jnp.sin(ang_r), jnp.cos(ang_c), jnp.sin(ang_c)
    zero = jnp.zeros_like(sr)
    per_head = lambda parts: jnp.tile(jnp.concatenate(parts, axis=1), (1, LANES // HEAD_DIM))
    cos = per_head([cr, cr, cc, cc])
    sa = per_head([-sr, zero, -sc, zero])
    sb = per_head([zero, sr, zero, sc])
    return cos, sa, sb


def _prepare_params(attn_norm_g, w_in, q_norm_g, k_norm_g, sg_norm_g, sg_w, sg_b,
                    w_branch_a, w_branch_b, w_mix_out, ffn_norm_g, w_up, conv_w, conv_b,
                    w_down, final_norm_g):
    depth = w_in.shape[0]
    order = np.concatenate([
        np.arange(h * HEAD_DIM, (h + 1) * HEAD_DIM)
        for j in range(Q_PER_KV) for h in (j, Q_PER_KV + j)])
    w_in_p = jnp.concatenate([w_in[:, :, order], w_in[:, :, Q_WIDTH:]], axis=2).astype(BF16)
    two_heads = lambda g: jnp.tile(g, (1, LANES // HEAD_DIM)).reshape(depth, 1, LANES)
    blk = np.arange(LANES) // HEAD_DIM
    return dict(
        attn_norm_g=attn_norm_g.reshape(depth, 1, D_MODEL),
        w_in=w_in_p,
        q_norm_g=two_heads(q_norm_g),
        k_norm_g=two_heads(k_norm_g),
        head_ones=jnp.asarray(blk[:, None] == blk[None, :], BF16),
        sg_norm_g=sg_norm_g.reshape(depth, 1, SG_WIDTH),
        sg_w=sg_w.astype(BF16),
        sg_bias=jnp.repeat(jnp.swapaxes(sg_b, 1, 2), SG_GROUP_DIM, axis=2),
        w_branch_a=w_branch_a.astype(BF16),
        w_branch_b=w_branch_b.astype(BF16),
        w_mix_out=w_mix_out.astype(BF16),
        ffn_norm_g=ffn_norm_g.reshape(depth, 1, D_MODEL),
        w_up=w_up.astype(BF16),
        conv_w=conv_w,
        conv_b=conv_b.reshape(depth, 1, 2 * D_FF),
        w_down=w_down.astype(BF16),
        final_norm_g=final_norm_g.reshape(1, D_MODEL),
    )


def _trunk(x3, p):
    batch, seq_len, _ = x3.shape
    depth = p["w_in"].shape[0]
    tok, tq, tk = _tiles(seq_len)
    assert seq_len % tok == 0 and seq_len % tq == 0 and seq_len % (2 * tk) == 0
    assert tok % (2 * CHUNK) == 0 and tok % tk == 0 and tok % HALO == 0
    p = dict(p)
    p["rope_cos"], p["rope_sa"], p["rope_sb"] = _rope_tables(seq_len)
    x = x3.reshape(batch * seq_len, D_MODEL)
    for layer in range(depth):
        q, kp, vt, sga, ybg = _in_proj(x, layer, p, seq_len)
        x = _attn_mix(x, q, kp, vt, sga, ybg, layer, p, batch, seq_len)
        x = _conv_ffn(x, layer, p, batch, seq_len, final=(layer == depth - 1))
    return x.reshape(batch, seq_len, D_MODEL)


def kernel(x_prompt, x_sample, attn_norm_g, w_in, q_norm_g, k_norm_g, sg_norm_g, sg_w, sg_b,
           w_branch_a, w_branch_b, w_mix_out, ffn_norm_g, w_up, conv_w, conv_b, w_down,
           final_norm_g):
    p = _prepare_params(attn_norm_g, w_in, q_norm_g, k_norm_g, sg_norm_g, sg_w, sg_b,
                        w_branch_a, w_branch_b, w_mix_out, ffn_norm_g, w_up, conv_w, conv_b,
                        w_down, final_norm_g)
    return (_trunk(x_prompt, p), _trunk(x_sample, p))
```

```python
import functools
import math

import jax
import jax.numpy as jnp
import numpy as np
from jax import lax
from jax.experimental import pallas as pl
from jax.experimental.pallas import tpu as pltpu

D_MODEL = 1024
GRID_W = 64
HEAD_DIM = 64
N_Q_HEADS = 8
N_KV_HEADS = 2
Q_PER_KV = N_Q_HEADS // N_KV_HEADS
Q_WIDTH = N_Q_HEADS * HEAD_DIM
KV_WIDTH = N_KV_HEADS * HEAD_DIM
SG_GROUPS = 8
SG_GROUP_DIM = 64
SG_WIDTH = SG_GROUPS * SG_GROUP_DIM
CHUNK = 128
D_FF = 2048
CONV_W = 3
ROPE_THETA = 10000.0
EPS = 1e-6
IN_WIDTH = Q_WIDTH + 2 * KV_WIDTH + 2 * SG_WIDTH + 2 * D_MODEL

LANES = 128
F32_SUBLANES = 8
BF16_SUBLANES = 16
V7X_VMEM_BYTES = 64 * 1024 * 1024

_OFF_K = Q_WIDTH
_OFF_V = _OFF_K + KV_WIDTH
_OFF_U = _OFF_V + KV_WIDTH
_OFF_VS = _OFF_U + SG_WIDTH
_OFF_GA = _OFF_VS + SG_WIDTH
_OFF_GB = _OFF_GA + D_MODEL

VT_ROWS = HEAD_DIM + BF16_SUBLANES
ROPE_QUARTER = HEAD_DIM // 4
NEG_BIG = -1e30
Q_SCALE = math.log2(math.e) / math.sqrt(HEAD_DIM)
HALO = F32_SUBLANES

F32 = jnp.float32
BF16 = jnp.bfloat16


def _tiles(seq_len):
    tok = min(512, seq_len)
    tq = min(512, seq_len)
    tk = min(256, seq_len // 2)
    return tok, tq, tk


def _vmem_limit(block_bytes, scratch_bytes=0):
    return int(min(V7X_VMEM_BYTES - (4 << 20), 2 * block_bytes + scratch_bytes + (24 << 20)))


def _rms(x, g):
    return x * lax.rsqrt(jnp.mean(x * x, axis=-1, keepdims=True) + EPS) * g


def _gelu(x):
    c = math.sqrt(2.0 / math.pi)
    half = 0.5 * x
    return half + half * jnp.tanh(x * (c + (c * 0.044715) * (x * x)))


def _lane_lt64(shape):
    lane = lax.broadcasted_iota(jnp.int32, shape, len(shape) - 1)
    return (lane % LANES) < HEAD_DIM


def _in_proj_kernel(x_ref, ng_ref, w_ref, qcos_ref, qsa_ref, qsb_ref, kg_ref, cos_ref, sa_ref,
                    sb_ref, bd_ref, sgg_ref, sgw_ref, sgb_ref, wb_ref,
                    q_ref, k_ref, vt_ref, sga_ref, ybg_ref, *, tk):
    rows = x_ref.shape[0]
    hb = _rms(x_ref[...], ng_ref[...]).astype(BF16)

    def proj(lo, hi):
        return jnp.dot(hb, w_ref[:, lo:hi], preferred_element_type=F32)

    cos = cos_ref[...]
    sa = sa_ref[...]
    sb = sb_ref[...]

    def head_norm_rope(z, gain):
        ss = jnp.dot((z * z).astype(BF16), bd_ref[...], preferred_element_type=F32)
        zn = z * lax.rsqrt(ss * (1.0 / HEAD_DIM) + EPS) * gain
        return (zn * cos + pltpu.roll(zn, LANES - ROPE_QUARTER, 1) * sa
                + pltpu.roll(zn, ROPE_QUARTER, 1) * sb)

    zq = proj(0, Q_WIDTH)
    zkv = proj(_OFF_K, _OFF_U)

    qc, qa, qb = qcos_ref[...], qsa_ref[...], qsb_ref[...]
    for j in range(Q_WIDTH // LANES):
        zt = zq[:, j * LANES:(j + 1) * LANES].T
        for hh in range(LANES // HEAD_DIM):
            z = zt[hh * HEAD_DIM:(hh + 1) * HEAD_DIM]
            ss = jnp.sum(z * z, axis=0, keepdims=True)
            zn = z * lax.rsqrt(ss * (1.0 / HEAD_DIM) + EPS)
            up = jnp.concatenate([zn[ROPE_QUARTER:], zn[:ROPE_QUARTER]], axis=0)
            dn = jnp.concatenate([zn[-ROPE_QUARTER:], zn[:-ROPE_QUARTER]], axis=0)
            q_ref[j, hh * HEAD_DIM:(hh + 1) * HEAD_DIM, :] = (
                zn * qc + up * qa + dn * qb).astype(BF16)

    zu = proj(_OFF_U, _OFF_VS)

    kr = head_norm_rope(zkv[:, :KV_WIDTH], kg_ref[...])
    first = _lane_lt64(kr.shape)
    k_ref[0] = jnp.where(first, kr, 0.0).astype(BF16)
    k_ref[1] = jnp.where(first, 0.0, kr).astype(BF16)

    vt = zkv[:, KV_WIDTH:].T
    ones = jnp.ones((VT_ROWS - HEAD_DIM, tk), BF16)
    for g in range(N_KV_HEADS):
        for r in range(rows // tk):
            vt_ref[g, r, :HEAD_DIM, :] = vt[g * HEAD_DIM:(g + 1) * HEAD_DIM,
                                            r * tk:(r + 1) * tk].astype(BF16)
            vt_ref[g, r, HEAD_DIM:, :] = ones

    zvs = proj(_OFF_VS, _OFF_GA)
    u = _gelu(zu)
    zga = proj(_OFF_GA, _OFF_GB)

    vs = _rms(_gelu(zvs), sgg_ref[...]).astype(BF16)
    zgb = proj(_OFF_GB, IN_WIDTH)
    sga_ref[...] = jax.nn.sigmoid(zga).astype(BF16)
    pair = 2 * CHUNK
    lt64 = _lane_lt64((CHUNK, 2 * LANES))
    sgu_rows = []
    for c in range(rows // pair):
        r0 = c * pair
        slabs0, slabs1 = [], []
        for s in range(SG_WIDTH // LANES):
            cols = slice(s * LANES, (s + 1) * LANES)
            rhs = jnp.concatenate([vs[r0:r0 + CHUNK, cols], vs[r0 + CHUNK:r0 + pair, cols]], axis=1)
            oa = jnp.dot(sgw_ref[2 * s], rhs, preferred_element_type=F32)
            ob = jnp.dot(sgw_ref[2 * s + 1], rhs, preferred_element_type=F32)
            bias = sgb_ref[:, cols]
            sp = jnp.where(lt64, oa, ob) + jnp.concatenate([bias, bias], axis=1)
            slabs0.append(u[r0:r0 + CHUNK, cols] * sp[:, :LANES])
            slabs1.append(u[r0 + CHUNK:r0 + pair, cols] * sp[:, LANES:])
        sgu_rows.append(jnp.concatenate(slabs0, axis=1))
        sgu_rows.append(jnp.concatenate(slabs1, axis=1))
    sgu = jnp.concatenate(sgu_rows, axis=0).astype(BF16)
    sgb_gate = jax.nn.sigmoid(zgb)
    yb = jnp.dot(sgu, wb_ref[...], preferred_element_type=F32)
    ybg_ref[...] = (sgb_gate * yb).astype(BF16)


def _in_proj(x, layer, p, seq_len):
    n = x.shape[0]
    tok, _, tk = _tiles(seq_len)
    n_pos = seq_len // tok
    const2 = lambda i: (0, 0)
    layer3 = lambda i: (layer, 0, 0)
    blocks = (tok * D_MODEL * 4 + D_MODEL * IN_WIDTH * 2 + SG_WIDTH * D_MODEL * 2
              + tok * (Q_WIDTH + 2 * LANES + 2 * VT_ROWS + 2 * D_MODEL) * 2)
    return pl.pallas_call(
        functools.partial(_in_proj_kernel, tk=tk),
        grid=(n // tok,),
        in_specs=[
            pl.BlockSpec((tok, D_MODEL), lambda i: (i, 0)),
            pl.BlockSpec((None, 1, D_MODEL), layer3),
            pl.BlockSpec((None, D_MODEL, IN_WIDTH), layer3),
            pl.BlockSpec((None, HEAD_DIM, tok), lambda i: (layer, 0, i % n_pos)),
            pl.BlockSpec((None, HEAD_DIM, tok), lambda i: (layer, 0, i % n_pos)),
            pl.BlockSpec((None, HEAD_DIM, tok), lambda i: (layer, 0, i % n_pos)),
            pl.BlockSpec((None, 1, LANES), layer3),
            pl.BlockSpec((tok, LANES), lambda i: (i % n_pos, 0)),
            pl.BlockSpec((tok, LANES), lambda i: (i % n_pos, 0)),
            pl.BlockSpec((tok, LANES), lambda i: (i % n_pos, 0)),
            pl.BlockSpec((LANES, LANES), const2),
            pl.BlockSpec((None, 1, SG_WIDTH), layer3),
            pl.BlockSpec((None, SG_GROUPS, CHUNK, CHUNK), lambda i: (layer, 0, 0, 0)),
            pl.BlockSpec((None, CHUNK, SG_WIDTH), layer3),
            pl.BlockSpec((None, SG_WIDTH, D_MODEL), layer3),
        ],
        out_specs=[
            pl.BlockSpec((Q_WIDTH // LANES, LANES, tok), lambda i: (0, 0, i)),
            pl.BlockSpec((N_KV_HEADS, tok, LANES), lambda i: (0, i, 0)),
            pl.BlockSpec((N_KV_HEADS, tok // tk, VT_ROWS, tk), lambda i: (0, i, 0, 0)),
            pl.BlockSpec((tok, D_MODEL), lambda i: (i, 0)),
            pl.BlockSpec((tok, D_MODEL), lambda i: (i, 0)),
        ],
        out_shape=[
            jax.ShapeDtypeStruct((Q_WIDTH // LANES, LANES, n), BF16),
            jax.ShapeDtypeStruct((N_KV_HEADS, n, LANES), BF16),
            jax.ShapeDtypeStruct((N_KV_HEADS, n // tk, VT_ROWS, tk), BF16),
            jax.ShapeDtypeStruct((n, D_MODEL), BF16),
            jax.ShapeDtypeStruct((n, D_MODEL), BF16),
        ],
        compiler_params=pltpu.CompilerParams(
            dimension_semantics=("arbitrary",), vmem_limit_bytes=_vmem_limit(blocks)),
        name="in_proj",
    )(x, p["attn_norm_g"], p["w_in"], p["q_cos"], p["q_sa"], p["q_sb"], p["k_norm_g"],
      p["rope_cos"], p["rope_sa"], p["rope_sb"], p["head_ones"],
      p["sg_norm_g"], p["sg_w"], p["sg_bias"], p["w_branch_b"])


def _attn_mix_kernel(q_ref, qn_ref, k_ref, vt_ref, sga_ref, ybg_ref, x_ref, wa_ref, wm_ref,
                     o_ref, acc_ref, m_ref, s0_ref, s1_ref, bm0_ref, bm1_ref, *, tk):
    tq = q_ref.shape[2]
    n_kv = k_ref.shape[1] // tk
    acc_ref[...] = jnp.zeros(acc_ref.shape, F32)
    m_ref[...] = jnp.full(m_ref.shape, NEG_BIG, F32)

    def scores(blk, s_ref, bm_ref, h, qt_ref):
        g, j = divmod(h, Q_PER_KV)
        start = pl.multiple_of(blk * tk, tk)
        kb = k_ref[g, pl.ds(start, tk), :]
        s = jnp.dot(kb, qt_ref[j], preferred_element_type=F32)
        s_ref[h] = s
        bm_ref[h:h + 1, :] = jnp.max(s, axis=0, keepdims=True)

    def accumulate(blk, s_ref, bm_ref, h):
        g = h // Q_PER_KV
        vb = vt_ref[g, blk]
        m_old = m_ref[h:h + 1, :]
        m_new = jnp.maximum(m_old, bm_ref[h:h + 1, :])
        alpha = jnp.exp2(m_old - m_new)
        pt = jnp.exp2(s_ref[h] - m_new).astype(BF16)
        pv = jnp.dot(vb, pt, preferred_element_type=F32)
        acc_ref[h] = alpha * acc_ref[h] + pv
        m_ref[h:h + 1, :] = m_new

    def step(score_blk, score_refs, acc_blk, acc_refs, qt_ref=q_ref):
        for h in range(N_Q_HEADS):
            if score_blk is not None:
                scores(score_blk, *score_refs, h, qt_ref)
            if acc_blk is not None:
                accumulate(acc_blk, *acc_refs, h)

    even = (s0_ref, bm0_ref)
    odd = (s1_ref, bm1_ref)

    @pl.when(pl.program_id(1) == 0)
    def _():
        step(0, even, None, None)

    def kv_pair(t, carry):
        step(2 * t + 1, odd, 2 * t, even)
        step(2 * t + 2, even, 2 * t + 1, odd)
        return carry

    lax.fori_loop(0, n_kv // 2 - 1, kv_pair, 0)
    step(n_kv - 1, odd, n_kv - 2, even)
    step(0, even, n_kv - 1, odd, qn_ref)

    heads = []
    for h in range(N_Q_HEADS):
        a = acc_ref[h]
        heads.append(a[:HEAD_DIM] * (1.0 / a[HEAD_DIM:HEAD_DIM + 1]))
    ot = jnp.concatenate(heads, axis=0)
    ya = jnp.dot(ot.T.astype(BF16), wa_ref[...], preferred_element_type=F32)
    mix = sga_ref[...].astype(F32) * ya + ybg_ref[...].astype(F32)
    o_ref[...] = x_ref[...] + jnp.dot(mix.astype(BF16), wm_ref[...],
                                      preferred_element_type=F32)


def _attn_mix(x, q, kp, vt, sga, ybg, layer, p, batch, seq_len):
    _, tq, tk = _tiles(seq_len)
    n_kv = seq_len // tk
    n_q = seq_len // tq
    row = lambda b, i: (b * n_q + i, 0)
    blocks = (tq * (2 * Q_WIDTH + 2 * D_MODEL) * 2 + 2 * tq * D_MODEL * 4
              + N_KV_HEADS * seq_len * (LANES + VT_ROWS) * 2
              + (Q_WIDTH + D_MODEL) * D_MODEL * 2)
    scratch = N_Q_HEADS * tq * 4 * (VT_ROWS + 2 * tk + 3)
    return pl.pallas_call(
        functools.partial(_attn_mix_kernel, tk=tk),
        grid=(batch, n_q),
        in_specs=[
            pl.BlockSpec((Q_WIDTH // LANES, LANES, tq), lambda b, i: (0, 0, b * n_q + i)),
            pl.BlockSpec((Q_WIDTH // LANES, LANES, tq),
                         lambda b, i: (0, 0, b * n_q + jnp.minimum(i + 1, n_q - 1))),
            pl.BlockSpec((N_KV_HEADS, seq_len, LANES), lambda b, i: (0, b, 0)),
            pl.BlockSpec((N_KV_HEADS, n_kv, VT_ROWS, tk), lambda b, i: (0, b, 0, 0)),
            pl.BlockSpec((tq, D_MODEL), row),
            pl.BlockSpec((tq, D_MODEL), row),
            pl.BlockSpec((tq, D_MODEL), row),
            pl.BlockSpec((None, Q_WIDTH, D_MODEL), lambda b, i: (layer, 0, 0)),
            pl.BlockSpec((None, D_MODEL, D_MODEL), lambda b, i: (layer, 0, 0)),
        ],
        out_specs=pl.BlockSpec((tq, D_MODEL), row),
        out_shape=jax.ShapeDtypeStruct(x.shape, F32),
        scratch_shapes=[
            pltpu.VMEM((N_Q_HEADS, VT_ROWS, tq), F32),
            pltpu.VMEM((N_Q_HEADS, tq), F32),
            pltpu.VMEM((N_Q_HEADS, tk, tq), F32),
            pltpu.VMEM((N_Q_HEADS, tk, tq), F32),
            pltpu.VMEM((N_Q_HEADS, tq), F32),
            pltpu.VMEM((N_Q_HEADS, tq), F32),
        ],
        compiler_params=pltpu.CompilerParams(
            dimension_semantics=("arbitrary", "arbitrary"),
            vmem_limit_bytes=_vmem_limit(blocks, scratch)),
        name="attn_mix",
    )(q, q, kp, vt, sga, ybg, x, p["w_branch_a"], p["w_mix_out"])


FFN_COLS = 512


def _conv_ffn_kernel(xp_ref, x_ref, xn_ref, ng_ref, wu_ref, cw_ref, cb_ref, wd_ref, fg_ref,
                     o_ref, a_ref, *, final):
    i = pl.program_id(1)
    rows = x_ref.shape[0]
    x = x_ref[...]
    xp = jnp.where(i > 0, xp_ref[...], 0.0)
    xn = jnp.where(i < pl.num_programs(1) - 1, xn_ref[...], 0.0)
    hb = _rms(jnp.concatenate([xp, x, xn], axis=0), ng_ref[...]).astype(BF16)

    n_slab = FFN_COLS // LANES

    def up(lo):
        a = jnp.dot(hb, wu_ref[:, lo:lo + FFN_COLS], preferred_element_type=F32)
        for j in range(n_slab):
            a_ref[lo // LANES + j] = a[:, j * LANES:(j + 1) * LANES]

    def conv(lo):
        outs = []
        for j in range(n_slab):
            slab = lo // LANES + j
            cols = slice(lo + j * LANES, lo + (j + 1) * LANES)
            w = cw_ref[:, cols]
            outs.append(a_ref[slab, HALO - 1:HALO - 1 + rows, :] * w[0:1]
                        + a_ref[slab, HALO:HALO + rows, :] * w[1:2]
                        + a_ref[slab, HALO + 1:HALO + 1 + rows, :] * w[2:3] + cb_ref[:, cols])
        return jnp.concatenate(outs, axis=1)

    n_chunk = D_FF // FFN_COLS
    up(0)
    up(D_FF)
    y = x
    for c in range(n_chunk):
        if c + 1 < n_chunk:
            up((c + 1) * FFN_COLS)
            up(D_FF + (c + 1) * FFN_COLS)
        gate = _gelu(conv(c * FFN_COLS)) * conv(D_FF + c * FFN_COLS)
        y = y + jnp.dot(gate.astype(BF16), wd_ref[c * FFN_COLS:(c + 1) * FFN_COLS, :],
                        preferred_element_type=F32)
    o_ref[...] = _rms(y, fg_ref[...]) if final else y


def _conv_ffn(x, layer, p, batch, seq_len, final):
    tok, _, _ = _tiles(seq_len)
    n_t = seq_len // tok
    per = tok // HALO
    n_halo = seq_len // HALO
    row = lambda b, i: (b * n_t + i, 0)
    prev = lambda b, i: (b * n_halo + jnp.maximum(i * per - 1, 0), 0)
    nxt = lambda b, i: (b * n_halo + jnp.minimum((i + 1) * per, n_halo - 1), 0)
    layer3 = lambda b, i: (layer, 0, 0)
    blocks = 2 * tok * D_MODEL * 4 + 3 * D_MODEL * D_FF * 2
    return pl.pallas_call(
        functools.partial(_conv_ffn_kernel, final=final),
        grid=(batch, n_t),
        in_specs=[
            pl.BlockSpec((HALO, D_MODEL), prev),
            pl.BlockSpec((tok, D_MODEL), row),
            pl.BlockSpec((HALO, D_MODEL), nxt),
            pl.BlockSpec((None, 1, D_MODEL), layer3),
            pl.BlockSpec((None, D_MODEL, 2 * D_FF), layer3),
            pl.BlockSpec((None, CONV_W, 2 * D_FF), layer3),
            pl.BlockSpec((None, 1, 2 * D_FF), layer3),
            pl.BlockSpec((None, D_FF, D_MODEL), layer3),
            pl.BlockSpec((1, D_MODEL), lambda b, i: (0, 0)),
        ],
        out_specs=pl.BlockSpec((tok, D_MODEL), row),
        out_shape=jax.ShapeDtypeStruct(x.shape, F32),
        scratch_shapes=[pltpu.VMEM((2 * D_FF // LANES, tok + 2 * HALO, LANES), F32)],
        compiler_params=pltpu.CompilerParams(
            dimension_semantics=("arbitrary", "arbitrary"),
            vmem_limit_bytes=_vmem_limit(blocks, 2 * D_FF * (tok + 2 * HALO) * 4)),
        name="conv_ffn",
    )(x, x, x, p["ffn_norm_g"], p["w_up"], p["conv_w"], p["conv_b"], p["w_down"],
      p["final_norm_g"])


def _rope_tables(seq_len):
    t = jnp.arange(seq_len)
    half = HEAD_DIM // 2
    freq = ROPE_THETA ** (-jnp.arange(0, half, 2, dtype=F32) / half)
    ang_r = (t // GRID_W).astype(F32)[:, None] * freq[None, :]
    ang_c = (t % GRID_W).astype(F32)[:, None] * freq[None, :]
    cr, sr, cc, sc = jnp.cos(ang_r), jnp.sin(ang_r), jnp.cos(ang_c), jnp.sin(ang_c)
    zero = jnp.zeros_like(sr)
    per_head = lambda parts: jnp.tile(jnp.concatenate(parts, axis=1), (1, LANES // HEAD_DIM))
    cos = per_head([cr, cr, cc, cc])
    sa = per_head([-sr, zero, -sc, zero])
    sb = per_head([zero, sr, zero, sc])
    return cos, sa, sb


def _prepare_params(attn_norm_g, w_in, q_norm_g, k_norm_g, sg_norm_g, sg_w, sg_b,
                    w_branch_a, w_branch_b, w_mix_out, ffn_norm_g, w_up, conv_w, conv_b,
                    w_down, final_norm_g):
    depth = w_in.shape[0]
    order = np.concatenate([
        np.arange(h * HEAD_DIM, (h + 1) * HEAD_DIM)
        for j in range(Q_PER_KV) for h in (j, Q_PER_KV + j)])
    w_in_p = jnp.concatenate([w_in[:, :, order], w_in[:, :, Q_WIDTH:]], axis=2).astype(BF16)
    two_heads = lambda g: jnp.tile(g, (1, LANES // HEAD_DIM)).reshape(depth, 1, LANES)
    blk = np.arange(LANES) // HEAD_DIM
    return dict(
        attn_norm_g=attn_norm_g.reshape(depth, 1, D_MODEL),
        w_in=w_in_p,
        q_norm_g=q_norm_g,
        k_norm_g=two_heads(k_norm_g),
        head_ones=jnp.asarray(blk[:, None] == blk[None, :], BF16),
        sg_norm_g=sg_norm_g.reshape(depth, 1, SG_WIDTH),
        sg_w=sg_w.astype(BF16),
        sg_bias=jnp.repeat(jnp.swapaxes(sg_b, 1, 2), SG_GROUP_DIM, axis=2),
        w_branch_a=w_branch_a.astype(BF16),
        w_branch_b=w_branch_b.astype(BF16),
        w_mix_out=w_mix_out.astype(BF16),
        ffn_norm_g=ffn_norm_g.reshape(depth, 1, D_MODEL),
        w_up=w_up.astype(BF16),
        conv_w=conv_w,
        conv_b=conv_b.reshape(depth, 1, 2 * D_FF),
        w_down=w_down.astype(BF16),
        final_norm_g=final_norm_g.reshape(1, D_MODEL),
    )


def _trunk(x3, p):
    batch, seq_len, _ = x3.shape
    depth = p["w_in"].shape[0]
    tok, tq, tk = _tiles(seq_len)
    assert seq_len % tok == 0 and seq_len % tq == 0 and seq_len % (2 * tk) == 0
    assert tok % (2 * CHUNK) == 0 and tok % tk == 0 and tok % HALO == 0
    p = dict(p)
    p["rope_cos"], p["rope_sa"], p["rope_sb"] = _rope_tables(seq_len)
    g = p["q_norm_g"][:, :, None] * Q_SCALE
    head = lambda t: t[:, :HEAD_DIM].T[None]
    p["q_cos"] = head(p["rope_cos"]) * g
    p["q_sa"] = head(p["rope_sa"]) * jnp.roll(g, -ROPE_QUARTER, axis=1)
    p["q_sb"] = head(p["rope_sb"]) * jnp.roll(g, ROPE_QUARTER, axis=1)
    x = x3.reshape(batch * seq_len, D_MODEL)
    for layer in range(depth):
        q, kp, vt, sga, ybg = _in_proj(x, layer, p, seq_len)
        x = _attn_mix(x, q, kp, vt, sga, ybg, layer, p, batch, seq_len)
        x = _conv_ffn(x, layer, p, batch, seq_len, final=(layer == depth - 1))
    return x.reshape(batch, seq_len, D_MODEL)


def kernel(x_prompt, x_sample, attn_norm_g, w_in, q_norm_g, k_norm_g, sg_norm_g, sg_w, sg_b,
           w_branch_a, w_branch_b, w_mix_out, ffn_norm_g, w_up, conv_w, conv_b, w_down,
           final_norm_g):
    p = _prepare_params(attn_norm_g, w_in, q_norm_g, k_norm_g, sg_norm_g, sg_w, sg_b,
                        w_branch_a, w_branch_b, w_mix_out, ffn_norm_g, w_up, conv_w, conv_b,
                        w_down, final_norm_g)
    return (_trunk(x_prompt, p), _trunk(x_sample, p))
```

```python
import functools
import math

import jax
import jax.numpy as jnp
import numpy as np
from jax import lax
from jax.experimental import pallas as pl
from jax.experimental.pallas import tpu as pltpu

D_MODEL = 1024
GRID_W = 64
HEAD_DIM = 64
N_Q_HEADS = 8
N_KV_HEADS = 2
Q_PER_KV = N_Q_HEADS // N_KV_HEADS
Q_WIDTH = N_Q_HEADS * HEAD_DIM
KV_WIDTH = N_KV_HEADS * HEAD_DIM
SG_GROUPS = 8
SG_GROUP_DIM = 64
SG_WIDTH = SG_GROUPS * SG_GROUP_DIM
CHUNK = 128
D_FF = 2048
CONV_W = 3
ROPE_THETA = 10000.0
EPS = 1e-6
IN_WIDTH = Q_WIDTH + 2 * KV_WIDTH + 2 * SG_WIDTH + 2 * D_MODEL

LANES = 128
F32_SUBLANES = 8
BF16_SUBLANES = 16
V7X_VMEM_BYTES = 64 * 1024 * 1024

_OFF_K = Q_WIDTH
_OFF_V = _OFF_K + KV_WIDTH
_OFF_U = _OFF_V + KV_WIDTH
_OFF_VS = _OFF_U + SG_WIDTH
_OFF_GA = _OFF_VS + SG_WIDTH
_OFF_GB = _OFF_GA + D_MODEL

VT_ROWS = HEAD_DIM + BF16_SUBLANES
ROPE_QUARTER = HEAD_DIM // 4
NEG_BIG = -1e30
Q_SCALE = math.log2(math.e) / math.sqrt(HEAD_DIM)
HALO = F32_SUBLANES

F32 = jnp.float32
BF16 = jnp.bfloat16


def _tiles(seq_len):
    tok = min(512, seq_len)
    tq = min(512, seq_len)
    tk = min(256, seq_len // 2)
    return tok, tq, tk


def _vmem_limit(block_bytes, scratch_bytes=0):
    return int(min(V7X_VMEM_BYTES - (4 << 20), 2 * block_bytes + scratch_bytes + (24 << 20)))


def _rms(x, g):
    return x * lax.rsqrt(jnp.mean(x * x, axis=-1, keepdims=True) + EPS) * g


def _gelu(x):
    c = math.sqrt(2.0 / math.pi)
    half = 0.5 * x
    return half + half * jnp.tanh(x * (c + (c * 0.044715) * (x * x)))


def _lane_lt64(shape):
    lane = lax.broadcasted_iota(jnp.int32, shape, len(shape) - 1)
    return (lane % LANES) < HEAD_DIM


def _in_proj_kernel(x_ref, ng_ref, w_ref, qcos_ref, qsa_ref, qsb_ref, kg_ref, cos_ref, sa_ref,
                    sb_ref, bd_ref, sgg_ref, sgw_ref, sgb_ref, wb_ref,
                    q_ref, k_ref, vt_ref, sga_ref, ybg_ref, *, tk):
    rows = x_ref.shape[0]
    hb = _rms(x_ref[...], ng_ref[...]).astype(BF16)

    def proj(lo, hi):
        return jnp.dot(hb, w_ref[:, lo:hi], preferred_element_type=F32)

    cos = cos_ref[...]
    sa = sa_ref[...]
    sb = sb_ref[...]

    def head_norm_rope(z, gain):
        ss = jnp.dot((z * z).astype(BF16), bd_ref[...], preferred_element_type=F32)
        zn = z * lax.rsqrt(ss * (1.0 / HEAD_DIM) + EPS) * gain
        return (zn * cos + pltpu.roll(zn, LANES - ROPE_QUARTER, 1) * sa
                + pltpu.roll(zn, ROPE_QUARTER, 1) * sb)

    zq = proj(0, Q_WIDTH)
    zkv = proj(_OFF_K, _OFF_U)

    qc, qa, qb = qcos_ref[...], qsa_ref[...], qsb_ref[...]
    for j in range(Q_WIDTH // LANES):
        zt = zq[:, j * LANES:(j + 1) * LANES].T
        for hh in range(LANES // HEAD_DIM):
            z = zt[hh * HEAD_DIM:(hh + 1) * HEAD_DIM]
            ss = jnp.sum(z * z, axis=0, keepdims=True)
            zn = z * lax.rsqrt(ss * (1.0 / HEAD_DIM) + EPS)
            up = jnp.concatenate([zn[ROPE_QUARTER:], zn[:ROPE_QUARTER]], axis=0)
            dn = jnp.concatenate([zn[-ROPE_QUARTER:], zn[:-ROPE_QUARTER]], axis=0)
            q_ref[j, hh * HEAD_DIM:(hh + 1) * HEAD_DIM, :] = (
                zn * qc + up * qa + dn * qb).astype(BF16)

    zu = proj(_OFF_U, _OFF_VS)

    kr = head_norm_rope(zkv[:, :KV_WIDTH], kg_ref[...])
    first = _lane_lt64(kr.shape)
    k_ref[0] = jnp.where(first, kr, 0.0).astype(BF16)
    k_ref[1] = jnp.where(first, 0.0, kr).astype(BF16)

    vt = zkv[:, KV_WIDTH:].T
    ones = jnp.ones((VT_ROWS - HEAD_DIM, tk), BF16)
    for g in range(N_KV_HEADS):
        for r in range(rows // tk):
            vt_ref[g, r, :HEAD_DIM, :] = vt[g * HEAD_DIM:(g + 1) * HEAD_DIM,
                                            r * tk:(r + 1) * tk].astype(BF16)
            vt_ref[g, r, HEAD_DIM:, :] = ones

    zvs = proj(_OFF_VS, _OFF_GA)
    u = _gelu(zu)
    zga = proj(_OFF_GA, _OFF_GB)

    vs = _rms(_gelu(zvs), sgg_ref[...]).astype(BF16)
    zgb = proj(_OFF_GB, IN_WIDTH)
    sga_ref[...] = jax.nn.sigmoid(zga).astype(BF16)
    pair = 2 * CHUNK
    lt64 = _lane_lt64((CHUNK, 2 * LANES))
    sgu_rows = []
    for c in range(rows // pair):
        r0 = c * pair
        slabs0, slabs1 = [], []
        for s in range(SG_WIDTH // LANES):
            cols = slice(s * LANES, (s + 1) * LANES)
            rhs = jnp.concatenate([vs[r0:r0 + CHUNK, cols], vs[r0 + CHUNK:r0 + pair, cols]], axis=1)
            oa = jnp.dot(sgw_ref[2 * s], rhs, preferred_element_type=F32)
            ob = jnp.dot(sgw_ref[2 * s + 1], rhs, preferred_element_type=F32)
            bias = sgb_ref[:, cols]
            sp = jnp.where(lt64, oa, ob) + jnp.concatenate([bias, bias], axis=1)
            slabs0.append(u[r0:r0 + CHUNK, cols] * sp[:, :LANES])
            slabs1.append(u[r0 + CHUNK:r0 + pair, cols] * sp[:, LANES:])
        sgu_rows.append(jnp.concatenate(slabs0, axis=1))
        sgu_rows.append(jnp.concatenate(slabs1, axis=1))
    sgu = jnp.concatenate(sgu_rows, axis=0).astype(BF16)
    sgb_gate = jax.nn.sigmoid(zgb)
    yb = jnp.dot(sgu, wb_ref[...], preferred_element_type=F32)
    ybg_ref[...] = (sgb_gate * yb).astype(BF16)


def _in_proj(x, layer, p, seq_len):
    n = x.shape[0]
    tok, _, tk = _tiles(seq_len)
    n_pos = seq_len // tok
    const2 = lambda i: (0, 0)
    layer3 = lambda i: (layer, 0, 0)
    blocks = (tok * D_MODEL * 4 + D_MODEL * IN_WIDTH * 2 + SG_WIDTH * D_MODEL * 2
              + tok * (Q_WIDTH + 2 * LANES + 2 * VT_ROWS + 2 * D_MODEL) * 2)
    return pl.pallas_call(
        functools.partial(_in_proj_kernel, tk=tk),
        grid=(n // tok,),
        in_specs=[
            pl.BlockSpec((tok, D_MODEL), lambda i: (i, 0)),
            pl.BlockSpec((None, 1, D_MODEL), layer3),
            pl.BlockSpec((None, D_MODEL, IN_WIDTH), layer3),
            pl.BlockSpec((None, HEAD_DIM, tok), lambda i: (layer, 0, i % n_pos)),
            pl.BlockSpec((None, HEAD_DIM, tok), lambda i: (layer, 0, i % n_pos)),
            pl.BlockSpec((None, HEAD_DIM, tok), lambda i: (layer, 0, i % n_pos)),
            pl.BlockSpec((None, 1, LANES), layer3),
            pl.BlockSpec((tok, LANES), lambda i: (i % n_pos, 0)),
            pl.BlockSpec((tok, LANES), lambda i: (i % n_pos, 0)),
            pl.BlockSpec((tok, LANES), lambda i: (i % n_pos, 0)),
            pl.BlockSpec((LANES, LANES), const2),
            pl.BlockSpec((None, 1, SG_WIDTH), layer3),
            pl.BlockSpec((None, SG_GROUPS, CHUNK, CHUNK), lambda i: (layer, 0, 0, 0)),
            pl.BlockSpec((None, CHUNK, SG_WIDTH), layer3),
            pl.BlockSpec((None, SG_WIDTH, D_MODEL), layer3),
        ],
        out_specs=[
            pl.BlockSpec((Q_WIDTH // LANES, LANES, tok), lambda i: (0, 0, i)),
            pl.BlockSpec((N_KV_HEADS, tok, LANES), lambda i: (0, i, 0)),
            pl.BlockSpec((N_KV_HEADS, tok // tk, VT_ROWS, tk), lambda i: (0, i, 0, 0)),
            pl.BlockSpec((tok, D_MODEL), lambda i: (i, 0)),
            pl.BlockSpec((tok, D_MODEL), lambda i: (i, 0)),
        ],
        out_shape=[
            jax.ShapeDtypeStruct((Q_WIDTH // LANES, LANES, n), BF16),
            jax.ShapeDtypeStruct((N_KV_HEADS, n, LANES), BF16),
            jax.ShapeDtypeStruct((N_KV_HEADS, n // tk, VT_ROWS, tk), BF16),
            jax.ShapeDtypeStruct((n, D_MODEL), BF16),
            jax.ShapeDtypeStruct((n, D_MODEL), BF16),
        ],
        compiler_params=pltpu.CompilerParams(
            dimension_semantics=("arbitrary",), vmem_limit_bytes=_vmem_limit(blocks)),
        name="in_proj",
    )(x, p["attn_norm_g"], p["w_in"], p["q_cos"], p["q_sa"], p["q_sb"], p["k_norm_g"],
      p["rope_cos"], p["rope_sa"], p["rope_sb"], p["head_ones"],
      p["sg_norm_g"], p["sg_w"], p["sg_bias"], p["w_branch_b"])


def _attn_mix_kernel(q_ref, qn_ref, k_ref, vt_ref, sga_ref, ybg_ref, x_ref, wa_ref, wm_ref,
                     o_ref, acc_ref, m_ref, s0_ref, s1_ref, bm0_ref, bm1_ref, *, tk):
    tq = q_ref.shape[2]
    n_kv = k_ref.shape[1] // tk
    acc_ref[...] = jnp.zeros(acc_ref.shape, F32)
    m_ref[...] = jnp.full(m_ref.shape, NEG_BIG, F32)

    def scores(blk, s_ref, bm_ref, h, qt_ref):
        g, j = divmod(h, Q_PER_KV)
        start = pl.multiple_of(blk * tk, tk)
        kb = k_ref[g, pl.ds(start, tk), :]
        s = jnp.dot(kb, qt_ref[j], preferred_element_type=F32)
        s_ref[h] = s
        bm_ref[h:h + 1, :] = jnp.max(s, axis=0, keepdims=True)

    def accumulate(blk, s_ref, bm_ref, h):
        g = h // Q_PER_KV
        vb = vt_ref[g, blk]
        m_old = m_ref[h:h + 1, :]
        m_new = jnp.maximum(m_old, bm_ref[h:h + 1, :])
        alpha = jnp.exp2(m_old - m_new)
        pt = jnp.exp2(s_ref[h] - m_new).astype(BF16)
        pv = jnp.dot(vb, pt, preferred_element_type=F32)
        acc_ref[h] = alpha * acc_ref[h] + pv
        m_ref[h:h + 1, :] = m_new

    def step(score_blk, score_refs, acc_blk, acc_refs, qt_ref=q_ref):
        for h in range(N_Q_HEADS):
            if score_blk is not None:
                scores(score_blk, *score_refs, h, qt_ref)
            if acc_blk is not None:
                accumulate(acc_blk, *acc_refs, h)

    even = (s0_ref, bm0_ref)
    odd = (s1_ref, bm1_ref)

    @pl.when(pl.program_id(1) == 0)
    def _():
        step(0, even, None, None)

    def kv_pair(t, carry):
        step(2 * t + 1, odd, 2 * t, even)
        step(2 * t + 2, even, 2 * t + 1, odd)
        return carry

    lax.fori_loop(0, n_kv // 2 - 1, kv_pair, 0)
    step(n_kv - 1, odd, n_kv - 2, even)
    step(0, even, n_kv - 1, odd, qn_ref)

    heads = []
    for h in range(N_Q_HEADS):
        a = acc_ref[h]
        heads.append(a[:HEAD_DIM] * (1.0 / a[HEAD_DIM:HEAD_DIM + 1]))
    ot = jnp.concatenate(heads, axis=0)
    ya = jnp.dot(ot.T.astype(BF16), wa_ref[...], preferred_element_type=F32)
    mix = sga_ref[...].astype(F32) * ya + ybg_ref[...].astype(F32)
    o_ref[...] = x_ref[...] + jnp.dot(mix.astype(BF16), wm_ref[...],
                                      preferred_element_type=F32)


def _attn_mix(x, q, kp, vt, sga, ybg, layer, p, batch, seq_len):
    _, tq, tk = _tiles(seq_len)
    n_kv = seq_len // tk
    n_q = seq_len // tq
    row = lambda b, i: (b * n_q + i, 0)
    blocks = (tq * (2 * Q_WIDTH + 2 * D_MODEL) * 2 + 2 * tq * D_MODEL * 4
              + N_KV_HEADS * seq_len * (LANES + VT_ROWS) * 2
              + (Q_WIDTH + D_MODEL) * D_MODEL * 2)
    scratch = N_Q_HEADS * tq * 4 * (VT_ROWS + 2 * tk + 3)
    return pl.pallas_call(
        functools.partial(_attn_mix_kernel, tk=tk),
        grid=(batch, n_q),
        in_specs=[
            pl.BlockSpec((Q_WIDTH // LANES, LANES, tq), lambda b, i: (0, 0, b * n_q + i)),
            pl.BlockSpec((Q_WIDTH // LANES, LANES, tq),
                         lambda b, i: (0, 0, b * n_q + jnp.minimum(i + 1, n_q - 1))),
            pl.BlockSpec((N_KV_HEADS, seq_len, LANES), lambda b, i: (0, b, 0)),
            pl.BlockSpec((N_KV_HEADS, n_kv, VT_ROWS, tk), lambda b, i: (0, b, 0, 0)),
            pl.BlockSpec((tq, D_MODEL), row),
            pl.BlockSpec((tq, D_MODEL), row),
            pl.BlockSpec((tq, D_MODEL), row),
            pl.BlockSpec((None, Q_WIDTH, D_MODEL), lambda b, i: (layer, 0, 0)),
            pl.BlockSpec((None, D_MODEL, D_MODEL), lambda b, i: (layer, 0, 0)),
        ],
        out_specs=pl.BlockSpec((tq, D_MODEL), row),
        out_shape=jax.ShapeDtypeStruct(x.shape, F32),
        scratch_shapes=[
            pltpu.VMEM((N_Q_HEADS, VT_ROWS, tq), F32),
            pltpu.VMEM((N_Q_HEADS, tq), F32),
            pltpu.VMEM((N_Q_HEADS, tk, tq), F32),
            pltpu.VMEM((N_Q_HEADS, tk, tq), F32),
            pltpu.VMEM((N_Q_HEADS, tq), F32),
            pltpu.VMEM((N_Q_HEADS, tq), F32),
        ],
        compiler_params=pltpu.CompilerParams(
            dimension_semantics=("arbitrary", "arbitrary"),
            vmem_limit_bytes=_vmem_limit(blocks, scratch)),
        name="attn_mix",
    )(q, q, kp, vt, sga, ybg, x, p["w_branch_a"], p["w_mix_out"])


FFN_COLS = 512


def _conv_ffn_kernel(xp_ref, x_ref, xn_ref, ng_ref, wu_ref, cw_ref, cb_ref, wd_ref, fg_ref,
                     o_ref, a_ref, *, final):
    i = pl.program_id(1)
    rows = x_ref.shape[0]
    x = x_ref[...]
    xp = jnp.where(i > 0, xp_ref[...], 0.0)
    xn = jnp.where(i < pl.num_programs(1) - 1, xn_ref[...], 0.0)
    hb = _rms(jnp.concatenate([xp, x, xn], axis=0), ng_ref[...]).astype(BF16)

    n_slab = FFN_COLS // LANES

    def up(lo):
        a = jnp.dot(hb, wu_ref[:, lo:lo + FFN_COLS], preferred_element_type=F32)
        for j in range(n_slab):
            a_ref[lo // LANES + j] = a[:, j * LANES:(j + 1) * LANES]

    def conv(lo, width):
        outs = []
        for j in range(width // LANES):
            slab = lo // LANES + j
            cols = slice(lo + j * LANES, lo + (j + 1) * LANES)
            w = cw_ref[:, cols]
            outs.append(a_ref[slab, HALO - 1:HALO - 1 + rows, :] * w[0:1]
                        + a_ref[slab, HALO:HALO + rows, :] * w[1:2]
                        + a_ref[slab, HALO + 1:HALO + 1 + rows, :] * w[2:3] + cb_ref[:, cols])
        return jnp.concatenate(outs, axis=1)

    n_chunk = D_FF // FFN_COLS
    half = FFN_COLS // 2
    up(0)
    up(D_FF)
    y = x
    for c in range(n_chunk):
        if c + 1 < n_chunk:
            up((c + 1) * FFN_COLS)
            up(D_FF + (c + 1) * FFN_COLS)
        for lo in (c * FFN_COLS, c * FFN_COLS + half):
            gate = _gelu(conv(lo, half)) * conv(D_FF + lo, half)
            y = y + jnp.dot(gate.astype(BF16), wd_ref[lo:lo + half, :D_MODEL],
                            preferred_element_type=F32)
    o_ref[...] = _rms(y, fg_ref[...]) if final else y


def _conv_ffn(x, layer, p, batch, seq_len, final):
    tok, _, _ = _tiles(seq_len)
    n_t = seq_len // tok
    per = tok // HALO
    n_halo = seq_len // HALO
    row = lambda b, i: (b * n_t + i, 0)
    prev = lambda b, i: (b * n_halo + jnp.maximum(i * per - 1, 0), 0)
    nxt = lambda b, i: (b * n_halo + jnp.minimum((i + 1) * per, n_halo - 1), 0)
    layer3 = lambda b, i: (layer, 0, 0)
    blocks = 2 * tok * D_MODEL * 4 + 3 * D_MODEL * D_FF * 2
    return pl.pallas_call(
        functools.partial(_conv_ffn_kernel, final=final),
        grid=(batch, n_t),
        in_specs=[
            pl.BlockSpec((HALO, D_MODEL), prev),
            pl.BlockSpec((tok, D_MODEL), row),
            pl.BlockSpec((HALO, D_MODEL), nxt),
            pl.BlockSpec((None, 1, D_MODEL), layer3),
            pl.BlockSpec((None, D_MODEL, 2 * D_FF + LANES), layer3),
            pl.BlockSpec((None, CONV_W, 2 * D_FF), layer3),
            pl.BlockSpec((None, 1, 2 * D_FF), layer3),
            pl.BlockSpec((None, D_FF, D_MODEL + LANES), layer3),
            pl.BlockSpec((1, D_MODEL), lambda b, i: (0, 0)),
        ],
        out_specs=pl.BlockSpec((tok, D_MODEL), row),
        out_shape=jax.ShapeDtypeStruct(x.shape, F32),
        scratch_shapes=[pltpu.VMEM((2 * D_FF // LANES, tok + 2 * HALO, LANES), F32)],
        compiler_params=pltpu.CompilerParams(
            dimension_semantics=("arbitrary", "arbitrary"),
            vmem_limit_bytes=_vmem_limit(blocks, 2 * D_FF * (tok + 2 * HALO) * 4)),
        name="conv_ffn",
    )(x, x, x, p["ffn_norm_g"], p["w_up"], p["conv_w"], p["conv_b"], p["w_down"],
      p["final_norm_g"])


def _rope_tables(seq_len):
    t = jnp.arange(seq_len)
    half = HEAD_DIM // 2
    freq = ROPE_THETA ** (-jnp.arange(0, half, 2, dtype=F32) / half)
    ang_r = (t // GRID_W).astype(F32)[:, None] * freq[None, :]
    ang_c = (t % GRID_W).astype(F32)[:, None] * freq[None, :]
    cr, sr, cc, sc = jnp.cos(ang_r), jnp.sin(ang_r), jnp.cos(ang_c), jnp.sin(ang_c)
    zero = jnp.zeros_like(sr)
    per_head = lambda parts: jnp.tile(jnp.concatenate(parts, axis=1), (1, LANES // HEAD_DIM))
    cos = per_head([cr, cr, cc, cc])
    sa = per_head([-sr, zero, -sc, zero])
    sb = per_head([zero, sr, zero, sc])
    return cos, sa, sb


def _prepare_params(attn_norm_g, w_in, q_norm_g, k_norm_g, sg_norm_g, sg_w, sg_b,
                    w_branch_a, w_branch_b, w_mix_out, ffn_norm_g, w_up, conv_w, conv_b,
                    w_down, final_norm_g):
    depth = w_in.shape[0]
    order = np.concatenate([
        np.arange(h * HEAD_DIM, (h + 1) * HEAD_DIM)
        for j in range(Q_PER_KV) for h in (j, Q_PER_KV + j)])
    w_in_p = jnp.concatenate([w_in[:, :, order], w_in[:, :, Q_WIDTH:]], axis=2).astype(BF16)
    pad_cols = lambda w: jnp.pad(w.astype(BF16), ((0, 0), (0, 0), (0, LANES)))
    two_heads = lambda g: jnp.tile(g, (1, LANES // HEAD_DIM)).reshape(depth, 1, LANES)
    blk = np.arange(LANES) // HEAD_DIM
    return dict(
        attn_norm_g=attn_norm_g.reshape(depth, 1, D_MODEL),
        w_in=w_in_p,
        q_norm_g=q_norm_g,
        k_norm_g=two_heads(k_norm_g),
        head_ones=jnp.asarray(blk[:, None] == blk[None, :], BF16),
        sg_norm_g=sg_norm_g.reshape(depth, 1, SG_WIDTH),
        sg_w=sg_w.astype(BF16),
        sg_bias=jnp.repeat(jnp.swapaxes(sg_b, 1, 2), SG_GROUP_DIM, axis=2),
        w_branch_a=w_branch_a.astype(BF16),
        w_branch_b=w_branch_b.astype(BF16),
        w_mix_out=w_mix_out.astype(BF16),
        ffn_norm_g=ffn_norm_g.reshape(depth, 1, D_MODEL),
        w_up=pad_cols(w_up),
        conv_w=conv_w,
        conv_b=conv_b.reshape(depth, 1, 2 * D_FF),
        w_down=pad_cols(w_down),
        final_norm_g=final_norm_g.reshape(1, D_MODEL),
    )


def _trunk(x3, p):
    batch, seq_len, _ = x3.shape
    depth = p["w_in"].shape[0]
    tok, tq, tk = _tiles(seq_len)
    assert seq_len % tok == 0 and seq_len % tq == 0 and seq_len % (2 * tk) == 0
    assert tok % (2 * CHUNK) == 0 and tok % tk == 0 and tok % HALO == 0
    p = dict(p)
    p["rope_cos"], p["rope_sa"], p["rope_sb"] = _rope_tables(seq_len)
    g = p["q_norm_g"][:, :, None] * Q_SCALE
    head = lambda t: t[:, :HEAD_DIM].T[None]
    p["q_cos"] = head(p["rope_cos"]) * g
    p["q_sa"] = head(p["rope_sa"]) * jnp.roll(g, -ROPE_QUARTER, axis=1)
    p["q_sb"] = head(p["rope_sb"]) * jnp.roll(g, ROPE_QUARTER, axis=1)
    x = x3.reshape(batch * seq_len, D_MODEL)
    for layer in range(depth):
        q, kp, vt, sga, ybg = _in_proj(x, layer, p, seq_len)
        x = _attn_mix(x, q, kp, vt, sga, ybg, layer, p, batch, seq_len)
        x = _conv_ffn(x, layer, p, batch, seq_len, final=(layer == depth - 1))
    return x.reshape(batch, seq_len, D_MODEL)


def kernel(x_prompt, x_sample, attn_norm_g, w_in, q_norm_g, k_norm_g, sg_norm_g, sg_w, sg_b,
           w_branch_a, w_branch_b, w_mix_out, ffn_norm_g, w_up, conv_w, conv_b, w_down,
           final_norm_g):
    p = _prepare_params(attn_norm_g, w_in, q_norm_g, k_norm_g, sg_norm_g, sg_w, sg_b,
                        w_branch_a, w_branch_b, w_mix_out, ffn_norm_g, w_up, conv_w, conv_b,
                        w_down, final_norm_g)
    return (_trunk(x_prompt, p), _trunk(x_sample, p))
```

```python
import functools
import math

import jax
import jax.numpy as jnp
import numpy as np
from jax import lax
from jax.experimental import pallas as pl
from jax.experimental.pallas import tpu as pltpu

D_MODEL = 1024
GRID_W = 64
HEAD_DIM = 64
N_Q_HEADS = 8
N_KV_HEADS = 2
Q_PER_KV = N_Q_HEADS // N_KV_HEADS
Q_WIDTH = N_Q_HEADS * HEAD_DIM
KV_WIDTH = N_KV_HEADS * HEAD_DIM
SG_GROUPS = 8
SG_GROUP_DIM = 64
SG_WIDTH = SG_GROUPS * SG_GROUP_DIM
CHUNK = 128
D_FF = 2048
CONV_W = 3
ROPE_THETA = 10000.0
EPS = 1e-6
IN_WIDTH = Q_WIDTH + 2 * KV_WIDTH + 2 * SG_WIDTH + 2 * D_MODEL

LANES = 128
F32_SUBLANES = 8
BF16_SUBLANES = 16
V7X_VMEM_BYTES = 64 * 1024 * 1024

_OFF_K = Q_WIDTH
_OFF_V = _OFF_K + KV_WIDTH
_OFF_U = _OFF_V + KV_WIDTH
_OFF_VS = _OFF_U + SG_WIDTH
_OFF_GA = _OFF_VS + SG_WIDTH
_OFF_GB = _OFF_GA + D_MODEL

VT_ROWS = HEAD_DIM + BF16_SUBLANES
ROPE_QUARTER = HEAD_DIM // 4
NEG_BIG = -1e30
Q_SCALE = math.log2(math.e) / math.sqrt(HEAD_DIM)
HALO = F32_SUBLANES

F32 = jnp.float32
BF16 = jnp.bfloat16


def _tiles(seq_len):
    tok = min(512, seq_len)
    tq = min(512, seq_len)
    tk = min(256, seq_len // 2)
    return tok, tq, tk


VMEM_RESERVE_BYTES = 4 << 20
VMEM_TEMP_BYTES = 24 << 20


def _vmem_limit(block_bytes, scratch_bytes=0):
    want = 2 * block_bytes + scratch_bytes + VMEM_TEMP_BYTES
    return int(min(V7X_VMEM_BYTES - VMEM_RESERVE_BYTES, want))


def _rms(x, g):
    return x * lax.rsqrt(jnp.mean(x * x, axis=-1, keepdims=True) + EPS) * g


def _gelu(x):
    c = math.sqrt(2.0 / math.pi)
    half = 0.5 * x
    return half + half * jnp.tanh(x * (c + (c * 0.044715) * (x * x)))


def _lane_lt64(shape):
    lane = lax.broadcasted_iota(jnp.int32, shape, len(shape) - 1)
    return (lane % LANES) < HEAD_DIM


def _in_proj_kernel(x_ref, ng_ref, w_ref, qcos_ref, qsa_ref, qsb_ref, kg_ref, cos_ref, sa_ref,
                    sb_ref, bd_ref, sgg_ref, sgw_ref, sgb_ref, wb_ref,
                    q_ref, k_ref, vt_ref, sga_ref, ybg_ref, *, tk):
    rows = x_ref.shape[0]
    hb = _rms(x_ref[...], ng_ref[...]).astype(BF16)

    def proj(lo, hi):
        return jnp.dot(hb, w_ref[:, lo:hi], preferred_element_type=F32)

    cos = cos_ref[...]
    sa = sa_ref[...]
    sb = sb_ref[...]

    def head_norm_rope(z, gain):
        ss = jnp.dot((z * z).astype(BF16), bd_ref[...], preferred_element_type=F32)
        zn = z * lax.rsqrt(ss * (1.0 / HEAD_DIM) + EPS) * gain
        return (zn * cos + pltpu.roll(zn, LANES - ROPE_QUARTER, 1) * sa
                + pltpu.roll(zn, ROPE_QUARTER, 1) * sb)

    zq = proj(0, Q_WIDTH)
    zkv = proj(_OFF_K, _OFF_U)

    qc, qa, qb = qcos_ref[...], qsa_ref[...], qsb_ref[...]
    for j in range(Q_WIDTH // LANES):
        zt = zq[:, j * LANES:(j + 1) * LANES].T
        for hh in range(LANES // HEAD_DIM):
            z = zt[hh * HEAD_DIM:(hh + 1) * HEAD_DIM]
            ss = jnp.sum(z * z, axis=0, keepdims=True)
            zn = z * lax.rsqrt(ss * (1.0 / HEAD_DIM) + EPS)
            up = jnp.concatenate([zn[ROPE_QUARTER:], zn[:ROPE_QUARTER]], axis=0)
            dn = jnp.concatenate([zn[-ROPE_QUARTER:], zn[:-ROPE_QUARTER]], axis=0)
            q_ref[j, hh * HEAD_DIM:(hh + 1) * HEAD_DIM, :] = (
                zn * qc + up * qa + dn * qb).astype(BF16)

    zu = proj(_OFF_U, _OFF_VS)

    kr = head_norm_rope(zkv[:, :KV_WIDTH], kg_ref[...])
    first = _lane_lt64(kr.shape)
    k_ref[0] = jnp.where(first, kr, 0.0).astype(BF16)
    k_ref[1] = jnp.where(first, 0.0, kr).astype(BF16)

    vt = zkv[:, KV_WIDTH:].T
    ones = jnp.ones((VT_ROWS - HEAD_DIM, tk), BF16)
    for g in range(N_KV_HEADS):
        for r in range(rows // tk):
            vt_ref[g, r, :HEAD_DIM, :] = vt[g * HEAD_DIM:(g + 1) * HEAD_DIM,
                                            r * tk:(r + 1) * tk].astype(BF16)
            vt_ref[g, r, HEAD_DIM:, :] = ones

    zvs = proj(_OFF_VS, _OFF_GA)
    u = _gelu(zu)
    zga = proj(_OFF_GA, _OFF_GB)

    vs = _rms(_gelu(zvs), sgg_ref[...]).astype(BF16)
    zgb = proj(_OFF_GB, IN_WIDTH)
    sga_ref[...] = jax.nn.sigmoid(zga).astype(BF16)
    pair = 2 * CHUNK
    lt64 = _lane_lt64((CHUNK, 2 * LANES))
    sgu_rows = []
    for c in range(rows // pair):
        r0 = c * pair
        slabs0, slabs1 = [], []
        for s in range(SG_WIDTH // LANES):
            cols = slice(s * LANES, (s + 1) * LANES)
            rhs = jnp.concatenate([vs[r0:r0 + CHUNK, cols], vs[r0 + CHUNK:r0 + pair, cols]], axis=1)
            oa = jnp.dot(sgw_ref[2 * s], rhs, preferred_element_type=F32)
            ob = jnp.dot(sgw_ref[2 * s + 1], rhs, preferred_element_type=F32)
            bias = sgb_ref[:, cols]
            sp = jnp.where(lt64, oa, ob) + jnp.concatenate([bias, bias], axis=1)
            slabs0.append(u[r0:r0 + CHUNK, cols] * sp[:, :LANES])
            slabs1.append(u[r0 + CHUNK:r0 + pair, cols] * sp[:, LANES:])
        sgu_rows.append(jnp.concatenate(slabs0, axis=1))
        sgu_rows.append(jnp.concatenate(slabs1, axis=1))
    sgu = jnp.concatenate(sgu_rows, axis=0).astype(BF16)
    sgb_gate = jax.nn.sigmoid(zgb)
    yb = jnp.dot(sgu, wb_ref[...], preferred_element_type=F32)
    ybg_ref[...] = (sgb_gate * yb).astype(BF16)


def _in_proj(x, layer, p, seq_len):
    n = x.shape[0]
    tok, _, tk = _tiles(seq_len)
    n_pos = seq_len // tok
    const2 = lambda i: (0, 0)
    layer3 = lambda i: (layer, 0, 0)
    blocks = (tok * D_MODEL * 4 + D_MODEL * IN_WIDTH * 2 + SG_WIDTH * D_MODEL * 2
              + tok * (Q_WIDTH + 2 * LANES + 2 * VT_ROWS + 2 * D_MODEL) * 2)
    return pl.pallas_call(
        functools.partial(_in_proj_kernel, tk=tk),
        grid=(n // tok,),
        in_specs=[
            pl.BlockSpec((tok, D_MODEL), lambda i: (i, 0)),
            pl.BlockSpec((None, 1, D_MODEL), layer3),
            pl.BlockSpec((None, D_MODEL, IN_WIDTH), layer3),
            pl.BlockSpec((None, HEAD_DIM, tok), lambda i: (layer, 0, i % n_pos)),
            pl.BlockSpec((None, HEAD_DIM, tok), lambda i: (layer, 0, i % n_pos)),
            pl.BlockSpec((None, HEAD_DIM, tok), lambda i: (layer, 0, i % n_pos)),
            pl.BlockSpec((None, 1, LANES), layer3),
            pl.BlockSpec((tok, LANES), lambda i: (i % n_pos, 0)),
            pl.BlockSpec((tok, LANES), lambda i: (i % n_pos, 0)),
            pl.BlockSpec((tok, LANES), lambda i: (i % n_pos, 0)),
            pl.BlockSpec((LANES, LANES), const2),
            pl.BlockSpec((None, 1, SG_WIDTH), layer3),
            pl.BlockSpec((None, SG_GROUPS, CHUNK, CHUNK), lambda i: (layer, 0, 0, 0)),
            pl.BlockSpec((None, CHUNK, SG_WIDTH), layer3),
            pl.BlockSpec((None, SG_WIDTH, D_MODEL), layer3),
        ],
        out_specs=[
            pl.BlockSpec((Q_WIDTH // LANES, LANES, tok), lambda i: (0, 0, i)),
            pl.BlockSpec((N_KV_HEADS, tok, LANES), lambda i: (0, i, 0)),
            pl.BlockSpec((N_KV_HEADS, tok // tk, VT_ROWS, tk), lambda i: (0, i, 0, 0)),
            pl.BlockSpec((tok, D_MODEL), lambda i: (i, 0)),
            pl.BlockSpec((tok, D_MODEL), lambda i: (i, 0)),
        ],
        out_shape=[
            jax.ShapeDtypeStruct((Q_WIDTH // LANES, LANES, n), BF16),
            jax.ShapeDtypeStruct((N_KV_HEADS, n, LANES), BF16),
            jax.ShapeDtypeStruct((N_KV_HEADS, n // tk, VT_ROWS, tk), BF16),
            jax.ShapeDtypeStruct((n, D_MODEL), BF16),
            jax.ShapeDtypeStruct((n, D_MODEL), BF16),
        ],
        compiler_params=pltpu.CompilerParams(
            dimension_semantics=("arbitrary",), vmem_limit_bytes=_vmem_limit(blocks)),
        name="in_proj",
    )(x, p["attn_norm_g"], p["w_in"], p["q_cos"], p["q_sa"], p["q_sb"], p["k_norm_g"],
      p["rope_cos"], p["rope_sa"], p["rope_sb"], p["head_ones"],
      p["sg_norm_g"], p["sg_w"], p["sg_bias"], p["w_branch_b"])


def _attn_mix_kernel(q_ref, qn_ref, k_ref, vt_ref, sga_ref, ybg_ref, x_ref, wa_ref, wm_ref,
                     o_ref, acc_ref, m_ref, s0_ref, s1_ref, bm0_ref, bm1_ref, *, tk):
    tq = q_ref.shape[2]
    n_kv = k_ref.shape[1] // tk
    acc_ref[...] = jnp.zeros(acc_ref.shape, F32)
    m_ref[...] = jnp.full(m_ref.shape, NEG_BIG, F32)

    def scores(blk, s_ref, bm_ref, h, qt_ref):
        g, j = divmod(h, Q_PER_KV)
        start = pl.multiple_of(blk * tk, tk)
        kb = k_ref[g, pl.ds(start, tk), :]
        s = jnp.dot(kb, qt_ref[j], preferred_element_type=F32)
        s_ref[h] = s
        bm_ref[h:h + 1, :] = jnp.max(s, axis=0, keepdims=True)

    def accumulate(blk, s_ref, bm_ref, h):
        g = h // Q_PER_KV
        vb = vt_ref[g, blk]
        m_old = m_ref[h:h + 1, :]
        m_new = jnp.maximum(m_old, bm_ref[h:h + 1, :])
        alpha = jnp.exp2(m_old - m_new)
        pt = jnp.exp2(s_ref[h] - m_new).astype(BF16)
        pv = jnp.dot(vb, pt, preferred_element_type=F32)
        acc_ref[h] = alpha * acc_ref[h] + pv
        m_ref[h:h + 1, :] = m_new

    def step(score_blk, score_refs, acc_blk, acc_refs, qt_ref=q_ref):
        for h in range(N_Q_HEADS):
            if score_blk is not None:
                scores(score_blk, *score_refs, h, qt_ref)
            if acc_blk is not None:
                accumulate(acc_blk, *acc_refs, h)

    even = (s0_ref, bm0_ref)
    odd = (s1_ref, bm1_ref)

    @pl.when(pl.program_id(1) == 0)
    def _():
        step(0, even, None, None)

    def kv_pair(t, carry):
        step(2 * t + 1, odd, 2 * t, even)
        step(2 * t + 2, even, 2 * t + 1, odd)
        return carry

    lax.fori_loop(0, n_kv // 2 - 1, kv_pair, 0)
    step(n_kv - 1, odd, n_kv - 2, even)
    step(0, even, n_kv - 1, odd, qn_ref)

    heads = []
    for h in range(N_Q_HEADS):
        a = acc_ref[h]
        heads.append(a[:HEAD_DIM] * (1.0 / a[HEAD_DIM:HEAD_DIM + 1]))
    ot = jnp.concatenate(heads, axis=0)
    ya = jnp.dot(ot.T.astype(BF16), wa_ref[...], preferred_element_type=F32)
    mix = sga_ref[...].astype(F32) * ya + ybg_ref[...].astype(F32)
    o_ref[...] = x_ref[...] + jnp.dot(mix.astype(BF16), wm_ref[...],
                                      preferred_element_type=F32)


def _attn_mix(x, q, kp, vt, sga, ybg, layer, p, batch, seq_len):
    _, tq, tk = _tiles(seq_len)
    n_kv = seq_len // tk
    n_q = seq_len // tq
    row = lambda b, i: (b * n_q + i, 0)
    blocks = (tq * (2 * Q_WIDTH + 2 * D_MODEL) * 2 + 2 * tq * D_MODEL * 4
              + N_KV_HEADS * seq_len * (LANES + VT_ROWS) * 2
              + (Q_WIDTH + D_MODEL) * D_MODEL * 2)
    scratch = N_Q_HEADS * tq * 4 * (VT_ROWS + 2 * tk + 3)
    return pl.pallas_call(
        functools.partial(_attn_mix_kernel, tk=tk),
        grid=(batch, n_q),
        in_specs=[
            pl.BlockSpec((Q_WIDTH // LANES, LANES, tq), lambda b, i: (0, 0, b * n_q + i)),
            pl.BlockSpec((Q_WIDTH // LANES, LANES, tq),
                         lambda b, i: (0, 0, b * n_q + jnp.minimum(i + 1, n_q - 1))),
            pl.BlockSpec((N_KV_HEADS, seq_len, LANES), lambda b, i: (0, b, 0)),
            pl.BlockSpec((N_KV_HEADS, n_kv, VT_ROWS, tk), lambda b, i: (0, b, 0, 0)),
            pl.BlockSpec((tq, D_MODEL), row),
            pl.BlockSpec((tq, D_MODEL), row),
            pl.BlockSpec((tq, D_MODEL), row),
            pl.BlockSpec((None, Q_WIDTH, D_MODEL), lambda b, i: (layer, 0, 0)),
            pl.BlockSpec((None, D_MODEL, D_MODEL), lambda b, i: (layer, 0, 0)),
        ],
        out_specs=pl.BlockSpec((tq, D_MODEL), row),
        out_shape=jax.ShapeDtypeStruct(x.shape, F32),
        scratch_shapes=[
            pltpu.VMEM((N_Q_HEADS, VT_ROWS, tq), F32),
            pltpu.VMEM((N_Q_HEADS, tq), F32),
            pltpu.VMEM((N_Q_HEADS, tk, tq), F32),
            pltpu.VMEM((N_Q_HEADS, tk, tq), F32),
            pltpu.VMEM((N_Q_HEADS, tq), F32),
            pltpu.VMEM((N_Q_HEADS, tq), F32),
        ],
        compiler_params=pltpu.CompilerParams(
            dimension_semantics=("arbitrary", "arbitrary"),
            vmem_limit_bytes=_vmem_limit(blocks, scratch)),
        name="attn_mix",
    )(q, q, kp, vt, sga, ybg, x, p["w_branch_a"], p["w_mix_out"])


FFN_COLS = 512


def _conv_ffn_kernel(xp_ref, x_ref, xn_ref, ng_ref, wu_ref, cw_ref, cb_ref, wd_ref, fg_ref,
                     o_ref, a_ref, *, final):
    i = pl.program_id(1)
    rows = x_ref.shape[0]
    x = x_ref[...]
    xp = jnp.where(i > 0, xp_ref[...], 0.0)
    xn = jnp.where(i < pl.num_programs(1) - 1, xn_ref[...], 0.0)
    hb = _rms(jnp.concatenate([xp, x, xn], axis=0), ng_ref[...]).astype(BF16)

    n_slab = FFN_COLS // LANES

    def up(lo):
        a = jnp.dot(hb, wu_ref[:, lo:lo + FFN_COLS], preferred_element_type=F32)
        for j in range(n_slab):
            a_ref[lo // LANES + j] = a[:, j * LANES:(j + 1) * LANES]

    def conv(lo):
        outs = []
        for j in range(n_slab):
            slab = lo // LANES + j
            cols = slice(lo + j * LANES, lo + (j + 1) * LANES)
            w = cw_ref[:, cols]
            outs.append(a_ref[slab, HALO - 1:HALO - 1 + rows, :] * w[0:1]
                        + a_ref[slab, HALO:HALO + rows, :] * w[1:2]
                        + a_ref[slab, HALO + 1:HALO + 1 + rows, :] * w[2:3] + cb_ref[:, cols])
        return jnp.concatenate(outs, axis=1)

    n_chunk = D_FF // FFN_COLS
    up(0)
    up(D_FF)
    y = x
    for c in range(n_chunk):
        if c + 1 < n_chunk:
            up((c + 1) * FFN_COLS)
            up(D_FF + (c + 1) * FFN_COLS)
        gate = _gelu(conv(c * FFN_COLS)) * conv(D_FF + c * FFN_COLS)
        y = y + jnp.dot(gate.astype(BF16), wd_ref[c * FFN_COLS:(c + 1) * FFN_COLS, :],
                        preferred_element_type=F32)
    o_ref[...] = _rms(y, fg_ref[...]) if final else y


def _conv_ffn(x, layer, p, batch, seq_len, final):
    tok, _, _ = _tiles(seq_len)
    n_t = seq_len // tok
    per = tok // HALO
    n_halo = seq_len // HALO
    row = lambda b, i: (b * n_t + i, 0)
    prev = lambda b, i: (b * n_halo + jnp.maximum(i * per - 1, 0), 0)
    nxt = lambda b, i: (b * n_halo + jnp.minimum((i + 1) * per, n_halo - 1), 0)
    layer3 = lambda b, i: (layer, 0, 0)
    blocks = 2 * tok * D_MODEL * 4 + 3 * D_MODEL * D_FF * 2
    return pl.pallas_call(
        functools.partial(_conv_ffn_kernel, final=final),
        grid=(batch, n_t),
        in_specs=[
            pl.BlockSpec((HALO, D_MODEL), prev),
            pl.BlockSpec((tok, D_MODEL), row),
            pl.BlockSpec((HALO, D_MODEL), nxt),
            pl.BlockSpec((None, 1, D_MODEL), layer3),
            pl.BlockSpec((None, D_MODEL, 2 * D_FF), layer3),
            pl.BlockSpec((None, CONV_W, 2 * D_FF), layer3),
            pl.BlockSpec((None, 1, 2 * D_FF), layer3),
            pl.BlockSpec((None, D_FF, D_MODEL), layer3),
            pl.BlockSpec((1, D_MODEL), lambda b, i: (0, 0)),
        ],
        out_specs=pl.BlockSpec((tok, D_MODEL), row),
        out_shape=jax.ShapeDtypeStruct(x.shape, F32),
        scratch_shapes=[pltpu.VMEM((2 * D_FF // LANES, tok + 2 * HALO, LANES), F32)],
        compiler_params=pltpu.CompilerParams(
            dimension_semantics=("arbitrary", "arbitrary"),
            vmem_limit_bytes=_vmem_limit(blocks, 2 * D_FF * (tok + 2 * HALO) * 4)),
        name="conv_ffn",
    )(x, x, x, p["ffn_norm_g"], p["w_up"], p["conv_w"], p["conv_b"], p["w_down"],
      p["final_norm_g"])


def _rope_tables(seq_len):
    t = jnp.arange(seq_len)
    half = HEAD_DIM // 2
    freq = ROPE_THETA ** (-jnp.arange(0, half, 2, dtype=F32) / half)
    ang_r = (t // GRID_W).astype(F32)[:, None] * freq[None, :]
    ang_c = (t % GRID_W).astype(F32)[:, None] * freq[None, :]
    cr, sr, cc, sc = jnp.cos(ang_r), jnp.sin(ang_r), jnp.cos(ang_c), jnp.sin(ang_c)
    zero = jnp.zeros_like(sr)
    per_head = lambda parts: jnp.tile(jnp.concatenate(parts, axis=1), (1, LANES // HEAD_DIM))
    cos = per_head([cr, cr, cc, cc])
    sa = per_head([-sr, zero, -sc, zero])
    sb = per_head([zero, sr, zero, sc])
    return cos, sa, sb


def _prepare_params(attn_norm_g, w_in, q_norm_g, k_norm_g, sg_norm_g, sg_w, sg_b,
                    w_branch_a, w_branch_b, w_mix_out, ffn_norm_g, w_up, conv_w, conv_b,
                    w_down, final_norm_g):
    depth = w_in.shape[0]
    order = np.concatenate([
        np.arange(h * HEAD_DIM, (h + 1) * HEAD_DIM)
        for j in range(Q_PER_KV) for h in (j, Q_PER_KV + j)])
    w_in_p = jnp.concatenate([w_in[:, :, order], w_in[:, :, Q_WIDTH:]], axis=2).astype(BF16)
    two_heads = lambda g: jnp.tile(g, (1, LANES // HEAD_DIM)).reshape(depth, 1, LANES)
    blk = np.arange(LANES) // HEAD_DIM
    return dict(
        attn_norm_g=attn_norm_g.reshape(depth, 1, D_MODEL),
        w_in=w_in_p,
        q_norm_g=q_norm_g,
        k_norm_g=two_heads(k_norm_g),
        head_ones=jnp.asarray(blk[:, None] == blk[None, :], BF16),
        sg_norm_g=sg_norm_g.reshape(depth, 1, SG_WIDTH),
        sg_w=sg_w.astype(BF16),
        sg_bias=jnp.repeat(jnp.swapaxes(sg_b, 1, 2), SG_GROUP_DIM, axis=2),
        w_branch_a=w_branch_a.astype(BF16),
        w_branch_b=w_branch_b.astype(BF16),
        w_mix_out=w_mix_out.astype(BF16),
        ffn_norm_g=ffn_norm_g.reshape(depth, 1, D_MODEL),
        w_up=w_up.astype(BF16),
        conv_w=conv_w,
        conv_b=conv_b.reshape(depth, 1, 2 * D_FF),
        w_down=w_down.astype(BF16),
        final_norm_g=final_norm_g.reshape(1, D_MODEL),
    )


def _trunk(x3, p):
    batch, seq_len, _ = x3.shape
    depth = p["w_in"].shape[0]
    tok, tq, tk = _tiles(seq_len)
    assert seq_len % tok == 0 and seq_len % tq == 0 and seq_len % (2 * tk) == 0
    assert tok % (2 * CHUNK) == 0 and tok % tk == 0 and tok % HALO == 0
    p = dict(p)
    p["rope_cos"], p["rope_sa"], p["rope_sb"] = _rope_tables(seq_len)
    g = p["q_norm_g"][:, :, None] * Q_SCALE
    head = lambda t: t[:, :HEAD_DIM].T[None]
    p["q_cos"] = head(p["rope_cos"]) * g
    p["q_sa"] = head(p["rope_sa"]) * jnp.roll(g, -ROPE_QUARTER, axis=1)
    p["q_sb"] = head(p["rope_sb"]) * jnp.roll(g, ROPE_QUARTER, axis=1)
    x = x3.reshape(batch * seq_len, D_MODEL)
    for layer in range(depth):
        q, kp, vt, sga, ybg = _in_proj(x, layer, p, seq_len)
        x = _attn_mix(x, q, kp, vt, sga, ybg, layer, p, batch, seq_len)
        x = _conv_ffn(x, layer, p, batch, seq_len, final=(layer == depth - 1))
    return x.reshape(batch, seq_len, D_MODEL)


def kernel(x_prompt, x_sample, attn_norm_g, w_in, q_norm_g, k_norm_g, sg_norm_g, sg_w, sg_b,
           w_branch_a, w_branch_b, w_mix_out, ffn_norm_g, w_up, conv_w, conv_b, w_down,
           final_norm_g):
    p = _prepare_params(attn_norm_g, w_in, q_norm_g, k_norm_g, sg_norm_g, sg_w, sg_b,
                        w_branch_a, w_branch_b, w_mix_out, ffn_norm_g, w_up, conv_w, conv_b,
                        w_down, final_norm_g)
    return (_trunk(x_prompt, p), _trunk(x_sample, p))
```

```python
import functools
import math

import jax
import jax.numpy as jnp
import numpy as np
from jax import lax
from jax.experimental import pallas as pl
from jax.experimental.pallas import tpu as pltpu

D_MODEL = 1024
GRID_W = 64
HEAD_DIM = 64
N_Q_HEADS = 8
N_KV_HEADS = 2
Q_PER_KV = N_Q_HEADS // N_KV_HEADS
Q_WIDTH = N_Q_HEADS * HEAD_DIM
KV_WIDTH = N_KV_HEADS * HEAD_DIM
SG_GROUPS = 8
SG_GROUP_DIM = 64
SG_WIDTH = SG_GROUPS * SG_GROUP_DIM
CHUNK = 128
D_FF = 2048
CONV_W = 3
ROPE_THETA = 10000.0
EPS = 1e-6
IN_WIDTH = Q_WIDTH + 2 * KV_WIDTH + 2 * SG_WIDTH + 2 * D_MODEL

LANES = 128
F32_SUBLANES = 8
BF16_SUBLANES = 16
V7X_VMEM_BYTES = 64 * 1024 * 1024

_OFF_K = Q_WIDTH
_OFF_V = _OFF_K + KV_WIDTH
_OFF_U = _OFF_V + KV_WIDTH
_OFF_VS = _OFF_U + SG_WIDTH
_OFF_GA = _OFF_VS + SG_WIDTH
_OFF_GB = _OFF_GA + D_MODEL

VT_ROWS = HEAD_DIM + BF16_SUBLANES
ROPE_QUARTER = HEAD_DIM // 4
NEG_BIG = -1e30
Q_SCALE = math.log2(math.e) / math.sqrt(HEAD_DIM)
HALO = F32_SUBLANES
EXP2_SAFE_BOUND = 60.0

F32 = jnp.float32
BF16 = jnp.bfloat16


def _tiles(seq_len):
    tok = min(512, seq_len)
    tq = min(512, seq_len)
    tk = min(256, seq_len // 2)
    return tok, tq, tk


VMEM_RESERVE_BYTES = 4 << 20
VMEM_TEMP_BYTES = 24 << 20


def _vmem_limit(block_bytes, scratch_bytes=0):
    want = 2 * block_bytes + scratch_bytes + VMEM_TEMP_BYTES
    return int(min(V7X_VMEM_BYTES - VMEM_RESERVE_BYTES, want))


def _rms(x, g):
    return x * lax.rsqrt(jnp.mean(x * x, axis=-1, keepdims=True) + EPS) * g


def _gelu(x):
    c = math.sqrt(2.0 / math.pi)
    half = 0.5 * x
    return half + half * jnp.tanh(x * (c + (c * 0.044715) * (x * x)))


def _lane_lt64(shape):
    lane = lax.broadcasted_iota(jnp.int32, shape, len(shape) - 1)
    return (lane % LANES) < HEAD_DIM


def _in_proj_kernel(x_ref, ng_ref, w_ref, qcos_ref, qsa_ref, qsb_ref, kg_ref, cos_ref, sa_ref,
                    sb_ref, bd_ref, sgg_ref, sgw_ref, sgb_ref, wb_ref,
                    q_ref, k_ref, vt_ref, sga_ref, ybg_ref, *, tk):
    rows = x_ref.shape[0]
    hb = _rms(x_ref[...], ng_ref[...]).astype(BF16)

    def proj(lo, hi):
        return jnp.dot(hb, w_ref[:, lo:hi], preferred_element_type=F32)

    cos = cos_ref[...]
    sa = sa_ref[...]
    sb = sb_ref[...]

    def head_norm_rope(z, gain):
        ss = jnp.dot((z * z).astype(BF16), bd_ref[...], preferred_element_type=F32)
        zn = z * lax.rsqrt(ss * (1.0 / HEAD_DIM) + EPS) * gain
        return (zn * cos + pltpu.roll(zn, LANES - ROPE_QUARTER, 1) * sa
                + pltpu.roll(zn, ROPE_QUARTER, 1) * sb)

    zq = proj(0, Q_WIDTH)
    zkv = proj(_OFF_K, _OFF_U)

    qc, qa, qb = qcos_ref[...], qsa_ref[...], qsb_ref[...]
    for j in range(Q_WIDTH // LANES):
        zt = zq[:, j * LANES:(j + 1) * LANES].T
        for hh in range(LANES // HEAD_DIM):
            z = zt[hh * HEAD_DIM:(hh + 1) * HEAD_DIM]
            ss = jnp.sum(z * z, axis=0, keepdims=True)
            zn = z * lax.rsqrt(ss * (1.0 / HEAD_DIM) + EPS)
            up = jnp.concatenate([zn[ROPE_QUARTER:], zn[:ROPE_QUARTER]], axis=0)
            dn = jnp.concatenate([zn[-ROPE_QUARTER:], zn[:-ROPE_QUARTER]], axis=0)
            q_ref[j, hh * HEAD_DIM:(hh + 1) * HEAD_DIM, :] = (
                zn * qc + up * qa + dn * qb).astype(BF16)

    zu = proj(_OFF_U, _OFF_VS)

    kr = head_norm_rope(zkv[:, :KV_WIDTH], kg_ref[...])
    first = _lane_lt64(kr.shape)
    k_ref[0] = jnp.where(first, kr, 0.0).astype(BF16)
    k_ref[1] = jnp.where(first, 0.0, kr).astype(BF16)

    vt = zkv[:, KV_WIDTH:].T
    ones = jnp.ones((VT_ROWS - HEAD_DIM, tk), BF16)
    for g in range(N_KV_HEADS):
        for r in range(rows // tk):
            vt_ref[g, r, :HEAD_DIM, :] = vt[g * HEAD_DIM:(g + 1) * HEAD_DIM,
                                            r * tk:(r + 1) * tk].astype(BF16)
            vt_ref[g, r, HEAD_DIM:, :] = ones

    zvs = proj(_OFF_VS, _OFF_GA)
    u = _gelu(zu)
    zga = proj(_OFF_GA, _OFF_GB)

    vs = _rms(_gelu(zvs), sgg_ref[...]).astype(BF16)
    zgb = proj(_OFF_GB, IN_WIDTH)
    sga_ref[...] = jax.nn.sigmoid(zga).astype(BF16)
    pair = 2 * CHUNK
    lt64 = _lane_lt64((CHUNK, 2 * LANES))
    sgu_rows = []
    for c in range(rows // pair):
        r0 = c * pair
        slabs0, slabs1 = [], []
        for s in range(SG_WIDTH // LANES):
            cols = slice(s * LANES, (s + 1) * LANES)
            rhs = jnp.concatenate([vs[r0:r0 + CHUNK, cols], vs[r0 + CHUNK:r0 + pair, cols]], axis=1)
            oa = jnp.dot(sgw_ref[2 * s], rhs, preferred_element_type=F32)
            ob = jnp.dot(sgw_ref[2 * s + 1], rhs, preferred_element_type=F32)
            bias = sgb_ref[:, cols]
            sp = jnp.where(lt64, oa, ob) + jnp.concatenate([bias, bias], axis=1)
            slabs0.append(u[r0:r0 + CHUNK, cols] * sp[:, :LANES])
            slabs1.append(u[r0 + CHUNK:r0 + pair, cols] * sp[:, LANES:])
        sgu_rows.append(jnp.concatenate(slabs0, axis=1))
        sgu_rows.append(jnp.concatenate(slabs1, axis=1))
    sgu = jnp.concatenate(sgu_rows, axis=0).astype(BF16)
    sgb_gate = jax.nn.sigmoid(zgb)
    yb = jnp.dot(sgu, wb_ref[...], preferred_element_type=F32)
    ybg_ref[...] = (sgb_gate * yb).astype(BF16)


def _in_proj(x, layer, p, seq_len):
    n = x.shape[0]
    tok, _, tk = _tiles(seq_len)
    n_pos = seq_len // tok
    const2 = lambda i: (0, 0)
    layer3 = lambda i: (layer, 0, 0)
    blocks = (tok * D_MODEL * 4 + D_MODEL * IN_WIDTH * 2 + SG_WIDTH * D_MODEL * 2
              + tok * (Q_WIDTH + 2 * LANES + 2 * VT_ROWS + 2 * D_MODEL) * 2)
    return pl.pallas_call(
        functools.partial(_in_proj_kernel, tk=tk),
        grid=(n // tok,),
        in_specs=[
            pl.BlockSpec((tok, D_MODEL), lambda i: (i, 0)),
            pl.BlockSpec((None, 1, D_MODEL), layer3),
            pl.BlockSpec((None, D_MODEL, IN_WIDTH), layer3),
            pl.BlockSpec((None, HEAD_DIM, tok), lambda i: (layer, 0, i % n_pos)),
            pl.BlockSpec((None, HEAD_DIM, tok), lambda i: (layer, 0, i % n_pos)),
            pl.BlockSpec((None, HEAD_DIM, tok), lambda i: (layer, 0, i % n_pos)),
            pl.BlockSpec((None, 1, LANES), layer3),
            pl.BlockSpec((tok, LANES), lambda i: (i % n_pos, 0)),
            pl.BlockSpec((tok, LANES), lambda i: (i % n_pos, 0)),
            pl.BlockSpec((tok, LANES), lambda i: (i % n_pos, 0)),
            pl.BlockSpec((LANES, LANES), const2),
            pl.BlockSpec((None, 1, SG_WIDTH), layer3),
            pl.BlockSpec((None, SG_GROUPS, CHUNK, CHUNK), lambda i: (layer, 0, 0, 0)),
            pl.BlockSpec((None, CHUNK, SG_WIDTH), layer3),
            pl.BlockSpec((None, SG_WIDTH, D_MODEL), layer3),
        ],
        out_specs=[
            pl.BlockSpec((Q_WIDTH // LANES, LANES, tok), lambda i: (0, 0, i)),
            pl.BlockSpec((N_KV_HEADS, tok, LANES), lambda i: (0, i, 0)),
            pl.BlockSpec((N_KV_HEADS, tok // tk, VT_ROWS, tk), lambda i: (0, i, 0, 0)),
            pl.BlockSpec((tok, D_MODEL), lambda i: (i, 0)),
            pl.BlockSpec((tok, D_MODEL), lambda i: (i, 0)),
        ],
        out_shape=[
            jax.ShapeDtypeStruct((Q_WIDTH // LANES, LANES, n), BF16),
            jax.ShapeDtypeStruct((N_KV_HEADS, n, LANES), BF16),
            jax.ShapeDtypeStruct((N_KV_HEADS, n // tk, VT_ROWS, tk), BF16),
            jax.ShapeDtypeStruct((n, D_MODEL), BF16),
            jax.ShapeDtypeStruct((n, D_MODEL), BF16),
        ],
        compiler_params=pltpu.CompilerParams(
            dimension_semantics=("arbitrary",), vmem_limit_bytes=_vmem_limit(blocks)),
        name="in_proj",
    )(x, p["attn_norm_g"], p["w_in"], p["q_cos"], p["q_sa"], p["q_sb"], p["k_norm_g"],
      p["rope_cos"], p["rope_sa"], p["rope_sb"], p["head_ones"],
      p["sg_norm_g"], p["sg_w"], p["sg_bias"], p["w_branch_b"])


def _attn_mix_kernel(q_ref, qn_ref, k_ref, vt_ref, sga_ref, ybg_ref, x_ref, wa_ref, wm_ref,
                     o_ref, acc_ref, m_ref, s0_ref, s1_ref, bm0_ref, bm1_ref, *, tk):
    tq = q_ref.shape[2]
    n_kv = k_ref.shape[1] // tk
    acc_ref[...] = jnp.zeros(acc_ref.shape, F32)
    m_ref[...] = jnp.full(m_ref.shape, NEG_BIG, F32)

    def scores(blk, s_ref, bm_ref, h, qt_ref):
        g, j = divmod(h, Q_PER_KV)
        start = pl.multiple_of(blk * tk, tk)
        kb = k_ref[g, pl.ds(start, tk), :]
        s = jnp.dot(kb, qt_ref[j], preferred_element_type=F32)
        s_ref[h] = s
        bm_ref[h:h + 1, :] = jnp.max(s, axis=0, keepdims=True)

    def accumulate(blk, s_ref, bm_ref, h):
        g = h // Q_PER_KV
        vb = vt_ref[g, blk]
        m_old = m_ref[h:h + 1, :]
        m_new = jnp.maximum(m_old, bm_ref[h:h + 1, :])
        alpha = jnp.exp2(m_old - m_new)
        pt = jnp.exp2(s_ref[h] - m_new).astype(BF16)
        pv = jnp.dot(vb, pt, preferred_element_type=F32)
        acc_ref[h] = alpha * acc_ref[h] + pv
        m_ref[h:h + 1, :] = m_new

    def step(score_blk, score_refs, acc_blk, acc_refs, qt_ref=q_ref):
        for h in range(N_Q_HEADS):
            if score_blk is not None:
                scores(score_blk, *score_refs, h, qt_ref)
            if acc_blk is not None:
                accumulate(acc_blk, *acc_refs, h)

    even = (s0_ref, bm0_ref)
    odd = (s1_ref, bm1_ref)

    @pl.when(pl.program_id(1) == 0)
    def _():
        step(0, even, None, None)

    def kv_pair(t, carry):
        step(2 * t + 1, odd, 2 * t, even)
        step(2 * t + 2, even, 2 * t + 1, odd)
        return carry

    lax.fori_loop(0, n_kv // 2 - 1, kv_pair, 0)
    step(n_kv - 1, odd, n_kv - 2, even)
    step(0, even, n_kv - 1, odd, qn_ref)

    _attn_epilogue(acc_ref, sga_ref, ybg_ref, x_ref, wa_ref, wm_ref, o_ref)


def _attn_epilogue(acc_ref, sga_ref, ybg_ref, x_ref, wa_ref, wm_ref, o_ref):
    heads = []
    for h in range(N_Q_HEADS):
        a = acc_ref[h]
        heads.append(a[:HEAD_DIM] * (1.0 / a[HEAD_DIM:HEAD_DIM + 1]))
    ot = jnp.concatenate(heads, axis=0)
    ya = jnp.dot(ot.T.astype(BF16), wa_ref[...], preferred_element_type=F32)
    mix = sga_ref[...].astype(F32) * ya + ybg_ref[...].astype(F32)
    o_ref[...] = x_ref[...] + jnp.dot(mix.astype(BF16), wm_ref[...],
                                      preferred_element_type=F32)


SCORE_AHEAD = 2


def _attn_mix_bounded_kernel(q_ref, k_ref, vt_ref, sga_ref, ybg_ref, x_ref, wa_ref, wm_ref,
                             o_ref, acc_ref, *, tk):
    n_kv = k_ref.shape[1] // tk
    acc_ref[...] = jnp.zeros(acc_ref.shape, F32)

    def qk(blk, h):
        g, j = divmod(h, Q_PER_KV)
        start = pl.multiple_of(blk * tk, tk)
        kb = k_ref[g, pl.ds(start, tk), :]
        return jnp.dot(kb, q_ref[j], preferred_element_type=F32)

    def pv(blk, h, s):
        vb = vt_ref[h // Q_PER_KV, blk]
        pt = jnp.exp2(s).astype(BF16)
        acc_ref[h] = acc_ref[h] + jnp.dot(vb, pt, preferred_element_type=F32)

    def kv_pair(t, carry):
        items = [(2 * t + d, h) for d in range(2) for h in range(N_Q_HEADS)]
        pending = [qk(*items[n]) for n in range(SCORE_AHEAD)]
        for n in range(len(items)):
            if n + SCORE_AHEAD < len(items):
                pending.append(qk(*items[n + SCORE_AHEAD]))
            pv(*items[n], pending.pop(0))
        return carry

    lax.fori_loop(0, n_kv // 2, kv_pair, 0)
    _attn_epilogue(acc_ref, sga_ref, ybg_ref, x_ref, wa_ref, wm_ref, o_ref)


def _attn_mix(x, q, kp, vt, sga, ybg, layer, p, batch, seq_len, bounded):
    _, tq, tk = _tiles(seq_len)
    n_kv = seq_len // tk
    n_q = seq_len // tq
    row = lambda b, i: (b * n_q + i, 0)
    q_spec = pl.BlockSpec((Q_WIDTH // LANES, LANES, tq), lambda b, i: (0, 0, b * n_q + i))
    q_next = pl.BlockSpec((Q_WIDTH // LANES, LANES, tq),
                          lambda b, i: (0, 0, b * n_q + jnp.minimum(i + 1, n_q - 1)))
    common = [
        pl.BlockSpec((N_KV_HEADS, seq_len, LANES), lambda b, i: (0, b, 0)),
        pl.BlockSpec((N_KV_HEADS, n_kv, VT_ROWS, tk), lambda b, i: (0, b, 0, 0)),
        pl.BlockSpec((tq, D_MODEL), row),
        pl.BlockSpec((tq, D_MODEL), row),
        pl.BlockSpec((tq, D_MODEL), row),
        pl.BlockSpec((None, Q_WIDTH, D_MODEL), lambda b, i: (layer, 0, 0)),
        pl.BlockSpec((None, D_MODEL, D_MODEL), lambda b, i: (layer, 0, 0)),
    ]
    blocks = (tq * (2 * Q_WIDTH + 2 * D_MODEL) * 2 + 2 * tq * D_MODEL * 4
              + N_KV_HEADS * seq_len * (LANES + VT_ROWS) * 2
              + (Q_WIDTH + D_MODEL) * D_MODEL * 2)
    acc = pltpu.VMEM((N_Q_HEADS, VT_ROWS, tq), F32)
    if bounded:
        body, q_specs, q_args = _attn_mix_bounded_kernel, [q_spec], (q,)
        scratch_shapes = [acc]
        scratch = N_Q_HEADS * tq * 4 * VT_ROWS
    else:
        body, q_specs, q_args = _attn_mix_kernel, [q_spec, q_next], (q, q)
        scratch_shapes = [
            acc,
            pltpu.VMEM((N_Q_HEADS, tq), F32),
            pltpu.VMEM((N_Q_HEADS, tk, tq), F32),
            pltpu.VMEM((N_Q_HEADS, tk, tq), F32),
            pltpu.VMEM((N_Q_HEADS, tq), F32),
            pltpu.VMEM((N_Q_HEADS, tq), F32),
        ]
        scratch = N_Q_HEADS * tq * 4 * (VT_ROWS + 2 * tk + 3)
    return pl.pallas_call(
        functools.partial(body, tk=tk),
        grid=(batch, n_q),
        in_specs=q_specs + common,
        out_specs=pl.BlockSpec((tq, D_MODEL), row),
        out_shape=jax.ShapeDtypeStruct(x.shape, F32),
        scratch_shapes=scratch_shapes,
        compiler_params=pltpu.CompilerParams(
            dimension_semantics=("arbitrary", "arbitrary"),
            vmem_limit_bytes=_vmem_limit(blocks, scratch)),
        name="attn_mix_bounded" if bounded else "attn_mix",
    )(*q_args, kp, vt, sga, ybg, x, p["w_branch_a"], p["w_mix_out"])


FFN_COLS = 512


def _conv_ffn_kernel(xp_ref, x_ref, xn_ref, ng_ref, wu_ref, cw_ref, cb_ref, wd_ref, fg_ref,
                     o_ref, a_ref, *, final):
    i = pl.program_id(1)
    rows = x_ref.shape[0]
    x = x_ref[...]
    xp = jnp.where(i > 0, xp_ref[...], 0.0)
    xn = jnp.where(i < pl.num_programs(1) - 1, xn_ref[...], 0.0)
    hb = _rms(jnp.concatenate([xp, x, xn], axis=0), ng_ref[...]).astype(BF16)

    n_slab = FFN_COLS // LANES

    def up(lo):
        a = jnp.dot(hb, wu_ref[:, lo:lo + FFN_COLS], preferred_element_type=F32)
        for j in range(n_slab):
            a_ref[lo // LANES + j] = a[:, j * LANES:(j + 1) * LANES]

    def conv(lo):
        outs = []
        for j in range(n_slab):
            slab = lo // LANES + j
            cols = slice(lo + j * LANES, lo + (j + 1) * LANES)
            w = cw_ref[:, cols]
            outs.append(a_ref[slab, HALO - 1:HALO - 1 + rows, :] * w[0:1]
                        + a_ref[slab, HALO:HALO + rows, :] * w[1:2]
                        + a_ref[slab, HALO + 1:HALO + 1 + rows, :] * w[2:3] + cb_ref[:, cols])
        return jnp.concatenate(outs, axis=1)

    n_chunk = D_FF // FFN_COLS
    up(0)
    up(D_FF)
    y = x
    for c in range(n_chunk):
        if c + 1 < n_chunk:
            up((c + 1) * FFN_COLS)
            up(D_FF + (c + 1) * FFN_COLS)
        gate = _gelu(conv(c * FFN_COLS)) * conv(D_FF + c * FFN_COLS)
        y = y + jnp.dot(gate.astype(BF16), wd_ref[c * FFN_COLS:(c + 1) * FFN_COLS, :],
                        preferred_element_type=F32)
    o_ref[...] = _rms(y, fg_ref[...]) if final else y


def _conv_ffn(x, layer, p, batch, seq_len, final):
    tok, _, _ = _tiles(seq_len)
    n_t = seq_len // tok
    per = tok // HALO
    n_halo = seq_len // HALO
    row = lambda b, i: (b * n_t + i, 0)
    prev = lambda b, i: (b * n_halo + jnp.maximum(i * per - 1, 0), 0)
    nxt = lambda b, i: (b * n_halo + jnp.minimum((i + 1) * per, n_halo - 1), 0)
    layer3 = lambda b, i: (layer, 0, 0)
    blocks = 2 * tok * D_MODEL * 4 + 3 * D_MODEL * D_FF * 2
    return pl.pallas_call(
        functools.partial(_conv_ffn_kernel, final=final),
        grid=(batch, n_t),
        in_specs=[
            pl.BlockSpec((HALO, D_MODEL), prev),
            pl.BlockSpec((tok, D_MODEL), row),
            pl.BlockSpec((HALO, D_MODEL), nxt),
            pl.BlockSpec((None, 1, D_MODEL), layer3),
            pl.BlockSpec((None, D_MODEL, 2 * D_FF), layer3),
            pl.BlockSpec((None, CONV_W, 2 * D_FF), layer3),
            pl.BlockSpec((None, 1, 2 * D_FF), layer3),
            pl.BlockSpec((None, D_FF, D_MODEL), layer3),
            pl.BlockSpec((1, D_MODEL), lambda b, i: (0, 0)),
        ],
        out_specs=pl.BlockSpec((tok, D_MODEL), row),
        out_shape=jax.ShapeDtypeStruct(x.shape, F32),
        scratch_shapes=[pltpu.VMEM((2 * D_FF // LANES, tok + 2 * HALO, LANES), F32)],
        compiler_params=pltpu.CompilerParams(
            dimension_semantics=("arbitrary", "arbitrary"),
            vmem_limit_bytes=_vmem_limit(blocks, 2 * D_FF * (tok + 2 * HALO) * 4)),
        name="conv_ffn",
    )(x, x, x, p["ffn_norm_g"], p["w_up"], p["conv_w"], p["conv_b"], p["w_down"],
      p["final_norm_g"])


def _rope_tables(seq_len):
    t = jnp.arange(seq_len)
    half = HEAD_DIM // 2
    freq = ROPE_THETA ** (-jnp.arange(0, half, 2, dtype=F32) / half)
    ang_r = (t // GRID_W).astype(F32)[:, None] * freq[None, :]
    ang_c = (t % GRID_W).astype(F32)[:, None] * freq[None, :]
    cr, sr, cc, sc = jnp.cos(ang_r), jnp.sin(ang_r), jnp.cos(ang_c), jnp.sin(ang_c)
    zero = jnp.zeros_like(sr)
    per_head = lambda parts: jnp.tile(jnp.concatenate(parts, axis=1), (1, LANES // HEAD_DIM))
    cos = per_head([cr, cr, cc, cc])
    sa = per_head([-sr, zero, -sc, zero])
    sb = per_head([zero, sr, zero, sc])
    return cos, sa, sb


def _prepare_params(attn_norm_g, w_in, q_norm_g, k_norm_g, sg_norm_g, sg_w, sg_b,
                    w_branch_a, w_branch_b, w_mix_out, ffn_norm_g, w_up, conv_w, conv_b,
                    w_down, final_norm_g):
    depth = w_in.shape[0]
    order = np.concatenate([
        np.arange(h * HEAD_DIM, (h + 1) * HEAD_DIM)
        for j in range(Q_PER_KV) for h in (j, Q_PER_KV + j)])
    w_in_p = jnp.concatenate([w_in[:, :, order], w_in[:, :, Q_WIDTH:]], axis=2).astype(BF16)
    two_heads = lambda g: jnp.tile(g, (1, LANES // HEAD_DIM)).reshape(depth, 1, LANES)
    blk = np.arange(LANES) // HEAD_DIM
    return dict(
        attn_norm_g=attn_norm_g.reshape(depth, 1, D_MODEL),
        w_in=w_in_p,
        q_norm_g=q_norm_g,
        score_bound=_score_bound(q_norm_g, k_norm_g),
        k_norm_g=two_heads(k_norm_g),
        head_ones=jnp.asarray(blk[:, None] == blk[None, :], BF16),
        sg_norm_g=sg_norm_g.reshape(depth, 1, SG_WIDTH),
        sg_w=sg_w.astype(BF16),
        sg_bias=jnp.repeat(jnp.swapaxes(sg_b, 1, 2), SG_GROUP_DIM, axis=2),
        w_branch_a=w_branch_a.astype(BF16),
        w_branch_b=w_branch_b.astype(BF16),
        w_mix_out=w_mix_out.astype(BF16),
        ffn_norm_g=ffn_norm_g.reshape(depth, 1, D_MODEL),
        w_up=w_up.astype(BF16),
        conv_w=conv_w,
        conv_b=conv_b.reshape(depth, 1, 2 * D_FF),
        w_down=w_down.astype(BF16),
        final_norm_g=final_norm_g.reshape(1, D_MODEL),
    )


def _score_bound(q_norm_g, k_norm_g):
    rounding_margin = 1.02
    gq = jnp.max(jnp.abs(q_norm_g), axis=1)
    gk = jnp.max(jnp.abs(k_norm_g), axis=1)
    return HEAD_DIM * Q_SCALE * gq * gk * rounding_margin


def _trunk(x3, p):
    batch, seq_len, _ = x3.shape
    depth = p["w_in"].shape[0]
    tok, tq, tk = _tiles(seq_len)
    assert seq_len % tok == 0 and seq_len % tq == 0 and seq_len % (2 * tk) == 0
    assert tok % (2 * CHUNK) == 0 and tok % tk == 0 and tok % HALO == 0
    p = dict(p)
    p["rope_cos"], p["rope_sa"], p["rope_sb"] = _rope_tables(seq_len)
    g = p["q_norm_g"][:, :, None] * Q_SCALE
    head = lambda t: t[:, :HEAD_DIM].T[None]
    p["q_cos"] = head(p["rope_cos"]) * g
    p["q_sa"] = head(p["rope_sa"]) * jnp.roll(g, -ROPE_QUARTER, axis=1)
    p["q_sb"] = head(p["rope_sb"]) * jnp.roll(g, ROPE_QUARTER, axis=1)
    x = x3.reshape(batch * seq_len, D_MODEL)
    for layer in range(depth):
        q, kp, vt, sga, ybg = _in_proj(x, layer, p, seq_len)
        attn = lambda bounded: functools.partial(
            _attn_mix, layer=layer, p=p, batch=batch, seq_len=seq_len, bounded=bounded)
        x = lax.cond(p["score_bound"][layer] <= EXP2_SAFE_BOUND, attn(True), attn(False),
                     x, q, kp, vt, sga, ybg)
        x = _conv_ffn(x, layer, p, batch, seq_len, final=(layer == depth - 1))
    return x.reshape(batch, seq_len, D_MODEL)


def kernel(x_prompt, x_sample, attn_norm_g, w_in, q_norm_g, k_norm_g, sg_norm_g, sg_w, sg_b,
           w_branch_a, w_branch_b, w_mix_out, ffn_norm_g, w_up, conv_w, conv_b, w_down,
           final_norm_g):
    p = _prepare_params(attn_norm_g, w_in, q_norm_g, k_norm_g, sg_norm_g, sg_w, sg_b,
                        w_branch_a, w_branch_b, w_mix_out, ffn_norm_g, w_up, conv_w, conv_b,
                        w_down, final_norm_g)
    return (_trunk(x_prompt, p), _trunk(x_sample, p))
```

```python
import functools
import math

import jax
import jax.numpy as jnp
import numpy as np
from jax import lax
from jax.experimental import pallas as pl
from jax.experimental.pallas import tpu as pltpu

D_MODEL = 1024
GRID_W = 64
HEAD_DIM = 64
N_Q_HEADS = 8
N_KV_HEADS = 2
Q_PER_KV = N_Q_HEADS // N_KV_HEADS
Q_WIDTH = N_Q_HEADS * HEAD_DIM
KV_WIDTH = N_KV_HEADS * HEAD_DIM
SG_GROUPS = 8
SG_GROUP_DIM = 64
SG_WIDTH = SG_GROUPS * SG_GROUP_DIM
CHUNK = 128
D_FF = 2048
CONV_W = 3
ROPE_THETA = 10000.0
EPS = 1e-6
IN_WIDTH = Q_WIDTH + 2 * KV_WIDTH + 2 * SG_WIDTH + 2 * D_MODEL

LANES = 128
F32_SUBLANES = 8
BF16_SUBLANES = 16
V7X_VMEM_BYTES = 64 * 1024 * 1024

_OFF_K = Q_WIDTH
_OFF_V = _OFF_K + KV_WIDTH
_OFF_U = _OFF_V + KV_WIDTH
_OFF_VS = _OFF_U + SG_WIDTH
_OFF_GA = _OFF_VS + SG_WIDTH
_OFF_GB = _OFF_GA + D_MODEL

VT_ROWS = HEAD_DIM + BF16_SUBLANES
ROPE_QUARTER = HEAD_DIM // 4
NEG_BIG = -1e30
Q_SCALE = math.log2(math.e) / math.sqrt(HEAD_DIM)
HALO = F32_SUBLANES
EXP2_SAFE_BOUND = 60.0

F32 = jnp.float32
BF16 = jnp.bfloat16


def _tiles(seq_len):
    tok = min(512, seq_len)
    tq = min(512, seq_len)
    tk = min(256, seq_len // 2)
    return tok, tq, tk


VMEM_RESERVE_BYTES = 4 << 20
VMEM_TEMP_BYTES = 24 << 20


def _vmem_limit(block_bytes, scratch_bytes=0):
    want = 2 * block_bytes + scratch_bytes + VMEM_TEMP_BYTES
    return int(min(V7X_VMEM_BYTES - VMEM_RESERVE_BYTES, want))


def _rms(x, g):
    return x * lax.rsqrt(jnp.mean(x * x, axis=-1, keepdims=True) + EPS) * g


def _gelu(x):
    c = math.sqrt(2.0 / math.pi)
    half = 0.5 * x
    return half + half * jnp.tanh(x * (c + (c * 0.044715) * (x * x)))


def _lane_lt64(shape):
    lane = lax.broadcasted_iota(jnp.int32, shape, len(shape) - 1)
    return (lane % LANES) < HEAD_DIM


def _in_proj_kernel(x_ref, ng_ref, w_ref, qcos_ref, qsa_ref, qsb_ref, kg_ref, cos_ref, sa_ref,
                    sb_ref, bd_ref, sgg_ref, sgw_ref, sgb_ref, wb_ref,
                    q_ref, k_ref, vt_ref, sga_ref, ybg_ref, *, tk):
    rows = x_ref.shape[0]
    hb = _rms(x_ref[...], ng_ref[...]).astype(BF16)

    def proj(lo, hi):
        return jnp.dot(hb, w_ref[:, lo:hi], preferred_element_type=F32)

    cos = cos_ref[...]
    sa = sa_ref[...]
    sb = sb_ref[...]

    def head_norm_rope(z, gain):
        ss = jnp.dot((z * z).astype(BF16), bd_ref[...], preferred_element_type=F32)
        zn = z * lax.rsqrt(ss * (1.0 / HEAD_DIM) + EPS) * gain
        return (zn * cos + pltpu.roll(zn, LANES - ROPE_QUARTER, 1) * sa
                + pltpu.roll(zn, ROPE_QUARTER, 1) * sb)

    zq = proj(0, Q_WIDTH)
    zkv = proj(_OFF_K, _OFF_U)

    qc, qa, qb = qcos_ref[...], qsa_ref[...], qsb_ref[...]
    for j in range(Q_WIDTH // LANES):
        zt = zq[:, j * LANES:(j + 1) * LANES].T
        for hh in range(LANES // HEAD_DIM):
            z = zt[hh * HEAD_DIM:(hh + 1) * HEAD_DIM]
            ss = jnp.sum(z * z, axis=0, keepdims=True)
            zn = z * lax.rsqrt(ss * (1.0 / HEAD_DIM) + EPS)
            up = jnp.concatenate([zn[ROPE_QUARTER:], zn[:ROPE_QUARTER]], axis=0)
            dn = jnp.concatenate([zn[-ROPE_QUARTER:], zn[:-ROPE_QUARTER]], axis=0)
            q_ref[j, hh * HEAD_DIM:(hh + 1) * HEAD_DIM, :] = (
                zn * qc + up * qa + dn * qb).astype(BF16)

    zu = proj(_OFF_U, _OFF_VS)

    kr = head_norm_rope(zkv[:, :KV_WIDTH], kg_ref[...])
    first = _lane_lt64(kr.shape)
    k_ref[0] = jnp.where(first, kr, 0.0).astype(BF16)
    k_ref[1] = jnp.where(first, 0.0, kr).astype(BF16)

    vt = zkv[:, KV_WIDTH:].T
    ones = jnp.ones((VT_ROWS - HEAD_DIM, tk), BF16)
    for g in range(N_KV_HEADS):
        for r in range(rows // tk):
            vt_ref[g, r, :HEAD_DIM, :] = vt[g * HEAD_DIM:(g + 1) * HEAD_DIM,
                                            r * tk:(r + 1) * tk].astype(BF16)
            vt_ref[g, r, HEAD_DIM:, :] = ones

    zvs = proj(_OFF_VS, _OFF_GA)
    u = _gelu(zu)
    zga = proj(_OFF_GA, _OFF_GB)

    vs = _rms(_gelu(zvs), sgg_ref[...]).astype(BF16)
    zgb = proj(_OFF_GB, IN_WIDTH)
    sga_ref[...] = jax.nn.sigmoid(zga).astype(BF16)
    pair = 2 * CHUNK
    lt64 = _lane_lt64((CHUNK, 2 * LANES))
    sgu_rows = []
    for c in range(rows // pair):
        r0 = c * pair
        slabs0, slabs1 = [], []
        for s in range(SG_WIDTH // LANES):
            cols = slice(s * LANES, (s + 1) * LANES)
            rhs = jnp.concatenate([vs[r0:r0 + CHUNK, cols], vs[r0 + CHUNK:r0 + pair, cols]], axis=1)
            oa = jnp.dot(sgw_ref[2 * s], rhs, preferred_element_type=F32)
            ob = jnp.dot(sgw_ref[2 * s + 1], rhs, preferred_element_type=F32)
            bias = sgb_ref[:, cols]
            sp = jnp.where(lt64, oa, ob) + jnp.concatenate([bias, bias], axis=1)
            slabs0.append(u[r0:r0 + CHUNK, cols] * sp[:, :LANES])
            slabs1.append(u[r0 + CHUNK:r0 + pair, cols] * sp[:, LANES:])
        sgu_rows.append(jnp.concatenate(slabs0, axis=1))
        sgu_rows.append(jnp.concatenate(slabs1, axis=1))
    sgu = jnp.concatenate(sgu_rows, axis=0).astype(BF16)
    sgb_gate = jax.nn.sigmoid(zgb)
    yb = jnp.dot(sgu, wb_ref[...], preferred_element_type=F32)
    ybg_ref[...] = (sgb_gate * yb).astype(BF16)


def _in_proj(x, layer, p, seq_len):
    n = x.shape[0]
    tok, _, tk = _tiles(seq_len)
    n_pos = seq_len // tok
    const2 = lambda i: (0, 0)
    layer3 = lambda i: (layer, 0, 0)
    blocks = (tok * D_MODEL * 4 + D_MODEL * IN_WIDTH * 2 + SG_WIDTH * D_MODEL * 2
              + tok * (Q_WIDTH + 2 * LANES + 2 * VT_ROWS + 2 * D_MODEL) * 2)
    return pl.pallas_call(
        functools.partial(_in_proj_kernel, tk=tk),
        grid=(n // tok,),
        in_specs=[
            pl.BlockSpec((tok, D_MODEL), lambda i: (i, 0)),
            pl.BlockSpec((None, 1, D_MODEL), layer3),
            pl.BlockSpec((None, D_MODEL, IN_WIDTH), layer3),
            pl.BlockSpec((None, HEAD_DIM, tok), lambda i: (layer, 0, i % n_pos)),
            pl.BlockSpec((None, HEAD_DIM, tok), lambda i: (layer, 0, i % n_pos)),
            pl.BlockSpec((None, HEAD_DIM, tok), lambda i: (layer, 0, i % n_pos)),
            pl.BlockSpec((None, 1, LANES), layer3),
            pl.BlockSpec((tok, LANES), lambda i: (i % n_pos, 0)),
            pl.BlockSpec((tok, LANES), lambda i: (i % n_pos, 0)),
            pl.BlockSpec((tok, LANES), lambda i: (i % n_pos, 0)),
            pl.BlockSpec((LANES, LANES), const2),
            pl.BlockSpec((None, 1, SG_WIDTH), layer3),
            pl.BlockSpec((None, SG_GROUPS, CHUNK, CHUNK), lambda i: (layer, 0, 0, 0)),
            pl.BlockSpec((None, CHUNK, SG_WIDTH), layer3),
            pl.BlockSpec((None, SG_WIDTH, D_MODEL), layer3),
        ],
        out_specs=[
            pl.BlockSpec((Q_WIDTH // LANES, LANES, tok), lambda i: (0, 0, i)),
            pl.BlockSpec((N_KV_HEADS, tok, LANES), lambda i: (0, i, 0)),
            pl.BlockSpec((N_KV_HEADS, tok // tk, VT_ROWS, tk), lambda i: (0, i, 0, 0)),
            pl.BlockSpec((tok, D_MODEL), lambda i: (i, 0)),
            pl.BlockSpec((tok, D_MODEL), lambda i: (i, 0)),
        ],
        out_shape=[
            jax.ShapeDtypeStruct((Q_WIDTH // LANES, LANES, n), BF16),
            jax.ShapeDtypeStruct((N_KV_HEADS, n, LANES), BF16),
            jax.ShapeDtypeStruct((N_KV_HEADS, n // tk, VT_ROWS, tk), BF16),
            jax.ShapeDtypeStruct((n, D_MODEL), BF16),
            jax.ShapeDtypeStruct((n, D_MODEL), BF16),
        ],
        compiler_params=pltpu.CompilerParams(
            dimension_semantics=("arbitrary",), vmem_limit_bytes=_vmem_limit(blocks)),
        name="in_proj",
    )(x, p["attn_norm_g"], p["w_in"], p["q_cos"], p["q_sa"], p["q_sb"], p["k_norm_g"],
      p["rope_cos"], p["rope_sa"], p["rope_sb"], p["head_ones"],
      p["sg_norm_g"], p["sg_w"], p["sg_bias"], p["w_branch_b"])


def _attn_mix_kernel(q_ref, qn_ref, k_ref, vt_ref, sga_ref, ybg_ref, x_ref, wa_ref, wm_ref,
                     o_ref, acc_ref, m_ref, s0_ref, s1_ref, bm0_ref, bm1_ref, *, tk):
    tq = q_ref.shape[2]
    n_kv = k_ref.shape[1] // tk
    acc_ref[...] = jnp.zeros(acc_ref.shape, F32)
    m_ref[...] = jnp.full(m_ref.shape, NEG_BIG, F32)

    def scores(blk, s_ref, bm_ref, h, qt_ref):
        g, j = divmod(h, Q_PER_KV)
        start = pl.multiple_of(blk * tk, tk)
        kb = k_ref[g, pl.ds(start, tk), :]
        s = jnp.dot(kb, qt_ref[j], preferred_element_type=F32)
        s_ref[h] = s
        bm_ref[h:h + 1, :] = jnp.max(s, axis=0, keepdims=True)

    def accumulate(blk, s_ref, bm_ref, h):
        g = h // Q_PER_KV
        vb = vt_ref[g, blk]
        m_old = m_ref[h:h + 1, :]
        m_new = jnp.maximum(m_old, bm_ref[h:h + 1, :])
        alpha = jnp.exp2(m_old - m_new)
        pt = jnp.exp2(s_ref[h] - m_new).astype(BF16)
        pv = jnp.dot(vb, pt, preferred_element_type=F32)
        acc_ref[h] = alpha * acc_ref[h] + pv
        m_ref[h:h + 1, :] = m_new

    def step(score_blk, score_refs, acc_blk, acc_refs, qt_ref=q_ref):
        for h in range(N_Q_HEADS):
            if score_blk is not None:
                scores(score_blk, *score_refs, h, qt_ref)
            if acc_blk is not None:
                accumulate(acc_blk, *acc_refs, h)

    even = (s0_ref, bm0_ref)
    odd = (s1_ref, bm1_ref)

    @pl.when(pl.program_id(1) == 0)
    def _():
        step(0, even, None, None)

    def kv_pair(t, carry):
        step(2 * t + 1, odd, 2 * t, even)
        step(2 * t + 2, even, 2 * t + 1, odd)
        return carry

    lax.fori_loop(0, n_kv // 2 - 1, kv_pair, 0)
    step(n_kv - 1, odd, n_kv - 2, even)
    step(0, even, n_kv - 1, odd, qn_ref)

    _attn_epilogue(acc_ref, sga_ref, ybg_ref, x_ref, wa_ref, wm_ref, o_ref)


def _attn_epilogue(acc_ref, sga_ref, ybg_ref, x_ref, wa_ref, wm_ref, o_ref):
    heads = []
    for h in range(N_Q_HEADS):
        a = acc_ref[h]
        heads.append(a[:HEAD_DIM] * (1.0 / a[HEAD_DIM:HEAD_DIM + 1]))
    ot = jnp.concatenate(heads, axis=0)
    ya = jnp.dot(ot.T.astype(BF16), wa_ref[...], preferred_element_type=F32)
    mix = sga_ref[...].astype(F32) * ya + ybg_ref[...].astype(F32)
    o_ref[...] = x_ref[...] + jnp.dot(mix.astype(BF16), wm_ref[...],
                                      preferred_element_type=F32)


SCORE_AHEAD = 2
KV_BLOCKS_PER_ITER = 8


def _attn_mix_bounded_kernel(q_ref, k_ref, vt_ref, sga_ref, ybg_ref, x_ref, wa_ref, wm_ref,
                             o_ref, acc_ref, *, tk):
    n_kv = k_ref.shape[1] // tk
    acc_ref[...] = jnp.zeros(acc_ref.shape, F32)

    def qk(blk, h):
        g, j = divmod(h, Q_PER_KV)
        start = pl.multiple_of(blk * tk, tk)
        kb = k_ref[g, pl.ds(start, tk), :]
        return jnp.dot(kb, q_ref[j], preferred_element_type=F32)

    def pv(blk, h, s):
        vb = vt_ref[h // Q_PER_KV, blk]
        pt = jnp.exp2(s).astype(BF16)
        acc_ref[h] = acc_ref[h] + jnp.dot(vb, pt, preferred_element_type=F32)

    per_iter = math.gcd(n_kv, KV_BLOCKS_PER_ITER)

    def kv_group(t, carry):
        items = [(per_iter * t + d, h) for d in range(per_iter) for h in range(N_Q_HEADS)]
        pending = [qk(*items[n]) for n in range(SCORE_AHEAD)]
        for n in range(len(items)):
            if n + SCORE_AHEAD < len(items):
                pending.append(qk(*items[n + SCORE_AHEAD]))
            pv(*items[n], pending.pop(0))
        return carry

    lax.fori_loop(0, n_kv // per_iter, kv_group, 0)
    _attn_epilogue(acc_ref, sga_ref, ybg_ref, x_ref, wa_ref, wm_ref, o_ref)


def _attn_mix(x, q, kp, vt, sga, ybg, layer, p, batch, seq_len, bounded):
    _, tq, tk = _tiles(seq_len)
    n_kv = seq_len // tk
    n_q = seq_len // tq
    row = lambda b, i: (b * n_q + i, 0)
    q_spec = pl.BlockSpec((Q_WIDTH // LANES, LANES, tq), lambda b, i: (0, 0, b * n_q + i))
    q_next = pl.BlockSpec((Q_WIDTH // LANES, LANES, tq),
                          lambda b, i: (0, 0, b * n_q + jnp.minimum(i + 1, n_q - 1)))
    common = [
        pl.BlockSpec((N_KV_HEADS, seq_len, LANES), lambda b, i: (0, b, 0)),
        pl.BlockSpec((N_KV_HEADS, n_kv, VT_ROWS, tk), lambda b, i: (0, b, 0, 0)),
        pl.BlockSpec((tq, D_MODEL), row),
        pl.BlockSpec((tq, D_MODEL), row),
        pl.BlockSpec((tq, D_MODEL), row),
        pl.BlockSpec((None, Q_WIDTH, D_MODEL), lambda b, i: (layer, 0, 0)),
        pl.BlockSpec((None, D_MODEL, D_MODEL), lambda b, i: (layer, 0, 0)),
    ]
    blocks = (tq * (2 * Q_WIDTH + 2 * D_MODEL) * 2 + 2 * tq * D_MODEL * 4
              + N_KV_HEADS * seq_len * (LANES + VT_ROWS) * 2
              + (Q_WIDTH + D_MODEL) * D_MODEL * 2)
    acc = pltpu.VMEM((N_Q_HEADS, VT_ROWS, tq), F32)
    if bounded:
        body, q_specs, q_args = _attn_mix_bounded_kernel, [q_spec], (q,)
        scratch_shapes = [acc]
        scratch = N_Q_HEADS * tq * 4 * VT_ROWS
    else:
        body, q_specs, q_args = _attn_mix_kernel, [q_spec, q_next], (q, q)
        scratch_shapes = [
            acc,
            pltpu.VMEM((N_Q_HEADS, tq), F32),
            pltpu.VMEM((N_Q_HEADS, tk, tq), F32),
            pltpu.VMEM((N_Q_HEADS, tk, tq), F32),
            pltpu.VMEM((N_Q_HEADS, tq), F32),
            pltpu.VMEM((N_Q_HEADS, tq), F32),
        ]
        scratch = N_Q_HEADS * tq * 4 * (VT_ROWS + 2 * tk + 3)
    return pl.pallas_call(
        functools.partial(body, tk=tk),
        grid=(batch, n_q),
        in_specs=q_specs + common,
        out_specs=pl.BlockSpec((tq, D_MODEL), row),
        out_shape=jax.ShapeDtypeStruct(x.shape, F32),
        scratch_shapes=scratch_shapes,
        compiler_params=pltpu.CompilerParams(
            dimension_semantics=("arbitrary", "arbitrary"),
            vmem_limit_bytes=_vmem_limit(blocks, scratch)),
        name="attn_mix_bounded" if bounded else "attn_mix",
    )(*q_args, kp, vt, sga, ybg, x, p["w_branch_a"], p["w_mix_out"])


FFN_COLS = 512


def _conv_ffn_kernel(xp_ref, x_ref, xn_ref, ng_ref, wu_ref, cw_ref, cb_ref, wd_ref, fg_ref,
                     o_ref, a_ref, *, final):
    i = pl.program_id(1)
    rows = x_ref.shape[0]
    x = x_ref[...]
    xp = jnp.where(i > 0, xp_ref[...], 0.0)
    xn = jnp.where(i < pl.num_programs(1) - 1, xn_ref[...], 0.0)
    hb = _rms(jnp.concatenate([xp, x, xn], axis=0), ng_ref[...]).astype(BF16)

    n_slab = FFN_COLS // LANES

    def up(lo):
        a = jnp.dot(hb, wu_ref[:, lo:lo + FFN_COLS], preferred_element_type=F32)
        for j in range(n_slab):
            a_ref[lo // LANES + j] = a[:, j * LANES:(j + 1) * LANES]

    def conv(lo):
        outs = []
        for j in range(n_slab):
            slab = lo // LANES + j
            cols = slice(lo + j * LANES, lo + (j + 1) * LANES)
            w = cw_ref[:, cols]
            outs.append(a_ref[slab, HALO - 1:HALO - 1 + rows, :] * w[0:1]
                        + a_ref[slab, HALO:HALO + rows, :] * w[1:2]
                        + a_ref[slab, HALO + 1:HALO + 1 + rows, :] * w[2:3] + cb_ref[:, cols])
        return jnp.concatenate(outs, axis=1)

    n_chunk = D_FF // FFN_COLS
    up(0)
    up(D_FF)
    y = x
    for c in range(n_chunk):
        if c + 1 < n_chunk:
            up((c + 1) * FFN_COLS)
            up(D_FF + (c + 1) * FFN_COLS)
        gate = _gelu(conv(c * FFN_COLS)) * conv(D_FF + c * FFN_COLS)
        y = y + jnp.dot(gate.astype(BF16), wd_ref[c * FFN_COLS:(c + 1) * FFN_COLS, :],
                        preferred_element_type=F32)
    o_ref[...] = _rms(y, fg_ref[...]) if final else y


def _conv_ffn(x, layer, p, batch, seq_len, final):
    tok, _, _ = _tiles(seq_len)
    n_t = seq_len // tok
    per = tok // HALO
    n_halo = seq_len // HALO
    row = lambda b, i: (b * n_t + i, 0)
    prev = lambda b, i: (b * n_halo + jnp.maximum(i * per - 1, 0), 0)
    nxt = lambda b, i: (b * n_halo + jnp.minimum((i + 1) * per, n_halo - 1), 0)
    layer3 = lambda b, i: (layer, 0, 0)
    blocks = 2 * tok * D_MODEL * 4 + 3 * D_MODEL * D_FF * 2
    return pl.pallas_call(
        functools.partial(_conv_ffn_kernel, final=final),
        grid=(batch, n_t),
        in_specs=[
            pl.BlockSpec((HALO, D_MODEL), prev),
            pl.BlockSpec((tok, D_MODEL), row),
            pl.BlockSpec((HALO, D_MODEL), nxt),
            pl.BlockSpec((None, 1, D_MODEL), layer3),
            pl.BlockSpec((None, D_MODEL, 2 * D_FF), layer3),
            pl.BlockSpec((None, CONV_W, 2 * D_FF), layer3),
            pl.BlockSpec((None, 1, 2 * D_FF), layer3),
            pl.BlockSpec((None, D_FF, D_MODEL), layer3),
            pl.BlockSpec((1, D_MODEL), lambda b, i: (0, 0)),
        ],
        out_specs=pl.BlockSpec((tok, D_MODEL), row),
        out_shape=jax.ShapeDtypeStruct(x.shape, F32),
        scratch_shapes=[pltpu.VMEM((2 * D_FF // LANES, tok + 2 * HALO, LANES), F32)],
        compiler_params=pltpu.CompilerParams(
            dimension_semantics=("arbitrary", "arbitrary"),
            vmem_limit_bytes=_vmem_limit(blocks, 2 * D_FF * (tok + 2 * HALO) * 4)),
        name="conv_ffn",
    )(x, x, x, p["ffn_norm_g"], p["w_up"], p["conv_w"], p["conv_b"], p["w_down"],
      p["final_norm_g"])


def _rope_tables(seq_len):
    t = jnp.arange(seq_len)
    half = HEAD_DIM // 2
    freq = ROPE_THETA ** (-jnp.arange(0, half, 2, dtype=F32) / half)
    ang_r = (t // GRID_W).astype(F32)[:, None] * freq[None, :]
    ang_c = (t % GRID_W).astype(F32)[:, None] * freq[None, :]
    cr, sr, cc, sc = jnp.cos(ang_r), jnp.sin(ang_r), jnp.cos(ang_c), jnp.sin(ang_c)
    zero = jnp.zeros_like(sr)
    per_head = lambda parts: jnp.tile(jnp.concatenate(parts, axis=1), (1, LANES // HEAD_DIM))
    cos = per_head([cr, cr, cc, cc])
    sa = per_head([-sr, zero, -sc, zero])
    sb = per_head([zero, sr, zero, sc])
    return cos, sa, sb


def _prepare_params(attn_norm_g, w_in, q_norm_g, k_norm_g, sg_norm_g, sg_w, sg_b,
                    w_branch_a, w_branch_b, w_mix_out, ffn_norm_g, w_up, conv_w, conv_b,
                    w_down, final_norm_g):
    depth = w_in.shape[0]
    order = np.concatenate([
        np.arange(h * HEAD_DIM, (h + 1) * HEAD_DIM)
        for j in range(Q_PER_KV) for h in (j, Q_PER_KV + j)])
    w_in_p = jnp.concatenate([w_in[:, :, order], w_in[:, :, Q_WIDTH:]], axis=2).astype(BF16)
    two_heads = lambda g: jnp.tile(g, (1, LANES // HEAD_DIM)).reshape(depth, 1, LANES)
    blk = np.arange(LANES) // HEAD_DIM
    return dict(
        attn_norm_g=attn_norm_g.reshape(depth, 1, D_MODEL),
        w_in=w_in_p,
        q_norm_g=q_norm_g,
        score_bound=_score_bound(q_norm_g, k_norm_g),
        k_norm_g=two_heads(k_norm_g),
        head_ones=jnp.asarray(blk[:, None] == blk[None, :], BF16),
        sg_norm_g=sg_norm_g.reshape(depth, 1, SG_WIDTH),
        sg_w=sg_w.astype(BF16),
        sg_bias=jnp.repeat(jnp.swapaxes(sg_b, 1, 2), SG_GROUP_DIM, axis=2),
        w_branch_a=w_branch_a.astype(BF16),
        w_branch_b=w_branch_b.astype(BF16),
        w_mix_out=w_mix_out.astype(BF16),
        ffn_norm_g=ffn_norm_g.reshape(depth, 1, D_MODEL),
        w_up=w_up.astype(BF16),
        conv_w=conv_w,
        conv_b=conv_b.reshape(depth, 1, 2 * D_FF),
        w_down=w_down.astype(BF16),
        final_norm_g=final_norm_g.reshape(1, D_MODEL),
    )


def _score_bound(q_norm_g, k_norm_g):
    rounding_margin = 1.02
    gq = jnp.max(jnp.abs(q_norm_g), axis=1)
    gk = jnp.max(jnp.abs(k_norm_g), axis=1)
    return HEAD_DIM * Q_SCALE * gq * gk * rounding_margin


def _trunk(x3, p):
    batch, seq_len, _ = x3.shape
    depth = p["w_in"].shape[0]
    tok, tq, tk = _tiles(seq_len)
    assert seq_len % tok == 0 and seq_len % tq == 0 and seq_len % (2 * tk) == 0
    assert tok % (2 * CHUNK) == 0 and tok % tk == 0 and tok % HALO == 0
    p = dict(p)
    p["rope_cos"], p["rope_sa"], p["rope_sb"] = _rope_tables(seq_len)
    g = p["q_norm_g"][:, :, None] * Q_SCALE
    head = lambda t: t[:, :HEAD_DIM].T[None]
    p["q_cos"] = head(p["rope_cos"]) * g
    p["q_sa"] = head(p["rope_sa"]) * jnp.roll(g, -ROPE_QUARTER, axis=1)
    p["q_sb"] = head(p["rope_sb"]) * jnp.roll(g, ROPE_QUARTER, axis=1)
    x = x3.reshape(batch * seq_len, D_MODEL)
    for layer in range(depth):
        q, kp, vt, sga, ybg = _in_proj(x, layer, p, seq_len)
        attn = lambda bounded: functools.partial(
            _attn_mix, layer=layer, p=p, batch=batch, seq_len=seq_len, bounded=bounded)
        x = lax.cond(p["score_bound"][layer] <= EXP2_SAFE_BOUND, attn(True), attn(False),
                     x, q, kp, vt, sga, ybg)
        x = _conv_ffn(x, layer, p, batch, seq_len, final=(layer == depth - 1))
    return x.reshape(batch, seq_len, D_MODEL)


def kernel(x_prompt, x_sample, attn_norm_g, w_in, q_norm_g, k_norm_g, sg_norm_g, sg_w, sg_b,
           w_branch_a, w_branch_b, w_mix_out, ffn_norm_g, w_up, conv_w, conv_b, w_down,
           final_norm_g):
    p = _prepare_params(attn_norm_g, w_in, q_norm_g, k_norm_g, sg_norm_g, sg_w, sg_b,
                        w_branch_a, w_branch_b, w_mix_out, ffn_norm_g, w_up, conv_w, conv_b,
                        w_down, final_norm_g)
    return (_trunk(x_prompt, p), _trunk(x_sample, p))
```

```python
import functools
import math

import jax
import jax.numpy as jnp
import numpy as np
from jax import lax
from jax.experimental import pallas as pl
from jax.experimental.pallas import tpu as pltpu

D_MODEL = 1024
GRID_W = 64
HEAD_DIM = 64
N_Q_HEADS = 8
N_KV_HEADS = 2
Q_PER_KV = N_Q_HEADS // N_KV_HEADS
Q_WIDTH = N_Q_HEADS * HEAD_DIM
KV_WIDTH = N_KV_HEADS * HEAD_DIM
SG_GROUPS = 8
SG_GROUP_DIM = 64
SG_WIDTH = SG_GROUPS * SG_GROUP_DIM
CHUNK = 128
D_FF = 2048
CONV_W = 3
ROPE_THETA = 10000.0
EPS = 1e-6
IN_WIDTH = Q_WIDTH + 2 * KV_WIDTH + 2 * SG_WIDTH + 2 * D_MODEL

LANES = 128
F32_SUBLANES = 8
BF16_SUBLANES = 16
V7X_VMEM_BYTES = 64 * 1024 * 1024

_OFF_K = Q_WIDTH
_OFF_V = _OFF_K + KV_WIDTH
_OFF_U = _OFF_V + KV_WIDTH
_OFF_VS = _OFF_U + SG_WIDTH
_OFF_GA = _OFF_VS + SG_WIDTH
_OFF_GB = _OFF_GA + D_MODEL

VT_ROWS = HEAD_DIM + BF16_SUBLANES
ROPE_QUARTER = HEAD_DIM // 4
NEG_BIG = -1e30
Q_SCALE = math.log2(math.e) / math.sqrt(HEAD_DIM)
HALO = F32_SUBLANES
EXP2_SAFE_BOUND = 60.0

F32 = jnp.float32
BF16 = jnp.bfloat16


def _tiles(seq_len):
    tok = min(512, seq_len)
    tq = min(512, seq_len)
    tk = min(256, seq_len // 2)
    return tok, tq, tk


VMEM_RESERVE_BYTES = 4 << 20
VMEM_TEMP_BYTES = 24 << 20


def _vmem_limit(block_bytes, scratch_bytes=0):
    want = 2 * block_bytes + scratch_bytes + VMEM_TEMP_BYTES
    return int(min(V7X_VMEM_BYTES - VMEM_RESERVE_BYTES, want))


def _rms(x, g):
    return x * lax.rsqrt(jnp.mean(x * x, axis=-1, keepdims=True) + EPS) * g


def _gelu(x):
    c = math.sqrt(2.0 / math.pi)
    half = 0.5 * x
    return half + half * jnp.tanh(x * (c + (c * 0.044715) * (x * x)))


def _lane_lt64(shape):
    lane = lax.broadcasted_iota(jnp.int32, shape, len(shape) - 1)
    return (lane % LANES) < HEAD_DIM


def _in_proj_kernel(x_ref, ng_ref, w_ref, qcos_ref, qsa_ref, qsb_ref, kg_ref, cos_ref, sa_ref,
                    sb_ref, bd_ref, sgg_ref, sgw_ref, sgb_ref, wb_ref,
                    q_ref, k_ref, vt_ref, sga_ref, ybg_ref, *, tk):
    rows = x_ref.shape[0]
    hb = _rms(x_ref[...], ng_ref[...]).astype(BF16)

    def proj(lo, hi):
        return jnp.dot(hb, w_ref[:, lo:hi], preferred_element_type=F32)

    cos = cos_ref[...]
    sa = sa_ref[...]
    sb = sb_ref[...]

    def head_norm_rope(z, gain):
        ss = jnp.dot((z * z).astype(BF16), bd_ref[...], preferred_element_type=F32)
        zn = z * lax.rsqrt(ss * (1.0 / HEAD_DIM) + EPS) * gain
        return (zn * cos + pltpu.roll(zn, LANES - ROPE_QUARTER, 1) * sa
                + pltpu.roll(zn, ROPE_QUARTER, 1) * sb)

    zq = proj(0, Q_WIDTH)
    zkv = proj(_OFF_K, _OFF_U)

    qc, qa, qb = qcos_ref[...], qsa_ref[...], qsb_ref[...]
    for j in range(Q_WIDTH // LANES):
        zt = zq[:, j * LANES:(j + 1) * LANES].T
        for hh in range(LANES // HEAD_DIM):
            z = zt[hh * HEAD_DIM:(hh + 1) * HEAD_DIM]
            ss = jnp.sum(z * z, axis=0, keepdims=True)
            zn = z * lax.rsqrt(ss * (1.0 / HEAD_DIM) + EPS)
            up = jnp.concatenate([zn[ROPE_QUARTER:], zn[:ROPE_QUARTER]], axis=0)
            dn = jnp.concatenate([zn[-ROPE_QUARTER:], zn[:-ROPE_QUARTER]], axis=0)
            q_ref[j, hh * HEAD_DIM:(hh + 1) * HEAD_DIM, :] = (
                zn * qc + up * qa + dn * qb).astype(BF16)

    zu = proj(_OFF_U, _OFF_VS)

    kr = head_norm_rope(zkv[:, :KV_WIDTH], kg_ref[...])
    first = _lane_lt64(kr.shape)
    k_ref[0] = jnp.where(first, kr, 0.0).astype(BF16)
    k_ref[1] = jnp.where(first, 0.0, kr).astype(BF16)

    vt = zkv[:, KV_WIDTH:].T
    ones = jnp.ones((VT_ROWS - HEAD_DIM, tk), BF16)
    for g in range(N_KV_HEADS):
        for r in range(rows // tk):
            vt_ref[g, r, :HEAD_DIM, :] = vt[g * HEAD_DIM:(g + 1) * HEAD_DIM,
                                            r * tk:(r + 1) * tk].astype(BF16)
            vt_ref[g, r, HEAD_DIM:, :] = ones

    zvs = proj(_OFF_VS, _OFF_GA)
    u = _gelu(zu)
    zga = proj(_OFF_GA, _OFF_GB)

    vs = _rms(_gelu(zvs), sgg_ref[...]).astype(BF16)
    zgb = proj(_OFF_GB, IN_WIDTH)
    sga_ref[...] = jax.nn.sigmoid(zga).astype(BF16)
    pair = 2 * CHUNK
    lt64 = _lane_lt64((CHUNK, 2 * LANES))
    sgu_rows = []
    for c in range(rows // pair):
        r0 = c * pair
        slabs0, slabs1 = [], []
        for s in range(SG_WIDTH // LANES):
            cols = slice(s * LANES, (s + 1) * LANES)
            rhs = jnp.concatenate([vs[r0:r0 + CHUNK, cols], vs[r0 + CHUNK:r0 + pair, cols]], axis=1)
            oa = jnp.dot(sgw_ref[2 * s], rhs, preferred_element_type=F32)
            ob = jnp.dot(sgw_ref[2 * s + 1], rhs, preferred_element_type=F32)
            bias = sgb_ref[:, cols]
            sp = jnp.where(lt64, oa, ob) + jnp.concatenate([bias, bias], axis=1)
            slabs0.append(u[r0:r0 + CHUNK, cols] * sp[:, :LANES])
            slabs1.append(u[r0 + CHUNK:r0 + pair, cols] * sp[:, LANES:])
        sgu_rows.append(jnp.concatenate(slabs0, axis=1))
        sgu_rows.append(jnp.concatenate(slabs1, axis=1))
    sgu = jnp.concatenate(sgu_rows, axis=0).astype(BF16)
    sgb_gate = jax.nn.sigmoid(zgb)
    yb = jnp.dot(sgu, wb_ref[...], preferred_element_type=F32)
    ybg_ref[...] = (sgb_gate * yb).astype(BF16)


def _in_proj(x, layer, p, seq_len):
    n = x.shape[0]
    tok, _, tk = _tiles(seq_len)
    n_pos = seq_len // tok
    const2 = lambda i: (0, 0)
    layer3 = lambda i: (layer, 0, 0)
    blocks = (tok * D_MODEL * 4 + D_MODEL * IN_WIDTH * 2 + SG_WIDTH * D_MODEL * 2
              + tok * (Q_WIDTH + 2 * LANES + 2 * VT_ROWS + 2 * D_MODEL) * 2)
    return pl.pallas_call(
        functools.partial(_in_proj_kernel, tk=tk),
        grid=(n // tok,),
        in_specs=[
            pl.BlockSpec((tok, D_MODEL), lambda i: (i, 0)),
            pl.BlockSpec((None, 1, D_MODEL), layer3),
            pl.BlockSpec((None, D_MODEL, IN_WIDTH), layer3),
            pl.BlockSpec((None, HEAD_DIM, tok), lambda i: (layer, 0, i % n_pos)),
            pl.BlockSpec((None, HEAD_DIM, tok), lambda i: (layer, 0, i % n_pos)),
            pl.BlockSpec((None, HEAD_DIM, tok), lambda i: (layer, 0, i % n_pos)),
            pl.BlockSpec((None, 1, LANES), layer3),
            pl.BlockSpec((tok, LANES), lambda i: (i % n_pos, 0)),
            pl.BlockSpec((tok, LANES), lambda i: (i % n_pos, 0)),
            pl.BlockSpec((tok, LANES), lambda i: (i % n_pos, 0)),
            pl.BlockSpec((LANES, LANES), const2),
            pl.BlockSpec((None, 1, SG_WIDTH), layer3),
            pl.BlockSpec((None, SG_GROUPS, CHUNK, CHUNK), lambda i: (layer, 0, 0, 0)),
            pl.BlockSpec((None, CHUNK, SG_WIDTH), layer3),
            pl.BlockSpec((None, SG_WIDTH, D_MODEL), layer3),
        ],
        out_specs=[
            pl.BlockSpec((Q_WIDTH // LANES, LANES, tok), lambda i: (0, 0, i)),
            pl.BlockSpec((N_KV_HEADS, tok, LANES), lambda i: (0, i, 0)),
            pl.BlockSpec((N_KV_HEADS, tok // tk, VT_ROWS, tk), lambda i: (0, i, 0, 0)),
            pl.BlockSpec((tok, D_MODEL), lambda i: (i, 0)),
            pl.BlockSpec((tok, D_MODEL), lambda i: (i, 0)),
        ],
        out_shape=[
            jax.ShapeDtypeStruct((Q_WIDTH // LANES, LANES, n), BF16),
            jax.ShapeDtypeStruct((N_KV_HEADS, n, LANES), BF16),
            jax.ShapeDtypeStruct((N_KV_HEADS, n // tk, VT_ROWS, tk), BF16),
            jax.ShapeDtypeStruct((n, D_MODEL), BF16),
            jax.ShapeDtypeStruct((n, D_MODEL), BF16),
        ],
        compiler_params=pltpu.CompilerParams(
            dimension_semantics=("arbitrary",), vmem_limit_bytes=_vmem_limit(blocks)),
        name="in_proj",
    )(x, p["attn_norm_g"], p["w_in"], p["q_cos"], p["q_sa"], p["q_sb"], p["k_norm_g"],
      p["rope_cos"], p["rope_sa"], p["rope_sb"], p["head_ones"],
      p["sg_norm_g"], p["sg_w"], p["sg_bias"], p["w_branch_b"])


def _attn_mix_kernel(q_ref, qn_ref, k_ref, vt_ref, sga_ref, ybg_ref, x_ref, wa_ref, wm_ref,
                     o_ref, acc_ref, m_ref, s0_ref, s1_ref, bm0_ref, bm1_ref, *, tk):
    tq = q_ref.shape[2]
    n_kv = k_ref.shape[1] // tk
    acc_ref[...] = jnp.zeros(acc_ref.shape, F32)
    m_ref[...] = jnp.full(m_ref.shape, NEG_BIG, F32)

    def scores(blk, s_ref, bm_ref, h, qt_ref):
        g, j = divmod(h, Q_PER_KV)
        start = pl.multiple_of(blk * tk, tk)
        kb = k_ref[g, pl.ds(start, tk), :]
        s = jnp.dot(kb, qt_ref[j], preferred_element_type=F32)
        s_ref[h] = s
        bm_ref[h:h + 1, :] = jnp.max(s, axis=0, keepdims=True)

    def accumulate(blk, s_ref, bm_ref, h):
        g = h // Q_PER_KV
        vb = vt_ref[g, blk]
        m_old = m_ref[h:h + 1, :]
        m_new = jnp.maximum(m_old, bm_ref[h:h + 1, :])
        alpha = jnp.exp2(m_old - m_new)
        pt = jnp.exp2(s_ref[h] - m_new).astype(BF16)
        pv = jnp.dot(vb, pt, preferred_element_type=F32)
        acc_ref[h] = alpha * acc_ref[h] + pv
        m_ref[h:h + 1, :] = m_new

    def step(score_blk, score_refs, acc_blk, acc_refs, qt_ref=q_ref):
        for h in range(N_Q_HEADS):
            if score_blk is not None:
                scores(score_blk, *score_refs, h, qt_ref)
            if acc_blk is not None:
                accumulate(acc_blk, *acc_refs, h)

    even = (s0_ref, bm0_ref)
    odd = (s1_ref, bm1_ref)

    @pl.when(pl.program_id(1) == 0)
    def _():
        step(0, even, None, None)

    def kv_pair(t, carry):
        step(2 * t + 1, odd, 2 * t, even)
        step(2 * t + 2, even, 2 * t + 1, odd)
        return carry

    lax.fori_loop(0, n_kv // 2 - 1, kv_pair, 0)
    step(n_kv - 1, odd, n_kv - 2, even)
    step(0, even, n_kv - 1, odd, qn_ref)

    _attn_epilogue(acc_ref, sga_ref, ybg_ref, x_ref, wa_ref, wm_ref, o_ref)


def _attn_epilogue(acc_ref, sga_ref, ybg_ref, x_ref, wa_ref, wm_ref, o_ref):
    heads = []
    for h in range(N_Q_HEADS):
        a = acc_ref[h]
        heads.append(a[:HEAD_DIM] * (1.0 / a[HEAD_DIM:HEAD_DIM + 1]))
    ot = jnp.concatenate(heads, axis=0)
    ya = jnp.dot(ot.T.astype(BF16), wa_ref[...], preferred_element_type=F32)
    mix = sga_ref[...].astype(F32) * ya + ybg_ref[...].astype(F32)
    o_ref[...] = x_ref[...] + jnp.dot(mix.astype(BF16), wm_ref[...],
                                      preferred_element_type=F32)


SCORE_AHEAD = 2
KV_BLOCKS_PER_ITER = 8


def _attn_mix_bounded_kernel(q_ref, k_ref, vt_ref, sga_ref, ybg_ref, x_ref, wa_ref, wm_ref,
                             o_ref, acc_ref, *, tk):
    n_kv = k_ref.shape[1] // tk
    acc_ref[...] = jnp.zeros(acc_ref.shape, F32)

    def qk(blk, h):
        g, j = divmod(h, Q_PER_KV)
        start = pl.multiple_of(blk * tk, tk)
        kb = k_ref[g, pl.ds(start, tk), :]
        return jnp.dot(kb, q_ref[j], preferred_element_type=F32)

    def pv(blk, h, s):
        vb = vt_ref[h // Q_PER_KV, blk]
        pt = jnp.exp2(s).astype(BF16)
        acc_ref[h] = acc_ref[h] + jnp.dot(vb, pt, preferred_element_type=F32)

    per_iter = math.gcd(n_kv, KV_BLOCKS_PER_ITER)

    def kv_group(t, carry):
        items = [(per_iter * t + d, h) for d in range(per_iter) for h in range(N_Q_HEADS)]
        pending = [qk(*items[n]) for n in range(SCORE_AHEAD)]
        for n in range(len(items)):
            if n + SCORE_AHEAD < len(items):
                pending.append(qk(*items[n + SCORE_AHEAD]))
            pv(*items[n], pending.pop(0))
        return carry

    lax.fori_loop(0, n_kv // per_iter, kv_group, 0)
    _attn_epilogue(acc_ref, sga_ref, ybg_ref, x_ref, wa_ref, wm_ref, o_ref)


def _attn_mix(x, q, kp, vt, sga, ybg, layer, p, batch, seq_len, bounded):
    _, tq, tk = _tiles(seq_len)
    n_kv = seq_len // tk
    n_q = seq_len // tq
    row = lambda b, i: (b * n_q + i, 0)
    q_spec = pl.BlockSpec((Q_WIDTH // LANES, LANES, tq), lambda b, i: (0, 0, b * n_q + i))
    q_next = pl.BlockSpec((Q_WIDTH // LANES, LANES, tq),
                          lambda b, i: (0, 0, b * n_q + jnp.minimum(i + 1, n_q - 1)))
    common = [
        pl.BlockSpec((N_KV_HEADS, seq_len, LANES), lambda b, i: (0, b, 0)),
        pl.BlockSpec((N_KV_HEADS, n_kv, VT_ROWS, tk), lambda b, i: (0, b, 0, 0)),
        pl.BlockSpec((tq, D_MODEL), row),
        pl.BlockSpec((tq, D_MODEL), row),
        pl.BlockSpec((tq, D_MODEL), row),
        pl.BlockSpec((None, Q_WIDTH, D_MODEL), lambda b, i: (layer, 0, 0)),
        pl.BlockSpec((None, D_MODEL, D_MODEL), lambda b, i: (layer, 0, 0)),
    ]
    blocks = (tq * (2 * Q_WIDTH + 2 * D_MODEL) * 2 + 2 * tq * D_MODEL * 4
              + N_KV_HEADS * seq_len * (LANES + VT_ROWS) * 2
              + (Q_WIDTH + D_MODEL) * D_MODEL * 2)
    acc = pltpu.VMEM((N_Q_HEADS, VT_ROWS, tq), F32)
    if bounded:
        body, q_specs, q_args = _attn_mix_bounded_kernel, [q_spec], (q,)
        scratch_shapes = [acc]
        scratch = N_Q_HEADS * tq * 4 * VT_ROWS
    else:
        body, q_specs, q_args = _attn_mix_kernel, [q_spec, q_next], (q, q)
        scratch_shapes = [
            acc,
            pltpu.VMEM((N_Q_HEADS, tq), F32),
            pltpu.VMEM((N_Q_HEADS, tk, tq), F32),
            pltpu.VMEM((N_Q_HEADS, tk, tq), F32),
            pltpu.VMEM((N_Q_HEADS, tq), F32),
            pltpu.VMEM((N_Q_HEADS, tq), F32),
        ]
        scratch = N_Q_HEADS * tq * 4 * (VT_ROWS + 2 * tk + 3)
    return pl.pallas_call(
        functools.partial(body, tk=tk),
        grid=(batch, n_q),
        in_specs=q_specs + common,
        out_specs=pl.BlockSpec((tq, D_MODEL), row),
        out_shape=jax.ShapeDtypeStruct(x.shape, F32),
        scratch_shapes=scratch_shapes,
        compiler_params=pltpu.CompilerParams(
            dimension_semantics=("arbitrary", "arbitrary"),
            vmem_limit_bytes=_vmem_limit(blocks, scratch)),
        name="attn_mix_bounded" if bounded else "attn_mix",
    )(*q_args, kp, vt, sga, ybg, x, p["w_branch_a"], p["w_mix_out"])


FFN_COLS = 512


def _conv_ffn_kernel(xp_ref, x_ref, xn_ref, ng_ref, wu_ref, cw_ref, cb_ref, wd_ref, fg_ref,
                     o_ref, a_ref, *, final):
    i = pl.program_id(1)
    rows = x_ref.shape[0]
    x = x_ref[...]
    xp = jnp.where(i > 0, xp_ref[...], 0.0)
    xn = jnp.where(i < pl.num_programs(1) - 1, xn_ref[...], 0.0)
    hb = _rms(jnp.concatenate([xp, x, xn], axis=0), ng_ref[...]).astype(BF16)

    n_slab = FFN_COLS // LANES

    def up(lo):
        a = jnp.dot(hb, wu_ref[:, lo:lo + FFN_COLS], preferred_element_type=F32)
        for j in range(n_slab):
            a_ref[lo // LANES + j] = a[:, j * LANES:(j + 1) * LANES]

    def conv(lo):
        outs = []
        for j in range(n_slab):
            slab = lo // LANES + j
            cols = slice(lo + j * LANES, lo + (j + 1) * LANES)
            w = cw_ref[:, cols]
            outs.append(a_ref[slab, HALO - 1:HALO - 1 + rows, :] * w[0:1]
                        + a_ref[slab, HALO:HALO + rows, :] * w[1:2]
                        + a_ref[slab, HALO + 1:HALO + 1 + rows, :] * w[2:3] + cb_ref[:, cols])
        return jnp.concatenate(outs, axis=1)

    def gate(c):
        return (_gelu(conv(c * FFN_COLS)) * conv(D_FF + c * FFN_COLS)).astype(BF16)

    def down(g, c):
        return jnp.dot(g, wd_ref[c * FFN_COLS:(c + 1) * FFN_COLS, :], preferred_element_type=F32)

    def up_pair(c):
        up(c * FFN_COLS)
        up(D_FF + c * FFN_COLS)

    n_chunk = D_FF // FFN_COLS
    up_pair(0)
    up_pair(1)
    y = x
    gates = {0: gate(0)}
    for c in range(2, n_chunk + 1):
        if c < n_chunk:
            up_pair(c)
        y = y + down(gates.pop(c - 2), c - 2)
        gates[c - 1] = gate(c - 1)
    y = y + down(gates.pop(n_chunk - 1), n_chunk - 1)
    o_ref[...] = _rms(y, fg_ref[...]) if final else y


def _conv_ffn(x, layer, p, batch, seq_len, final):
    tok, _, _ = _tiles(seq_len)
    n_t = seq_len // tok
    per = tok // HALO
    n_halo = seq_len // HALO
    row = lambda b, i: (b * n_t + i, 0)
    prev = lambda b, i: (b * n_halo + jnp.maximum(i * per - 1, 0), 0)
    nxt = lambda b, i: (b * n_halo + jnp.minimum((i + 1) * per, n_halo - 1), 0)
    layer3 = lambda b, i: (layer, 0, 0)
    blocks = 2 * tok * D_MODEL * 4 + 3 * D_MODEL * D_FF * 2
    return pl.pallas_call(
        functools.partial(_conv_ffn_kernel, final=final),
        grid=(batch, n_t),
        in_specs=[
            pl.BlockSpec((HALO, D_MODEL), prev),
            pl.BlockSpec((tok, D_MODEL), row),
            pl.BlockSpec((HALO, D_MODEL), nxt),
            pl.BlockSpec((None, 1, D_MODEL), layer3),
            pl.BlockSpec((None, D_MODEL, 2 * D_FF), layer3),
            pl.BlockSpec((None, CONV_W, 2 * D_FF), layer3),
            pl.BlockSpec((None, 1, 2 * D_FF), layer3),
            pl.BlockSpec((None, D_FF, D_MODEL), layer3),
            pl.BlockSpec((1, D_MODEL), lambda b, i: (0, 0)),
        ],
        out_specs=pl.BlockSpec((tok, D_MODEL), row),
        out_shape=jax.ShapeDtypeStruct(x.shape, F32),
        scratch_shapes=[pltpu.VMEM((2 * D_FF // LANES, tok + 2 * HALO, LANES), F32)],
        compiler_params=pltpu.CompilerParams(
            dimension_semantics=("arbitrary", "arbitrary"),
            vmem_limit_bytes=_vmem_limit(blocks, 2 * D_FF * (tok + 2 * HALO) * 4)),
        name="conv_ffn",
    )(x, x, x, p["ffn_norm_g"], p["w_up"], p["conv_w"], p["conv_b"], p["w_down"],
      p["final_norm_g"])


def _rope_tables(seq_len):
    t = jnp.arange(seq_len)
    half = HEAD_DIM // 2
    freq = ROPE_THETA ** (-jnp.arange(0, half, 2, dtype=F32) / half)
    ang_r = (t // GRID_W).astype(F32)[:, None] * freq[None, :]
    ang_c = (t % GRID_W).astype(F32)[:, None] * freq[None, :]
    cr, sr, cc, sc = jnp.cos(ang_r), jnp.sin(ang_r), jnp.cos(ang_c), jnp.sin(ang_c)
    zero = jnp.zeros_like(sr)
    per_head = lambda parts: jnp.tile(jnp.concatenate(parts, axis=1), (1, LANES // HEAD_DIM))
    cos = per_head([cr, cr, cc, cc])
    sa = per_head([-sr, zero, -sc, zero])
    sb = per_head([zero, sr, zero, sc])
    return cos, sa, sb


def _prepare_params(attn_norm_g, w_in, q_norm_g, k_norm_g, sg_norm_g, sg_w, sg_b,
                    w_branch_a, w_branch_b, w_mix_out, ffn_norm_g, w_up, conv_w, conv_b,
                    w_down, final_norm_g):
    depth = w_in.shape[0]
    order = np.concatenate([
        np.arange(h * HEAD_DIM, (h + 1) * HEAD_DIM)
        for j in range(Q_PER_KV) for h in (j, Q_PER_KV + j)])
    w_in_p = jnp.concatenate([w_in[:, :, order], w_in[:, :, Q_WIDTH:]], axis=2).astype(BF16)
    two_heads = lambda g: jnp.tile(g, (1, LANES // HEAD_DIM)).reshape(depth, 1, LANES)
    blk = np.arange(LANES) // HEAD_DIM
    return dict(
        attn_norm_g=attn_norm_g.reshape(depth, 1, D_MODEL),
        w_in=w_in_p,
        q_norm_g=q_norm_g,
        score_bound=_score_bound(q_norm_g, k_norm_g),
        k_norm_g=two_heads(k_norm_g),
        head_ones=jnp.asarray(blk[:, None] == blk[None, :], BF16),
        sg_norm_g=sg_norm_g.reshape(depth, 1, SG_WIDTH),
        sg_w=sg_w.astype(BF16),
        sg_bias=jnp.repeat(jnp.swapaxes(sg_b, 1, 2), SG_GROUP_DIM, axis=2),
        w_branch_a=w_branch_a.astype(BF16),
        w_branch_b=w_branch_b.astype(BF16),
        w_mix_out=w_mix_out.astype(BF16),
        ffn_norm_g=ffn_norm_g.reshape(depth, 1, D_MODEL),
        w_up=w_up.astype(BF16),
        conv_w=conv_w,
        conv_b=conv_b.reshape(depth, 1, 2 * D_FF),
        w_down=w_down.astype(BF16),
        final_norm_g=final_norm_g.reshape(1, D_MODEL),
    )


def _score_bound(q_norm_g, k_norm_g):
    rounding_margin = 1.02
    gq = jnp.max(jnp.abs(q_norm_g), axis=1)
    gk = jnp.max(jnp.abs(k_norm_g), axis=1)
    return HEAD_DIM * Q_SCALE * gq * gk * rounding_margin


def _trunk(x3, p):
    batch, seq_len, _ = x3.shape
    depth = p["w_in"].shape[0]
    tok, tq, tk = _tiles(seq_len)
    assert seq_len % tok == 0 and seq_len % tq == 0 and seq_len % (2 * tk) == 0
    assert tok % (2 * CHUNK) == 0 and tok % tk == 0 and tok % HALO == 0
    p = dict(p)
    p["rope_cos"], p["rope_sa"], p["rope_sb"] = _rope_tables(seq_len)
    g = p["q_norm_g"][:, :, None] * Q_SCALE
    head = lambda t: t[:, :HEAD_DIM].T[None]
    p["q_cos"] = head(p["rope_cos"]) * g
    p["q_sa"] = head(p["rope_sa"]) * jnp.roll(g, -ROPE_QUARTER, axis=1)
    p["q_sb"] = head(p["rope_sb"]) * jnp.roll(g, ROPE_QUARTER, axis=1)
    x = x3.reshape(batch * seq_len, D_MODEL)
    for layer in range(depth):
        q, kp, vt, sga, ybg = _in_proj(x, layer, p, seq_len)
        attn = lambda bounded: functools.partial(
            _attn_mix, layer=layer, p=p, batch=batch, seq_len=seq_len, bounded=bounded)
        x = lax.cond(p["score_bound"][layer] <= EXP2_SAFE_BOUND, attn(True), attn(False),
                     x, q, kp, vt, sga, ybg)
        x = _conv_ffn(x, layer, p, batch, seq_len, final=(layer == depth - 1))
    return x.reshape(batch, seq_len, D_MODEL)


def kernel(x_prompt, x_sample, attn_norm_g, w_in, q_norm_g, k_norm_g, sg_norm_g, sg_w, sg_b,
           w_branch_a, w_branch_b, w_mix_out, ffn_norm_g, w_up, conv_w, conv_b, w_down,
           final_norm_g):
    p = _prepare_params(attn_norm_g, w_in, q_norm_g, k_norm_g, sg_norm_g, sg_w, sg_b,
                        w_branch_a, w_branch_b, w_mix_out, ffn_norm_g, w_up, conv_w, conv_b,
                        w_down, final_norm_g)
    return (_trunk(x_prompt, p), _trunk(x_sample, p))
```

```python
import functools
import math

import jax
import jax.numpy as jnp
import numpy as np
from jax import lax
from jax.experimental import pallas as pl
from jax.experimental.pallas import tpu as pltpu

D_MODEL = 1024
GRID_W = 64
HEAD_DIM = 64
N_Q_HEADS = 8
N_KV_HEADS = 2
Q_PER_KV = N_Q_HEADS // N_KV_HEADS
Q_WIDTH = N_Q_HEADS * HEAD_DIM
KV_WIDTH = N_KV_HEADS * HEAD_DIM
SG_GROUPS = 8
SG_GROUP_DIM = 64
SG_WIDTH = SG_GROUPS * SG_GROUP_DIM
CHUNK = 128
D_FF = 2048
CONV_W = 3
ROPE_THETA = 10000.0
EPS = 1e-6
IN_WIDTH = Q_WIDTH + 2 * KV_WIDTH + 2 * SG_WIDTH + 2 * D_MODEL

LANES = 128
F32_SUBLANES = 8
BF16_SUBLANES = 16
V7X_VMEM_BYTES = 64 * 1024 * 1024

_OFF_K = Q_WIDTH
_OFF_V = _OFF_K + KV_WIDTH
_OFF_U = _OFF_V + KV_WIDTH
_OFF_VS = _OFF_U + SG_WIDTH
_OFF_GA = _OFF_VS + SG_WIDTH
_OFF_GB = _OFF_GA + D_MODEL

VT_ROWS = HEAD_DIM + BF16_SUBLANES
ROPE_QUARTER = HEAD_DIM // 4
NEG_BIG = -1e30
Q_SCALE = math.log2(math.e) / math.sqrt(HEAD_DIM)
HALO = F32_SUBLANES
EXP2_SAFE_BOUND = -1.0

F32 = jnp.float32
BF16 = jnp.bfloat16


def _tiles(seq_len):
    tok = min(512, seq_len)
    tq = min(512, seq_len)
    tk = min(256, seq_len // 2)
    return tok, tq, tk


VMEM_RESERVE_BYTES = 4 << 20
VMEM_TEMP_BYTES = 24 << 20


def _vmem_limit(block_bytes, scratch_bytes=0):
    want = 2 * block_bytes + scratch_bytes + VMEM_TEMP_BYTES
    return int(min(V7X_VMEM_BYTES - VMEM_RESERVE_BYTES, want))


def _rms(x, g):
    return x * lax.rsqrt(jnp.mean(x * x, axis=-1, keepdims=True) + EPS) * g


def _gelu(x):
    c = math.sqrt(2.0 / math.pi)
    half = 0.5 * x
    return half + half * jnp.tanh(x * (c + (c * 0.044715) * (x * x)))


def _lane_lt64(shape):
    lane = lax.broadcasted_iota(jnp.int32, shape, len(shape) - 1)
    return (lane % LANES) < HEAD_DIM


def _in_proj_kernel(x_ref, ng_ref, w_ref, qcos_ref, qsa_ref, qsb_ref, kg_ref, cos_ref, sa_ref,
                    sb_ref, bd_ref, sgg_ref, sgw_ref, sgb_ref, wb_ref,
                    q_ref, k_ref, vt_ref, sga_ref, ybg_ref, *, tk):
    rows = x_ref.shape[0]
    hb = _rms(x_ref[...], ng_ref[...]).astype(BF16)

    def proj(lo, hi):
        return jnp.dot(hb, w_ref[:, lo:hi], preferred_element_type=F32)

    cos = cos_ref[...]
    sa = sa_ref[...]
    sb = sb_ref[...]

    def head_norm_rope(z, gain):
        ss = jnp.dot((z * z).astype(BF16), bd_ref[...], preferred_element_type=F32)
        zn = z * lax.rsqrt(ss * (1.0 / HEAD_DIM) + EPS) * gain
        return (zn * cos + pltpu.roll(zn, LANES - ROPE_QUARTER, 1) * sa
                + pltpu.roll(zn, ROPE_QUARTER, 1) * sb)

    zq = proj(0, Q_WIDTH)
    zkv = proj(_OFF_K, _OFF_U)

    qc, qa, qb = qcos_ref[...], qsa_ref[...], qsb_ref[...]
    for j in range(Q_WIDTH // LANES):
        zt = zq[:, j * LANES:(j + 1) * LANES].T
        for hh in range(LANES // HEAD_DIM):
            z = zt[hh * HEAD_DIM:(hh + 1) * HEAD_DIM]
            ss = jnp.sum(z * z, axis=0, keepdims=True)
            zn = z * lax.rsqrt(ss * (1.0 / HEAD_DIM) + EPS)
            up = jnp.concatenate([zn[ROPE_QUARTER:], zn[:ROPE_QUARTER]], axis=0)
            dn = jnp.concatenate([zn[-ROPE_QUARTER:], zn[:-ROPE_QUARTER]], axis=0)
            q_ref[j, hh * HEAD_DIM:(hh + 1) * HEAD_DIM, :] = (
                zn * qc + up * qa + dn * qb).astype(BF16)

    zu = proj(_OFF_U, _OFF_VS)

    kr = head_norm_rope(zkv[:, :KV_WIDTH], kg_ref[...])
    first = _lane_lt64(kr.shape)
    k_ref[0] = jnp.where(first, kr, 0.0).astype(BF16)
    k_ref[1] = jnp.where(first, 0.0, kr).astype(BF16)

    vt = zkv[:, KV_WIDTH:].T
    ones = jnp.ones((VT_ROWS - HEAD_DIM, tk), BF16)
    for g in range(N_KV_HEADS):
        for r in range(rows // tk):
            vt_ref[g, r, :HEAD_DIM, :] = vt[g * HEAD_DIM:(g + 1) * HEAD_DIM,
                                            r * tk:(r + 1) * tk].astype(BF16)
            vt_ref[g, r, HEAD_DIM:, :] = ones

    zvs = proj(_OFF_VS, _OFF_GA)
    u = _gelu(zu)
    zga = proj(_OFF_GA, _OFF_GB)

    vs = _rms(_gelu(zvs), sgg_ref[...]).astype(BF16)
    zgb = proj(_OFF_GB, IN_WIDTH)
    sga_ref[...] = jax.nn.sigmoid(zga).astype(BF16)
    pair = 2 * CHUNK
    lt64 = _lane_lt64((CHUNK, 2 * LANES))
    sgu_rows = []
    for c in range(rows // pair):
        r0 = c * pair
        slabs0, slabs1 = [], []
        for s in range(SG_WIDTH // LANES):
            cols = slice(s * LANES, (s + 1) * LANES)
            rhs = jnp.concatenate([vs[r0:r0 + CHUNK, cols], vs[r0 + CHUNK:r0 + pair, cols]], axis=1)
            oa = jnp.dot(sgw_ref[2 * s], rhs, preferred_element_type=F32)
            ob = jnp.dot(sgw_ref[2 * s + 1], rhs, preferred_element_type=F32)
            bias = sgb_ref[:, cols]
            sp = jnp.where(lt64, oa, ob) + jnp.concatenate([bias, bias], axis=1)
            slabs0.append(u[r0:r0 + CHUNK, cols] * sp[:, :LANES])
            slabs1.append(u[r0 + CHUNK:r0 + pair, cols] * sp[:, LANES:])
        sgu_rows.append(jnp.concatenate(slabs0, axis=1))
        sgu_rows.append(jnp.concatenate(slabs1, axis=1))
    sgu = jnp.concatenate(sgu_rows, axis=0).astype(BF16)
    sgb_gate = jax.nn.sigmoid(zgb)
    yb = jnp.dot(sgu, wb_ref[...], preferred_element_type=F32)
    ybg_ref[...] = (sgb_gate * yb).astype(BF16)


def _in_proj(x, layer, p, seq_len):
    n = x.shape[0]
    tok, _, tk = _tiles(seq_len)
    n_pos = seq_len // tok
    const2 = lambda i: (0, 0)
    layer3 = lambda i: (layer, 0, 0)
    blocks = (tok * D_MODEL * 4 + D_MODEL * IN_WIDTH * 2 + SG_WIDTH * D_MODEL * 2
              + tok * (Q_WIDTH + 2 * LANES + 2 * VT_ROWS + 2 * D_MODEL) * 2)
    return pl.pallas_call(
        functools.partial(_in_proj_kernel, tk=tk),
        grid=(n // tok,),
        in_specs=[
            pl.BlockSpec((tok, D_MODEL), lambda i: (i, 0)),
            pl.BlockSpec((None, 1, D_MODEL), layer3),
            pl.BlockSpec((None, D_MODEL, IN_WIDTH), layer3),
            pl.BlockSpec((None, HEAD_DIM, tok), lambda i: (layer, 0, i % n_pos)),
            pl.BlockSpec((None, HEAD_DIM, tok), lambda i: (layer, 0, i % n_pos)),
            pl.BlockSpec((None, HEAD_DIM, tok), lambda i: (layer, 0, i % n_pos)),
            pl.BlockSpec((None, 1, LANES), layer3),
            pl.BlockSpec((tok, LANES), lambda i: (i % n_pos, 0)),
            pl.BlockSpec((tok, LANES), lambda i: (i % n_pos, 0)),
            pl.BlockSpec((tok, LANES), lambda i: (i % n_pos, 0)),
            pl.BlockSpec((LANES, LANES), const2),
            pl.BlockSpec((None, 1, SG_WIDTH), layer3),
            pl.BlockSpec((None, SG_GROUPS, CHUNK, CHUNK), lambda i: (layer, 0, 0, 0)),
            pl.BlockSpec((None, CHUNK, SG_WIDTH), layer3),
            pl.BlockSpec((None, SG_WIDTH, D_MODEL), layer3),
        ],
        out_specs=[
            pl.BlockSpec((Q_WIDTH // LANES, LANES, tok), lambda i: (0, 0, i)),
            pl.BlockSpec((N_KV_HEADS, tok, LANES), lambda i: (0, i, 0)),
            pl.BlockSpec((N_KV_HEADS, tok // tk, VT_ROWS, tk), lambda i: (0, i, 0, 0)),
            pl.BlockSpec((tok, D_MODEL), lambda i: (i, 0)),
            pl.BlockSpec((tok, D_MODEL), lambda i: (i, 0)),
        ],
        out_shape=[
            jax.ShapeDtypeStruct((Q_WIDTH // LANES, LANES, n), BF16),
            jax.ShapeDtypeStruct((N_KV_HEADS, n, LANES), BF16),
            jax.ShapeDtypeStruct((N_KV_HEADS, n // tk, VT_ROWS, tk), BF16),
            jax.ShapeDtypeStruct((n, D_MODEL), BF16),
            jax.ShapeDtypeStruct((n, D_MODEL), BF16),
        ],
        compiler_params=pltpu.CompilerParams(
            dimension_semantics=("arbitrary",), vmem_limit_bytes=_vmem_limit(blocks)),
        name="in_proj",
    )(x, p["attn_norm_g"], p["w_in"], p["q_cos"], p["q_sa"], p["q_sb"], p["k_norm_g"],
      p["rope_cos"], p["rope_sa"], p["rope_sb"], p["head_ones"],
      p["sg_norm_g"], p["sg_w"], p["sg_bias"], p["w_branch_b"])


def _attn_mix_kernel(q_ref, qn_ref, k_ref, vt_ref, sga_ref, ybg_ref, x_ref, wa_ref, wm_ref,
                     o_ref, acc_ref, m_ref, s0_ref, s1_ref, bm0_ref, bm1_ref, *, tk):
    tq = q_ref.shape[2]
    n_kv = k_ref.shape[1] // tk
    acc_ref[...] = jnp.zeros(acc_ref.shape, F32)
    m_ref[...] = jnp.full(m_ref.shape, NEG_BIG, F32)

    def scores(blk, s_ref, bm_ref, h, qt_ref):
        g, j = divmod(h, Q_PER_KV)
        start = pl.multiple_of(blk * tk, tk)
        kb = k_ref[g, pl.ds(start, tk), :]
        s = jnp.dot(kb, qt_ref[j], preferred_element_type=F32)
        s_ref[h] = s
        bm_ref[h:h + 1, :] = jnp.max(s, axis=0, keepdims=True)

    def accumulate(blk, s_ref, bm_ref, h):
        g = h // Q_PER_KV
        vb = vt_ref[g, blk]
        m_old = m_ref[h:h + 1, :]
        m_new = jnp.maximum(m_old, bm_ref[h:h + 1, :])
        alpha = jnp.exp2(m_old - m_new)
        pt = jnp.exp2(s_ref[h] - m_new).astype(BF16)
        pv = jnp.dot(vb, pt, preferred_element_type=F32)
        acc_ref[h] = alpha * acc_ref[h] + pv
        m_ref[h:h + 1, :] = m_new

    def step(score_blk, score_refs, acc_blk, acc_refs, qt_ref=q_ref):
        for h in range(N_Q_HEADS):
            if score_blk is not None:
                scores(score_blk, *score_refs, h, qt_ref)
            if acc_blk is not None:
                accumulate(acc_blk, *acc_refs, h)

    even = (s0_ref, bm0_ref)
    odd = (s1_ref, bm1_ref)

    @pl.when(pl.program_id(1) == 0)
    def _():
        step(0, even, None, None)

    def kv_pair(t, carry):
        step(2 * t + 1, odd, 2 * t, even)
        step(2 * t + 2, even, 2 * t + 1, odd)
        return carry

    lax.fori_loop(0, n_kv // 2 - 1, kv_pair, 0)
    step(n_kv - 1, odd, n_kv - 2, even)
    step(0, even, n_kv - 1, odd, qn_ref)

    _attn_epilogue(acc_ref, sga_ref, ybg_ref, x_ref, wa_ref, wm_ref, o_ref)


def _attn_epilogue(acc_ref, sga_ref, ybg_ref, x_ref, wa_ref, wm_ref, o_ref):
    heads = []
    for h in range(N_Q_HEADS):
        a = acc_ref[h]
        heads.append(a[:HEAD_DIM] * (1.0 / a[HEAD_DIM:HEAD_DIM + 1]))
    ot = jnp.concatenate(heads, axis=0)
    ya = jnp.dot(ot.T.astype(BF16), wa_ref[...], preferred_element_type=F32)
    mix = sga_ref[...].astype(F32) * ya + ybg_ref[...].astype(F32)
    o_ref[...] = x_ref[...] + jnp.dot(mix.astype(BF16), wm_ref[...],
                                      preferred_element_type=F32)


SCORE_AHEAD = 2
KV_BLOCKS_PER_ITER = 8


def _attn_mix_bounded_kernel(q_ref, k_ref, vt_ref, sga_ref, ybg_ref, x_ref, wa_ref, wm_ref,
                             o_ref, acc_ref, *, tk):
    n_kv = k_ref.shape[1] // tk
    acc_ref[...] = jnp.zeros(acc_ref.shape, F32)

    def qk(blk, h):
        g, j = divmod(h, Q_PER_KV)
        start = pl.multiple_of(blk * tk, tk)
        kb = k_ref[g, pl.ds(start, tk), :]
        return jnp.dot(kb, q_ref[j], preferred_element_type=F32)

    def pv(blk, h, s):
        vb = vt_ref[h // Q_PER_KV, blk]
        pt = jnp.exp2(s).astype(BF16)
        acc_ref[h] = acc_ref[h] + jnp.dot(vb, pt, preferred_element_type=F32)

    per_iter = math.gcd(n_kv, KV_BLOCKS_PER_ITER)

    def kv_group(t, carry):
        items = [(per_iter * t + d, h) for d in range(per_iter) for h in range(N_Q_HEADS)]
        pending = [qk(*items[n]) for n in range(SCORE_AHEAD)]
        for n in range(len(items)):
            if n + SCORE_AHEAD < len(items):
                pending.append(qk(*items[n + SCORE_AHEAD]))
            pv(*items[n], pending.pop(0))
        return carry

    lax.fori_loop(0, n_kv // per_iter, kv_group, 0)
    _attn_epilogue(acc_ref, sga_ref, ybg_ref, x_ref, wa_ref, wm_ref, o_ref)


def _attn_mix(x, q, kp, vt, sga, ybg, layer, p, batch, seq_len, bounded):
    _, tq, tk = _tiles(seq_len)
    n_kv = seq_len // tk
    n_q = seq_len // tq
    row = lambda b, i: (b * n_q + i, 0)
    q_spec = pl.BlockSpec((Q_WIDTH // LANES, LANES, tq), lambda b, i: (0, 0, b * n_q + i))
    q_next = pl.BlockSpec((Q_WIDTH // LANES, LANES, tq),
                          lambda b, i: (0, 0, b * n_q + jnp.minimum(i + 1, n_q - 1)))
    common = [
        pl.BlockSpec((N_KV_HEADS, seq_len, LANES), lambda b, i: (0, b, 0)),
        pl.BlockSpec((N_KV_HEADS, n_kv, VT_ROWS, tk), lambda b, i: (0, b, 0, 0)),
        pl.BlockSpec((tq, D_MODEL), row),
        pl.BlockSpec((tq, D_MODEL), row),
        pl.BlockSpec((tq, D_MODEL), row),
        pl.BlockSpec((None, Q_WIDTH, D_MODEL), lambda b, i: (layer, 0, 0)),
        pl.BlockSpec((None, D_MODEL, D_MODEL), lambda b, i: (layer, 0, 0)),
    ]
    blocks = (tq * (2 * Q_WIDTH + 2 * D_MODEL) * 2 + 2 * tq * D_MODEL * 4
              + N_KV_HEADS * seq_len * (LANES + VT_ROWS) * 2
              + (Q_WIDTH + D_MODEL) * D_MODEL * 2)
    acc = pltpu.VMEM((N_Q_HEADS, VT_ROWS, tq), F32)
    if bounded:
        body, q_specs, q_args = _attn_mix_bounded_kernel, [q_spec], (q,)
        scratch_shapes = [acc]
        scratch = N_Q_HEADS * tq * 4 * VT_ROWS
    else:
        body, q_specs, q_args = _attn_mix_kernel, [q_spec, q_next], (q, q)
        scratch_shapes = [
            acc,
            pltpu.VMEM((N_Q_HEADS, tq), F32),
            pltpu.VMEM((N_Q_HEADS, tk, tq), F32),
            pltpu.VMEM((N_Q_HEADS, tk, tq), F32),
            pltpu.VMEM((N_Q_HEADS, tq), F32),
            pltpu.VMEM((N_Q_HEADS, tq), F32),
        ]
        scratch = N_Q_HEADS * tq * 4 * (VT_ROWS + 2 * tk + 3)
    return pl.pallas_call(
        functools.partial(body, tk=tk),
        grid=(batch, n_q),
        in_specs=q_specs + common,
        out_specs=pl.BlockSpec((tq, D_MODEL), row),
        out_shape=jax.ShapeDtypeStruct(x.shape, F32),
        scratch_shapes=scratch_shapes,
        compiler_params=pltpu.CompilerParams(
            dimension_semantics=("arbitrary", "arbitrary"),
            vmem_limit_bytes=_vmem_limit(blocks, scratch)),
        name="attn_mix_bounded" if bounded else "attn_mix",
    )(*q_args, kp, vt, sga, ybg, x, p["w_branch_a"], p["w_mix_out"])


FFN_COLS = 512


def _conv_ffn_kernel(xp_ref, x_ref, xn_ref, ng_ref, wu_ref, cw_ref, cb_ref, wd_ref, fg_ref,
                     o_ref, a_ref, *, final):
    i = pl.program_id(1)
    rows = x_ref.shape[0]
    x = x_ref[...]
    xp = jnp.where(i > 0, xp_ref[...], 0.0)
    xn = jnp.where(i < pl.num_programs(1) - 1, xn_ref[...], 0.0)
    hb = _rms(jnp.concatenate([xp, x, xn], axis=0), ng_ref[...]).astype(BF16)

    n_slab = FFN_COLS // LANES

    def up(lo):
        a = jnp.dot(hb, wu_ref[:, lo:lo + FFN_COLS], preferred_element_type=F32)
        for j in range(n_slab):
            a_ref[lo // LANES + j] = a[:, j * LANES:(j + 1) * LANES]

    def conv(lo):
        outs = []
        for j in range(n_slab):
            slab = lo // LANES + j
            cols = slice(lo + j * LANES, lo + (j + 1) * LANES)
            w = cw_ref[:, cols]
            outs.append(a_ref[slab, HALO - 1:HALO - 1 + rows, :] * w[0:1]
                        + a_ref[slab, HALO:HALO + rows, :] * w[1:2]
                        + a_ref[slab, HALO + 1:HALO + 1 + rows, :] * w[2:3] + cb_ref[:, cols])
        return jnp.concatenate(outs, axis=1)

    def gate(c):
        return (_gelu(conv(c * FFN_COLS)) * conv(D_FF + c * FFN_COLS)).astype(BF16)

    def down(g, c):
        return jnp.dot(g, wd_ref[c * FFN_COLS:(c + 1) * FFN_COLS, :], preferred_element_type=F32)

    def up_pair(c):
        up(c * FFN_COLS)
        up(D_FF + c * FFN_COLS)

    n_chunk = D_FF // FFN_COLS
    up_pair(0)
    up_pair(1)
    y = x
    gates = {0: gate(0)}
    for c in range(2, n_chunk + 1):
        if c < n_chunk:
            up_pair(c)
        y = y + down(gates.pop(c - 2), c - 2)
        gates[c - 1] = gate(c - 1)
    y = y + down(gates.pop(n_chunk - 1), n_chunk - 1)
    o_ref[...] = _rms(y, fg_ref[...]) if final else y


def _conv_ffn(x, layer, p, batch, seq_len, final):
    tok, _, _ = _tiles(seq_len)
    n_t = seq_len // tok
    per = tok // HALO
    n_halo = seq_len // HALO
    row = lambda b, i: (b * n_t + i, 0)
    prev = lambda b, i: (b * n_halo + jnp.maximum(i * per - 1, 0), 0)
    nxt = lambda b, i: (b * n_halo + jnp.minimum((i + 1) * per, n_halo - 1), 0)
    layer3 = lambda b, i: (layer, 0, 0)
    blocks = 2 * tok * D_MODEL * 4 + 3 * D_MODEL * D_FF * 2
    return pl.pallas_call(
        functools.partial(_conv_ffn_kernel, final=final),
        grid=(batch, n_t),
        in_specs=[
            pl.BlockSpec((HALO, D_MODEL), prev),
            pl.BlockSpec((tok, D_MODEL), row),
            pl.BlockSpec((HALO, D_MODEL), nxt),
            pl.BlockSpec((None, 1, D_MODEL), layer3),
            pl.BlockSpec((None, D_MODEL, 2 * D_FF), layer3),
            pl.BlockSpec((None, CONV_W, 2 * D_FF), layer3),
            pl.BlockSpec((None, 1, 2 * D_FF), layer3),
            pl.BlockSpec((None, D_FF, D_MODEL), layer3),
            pl.BlockSpec((1, D_MODEL), lambda b, i: (0, 0)),
        ],
        out_specs=pl.BlockSpec((tok, D_MODEL), row),
        out_shape=jax.ShapeDtypeStruct(x.shape, F32),
        scratch_shapes=[pltpu.VMEM((2 * D_FF // LANES, tok + 2 * HALO, LANES), F32)],
        compiler_params=pltpu.CompilerParams(
            dimension_semantics=("arbitrary", "arbitrary"),
            vmem_limit_bytes=_vmem_limit(blocks, 2 * D_FF * (tok + 2 * HALO) * 4)),
        name="conv_ffn",
    )(x, x, x, p["ffn_norm_g"], p["w_up"], p["conv_w"], p["conv_b"], p["w_down"],
      p["final_norm_g"])


def _rope_tables(seq_len):
    t = jnp.arange(seq_len)
    half = HEAD_DIM // 2
    freq = ROPE_THETA ** (-jnp.arange(0, half, 2, dtype=F32) / half)
    ang_r = (t // GRID_W).astype(F32)[:, None] * freq[None, :]
    ang_c = (t % GRID_W).astype(F32)[:, None] * freq[None, :]
    cr, sr, cc, sc = jnp.cos(ang_r), jnp.sin(ang_r), jnp.cos(ang_c), jnp.sin(ang_c)
    zero = jnp.zeros_like(sr)
    per_head = lambda parts: jnp.tile(jnp.concatenate(parts, axis=1), (1, LANES // HEAD_DIM))
    cos = per_head([cr, cr, cc, cc])
    sa = per_head([-sr, zero, -sc, zero])
    sb = per_head([zero, sr, zero, sc])
    return cos, sa, sb


def _prepare_params(attn_norm_g, w_in, q_norm_g, k_norm_g, sg_norm_g, sg_w, sg_b,
                    w_branch_a, w_branch_b, w_mix_out, ffn_norm_g, w_up, conv_w, conv_b,
                    w_down, final_norm_g):
    depth = w_in.shape[0]
    order = np.concatenate([
        np.arange(h * HEAD_DIM, (h + 1) * HEAD_DIM)
        for j in range(Q_PER_KV) for h in (j, Q_PER_KV + j)])
    w_in_p = jnp.concatenate([w_in[:, :, order], w_in[:, :, Q_WIDTH:]], axis=2).astype(BF16)
    two_heads = lambda g: jnp.tile(g, (1, LANES // HEAD_DIM)).reshape(depth, 1, LANES)
    blk = np.arange(LANES) // HEAD_DIM
    return dict(
        attn_norm_g=attn_norm_g.reshape(depth, 1, D_MODEL),
        w_in=w_in_p,
        q_norm_g=q_norm_g,
        score_bound=_score_bound(q_norm_g, k_norm_g),
        k_norm_g=two_heads(k_norm_g),
        head_ones=jnp.asarray(blk[:, None] == blk[None, :], BF16),
        sg_norm_g=sg_norm_g.reshape(depth, 1, SG_WIDTH),
        sg_w=sg_w.astype(BF16),
        sg_bias=jnp.repeat(jnp.swapaxes(sg_b, 1, 2), SG_GROUP_DIM, axis=2),
        w_branch_a=w_branch_a.astype(BF16),
        w_branch_b=w_branch_b.astype(BF16),
        w_mix_out=w_mix_out.astype(BF16),
        ffn_norm_g=ffn_norm_g.reshape(depth, 1, D_MODEL),
        w_up=w_up.astype(BF16),
        conv_w=conv_w,
        conv_b=conv_b.reshape(depth, 1, 2 * D_FF),
        w_down=w_down.astype(BF16),
        final_norm_g=final_norm_g.reshape(1, D_MODEL),
    )


def _score_bound(q_norm_g, k_norm_g):
    rounding_margin = 1.02
    gq = jnp.max(jnp.abs(q_norm_g), axis=1)
    gk = jnp.max(jnp.abs(k_norm_g), axis=1)
    return HEAD_DIM * Q_SCALE * gq * gk * rounding_margin


def _trunk(x3, p):
    batch, seq_len, _ = x3.shape
    depth = p["w_in"].shape[0]
    tok, tq, tk = _tiles(seq_len)
    assert seq_len % tok == 0 and seq_len % tq == 0 and seq_len % (2 * tk) == 0
    assert tok % (2 * CHUNK) == 0 and tok % tk == 0 and tok % HALO == 0
    p = dict(p)
    p["rope_cos"], p["rope_sa"], p["rope_sb"] = _rope_tables(seq_len)
    g = p["q_norm_g"][:, :, None] * Q_SCALE
    head = lambda t: t[:, :HEAD_DIM].T[None]
    p["q_cos"] = head(p["rope_cos"]) * g
    p["q_sa"] = head(p["rope_sa"]) * jnp.roll(g, -ROPE_QUARTER, axis=1)
    p["q_sb"] = head(p["rope_sb"]) * jnp.roll(g, ROPE_QUARTER, axis=1)
    x = x3.reshape(batch * seq_len, D_MODEL)
    for layer in range(depth):
        q, kp, vt, sga, ybg = _in_proj(x, layer, p, seq_len)
        attn = lambda bounded: functools.partial(
            _attn_mix, layer=layer, p=p, batch=batch, seq_len=seq_len, bounded=bounded)
        x = lax.cond(p["score_bound"][layer] <= EXP2_SAFE_BOUND, attn(True), attn(False),
                     x, q, kp, vt, sga, ybg)
        x = _conv_ffn(x, layer, p, batch, seq_len, final=(layer == depth - 1))
    return x.reshape(batch, seq_len, D_MODEL)


def kernel(x_prompt, x_sample, attn_norm_g, w_in, q_norm_g, k_norm_g, sg_norm_g, sg_w, sg_b,
           w_branch_a, w_branch_b, w_mix_out, ffn_norm_g, w_up, conv_w, conv_b, w_down,
           final_norm_g):
    p = _prepare_params(attn_norm_g, w_in, q_norm_g, k_norm_g, sg_norm_g, sg_w, sg_b,
                        w_branch_a, w_branch_b, w_mix_out, ffn_norm_g, w_up, conv_w, conv_b,
                        w_down, final_norm_g)
    return (_trunk(x_prompt, p), _trunk(x_sample, p))
```

```python
import functools
import math

import jax
import jax.numpy as jnp
import numpy as np
from jax import lax
from jax.experimental import pallas as pl
from jax.experimental.pallas import tpu as pltpu

D_MODEL = 1024
GRID_W = 64
HEAD_DIM = 64
N_Q_HEADS = 8
N_KV_HEADS = 2
Q_PER_KV = N_Q_HEADS // N_KV_HEADS
Q_WIDTH = N_Q_HEADS * HEAD_DIM
KV_WIDTH = N_KV_HEADS * HEAD_DIM
SG_GROUPS = 8
SG_GROUP_DIM = 64
SG_WIDTH = SG_GROUPS * SG_GROUP_DIM
CHUNK = 128
D_FF = 2048
CONV_W = 3
ROPE_THETA = 10000.0
EPS = 1e-6
IN_WIDTH = Q_WIDTH + 2 * KV_WIDTH + 2 * SG_WIDTH + 2 * D_MODEL

LANES = 128
F32_SUBLANES = 8
BF16_SUBLANES = 16
V7X_VMEM_BYTES = 64 * 1024 * 1024

_OFF_K = Q_WIDTH
_OFF_V = _OFF_K + KV_WIDTH
_OFF_U = _OFF_V + KV_WIDTH
_OFF_VS = _OFF_U + SG_WIDTH
_OFF_GA = _OFF_VS + SG_WIDTH
_OFF_GB = _OFF_GA + D_MODEL

VT_ROWS = HEAD_DIM + BF16_SUBLANES
ROPE_QUARTER = HEAD_DIM // 4
NEG_BIG = -1e30
Q_SCALE = math.log2(math.e) / math.sqrt(HEAD_DIM)
HALO = F32_SUBLANES
EXP2_SAFE_BOUND = 60.0

F32 = jnp.float32
BF16 = jnp.bfloat16


def _tiles(seq_len):
    tok = min(512, seq_len)
    tq = min(512, seq_len)
    tk = min(256, seq_len // 2)
    return tok, tq, tk


VMEM_RESERVE_BYTES = 4 << 20
VMEM_TEMP_BYTES = 24 << 20


def _vmem_limit(block_bytes, scratch_bytes=0):
    want = 2 * block_bytes + scratch_bytes + VMEM_TEMP_BYTES
    return int(min(V7X_VMEM_BYTES - VMEM_RESERVE_BYTES, want))


def _rms(x, g):
    return x * lax.rsqrt(jnp.mean(x * x, axis=-1, keepdims=True) + EPS) * g


def _gelu(x):
    c = math.sqrt(2.0 / math.pi)
    half = 0.5 * x
    return half + half * jnp.tanh(x * (c + (c * 0.044715) * (x * x)))


def _lane_lt64(shape):
    lane = lax.broadcasted_iota(jnp.int32, shape, len(shape) - 1)
    return (lane % LANES) < HEAD_DIM


def _in_proj_kernel(x_ref, ng_ref, w_ref, qcos_ref, qsa_ref, qsb_ref, kg_ref, cos_ref, sa_ref,
                    sb_ref, bd_ref, sgg_ref, sgw_ref, sgb_ref, wb_ref,
                    q_ref, k_ref, vt_ref, sga_ref, ybg_ref, *, tk):
    rows = x_ref.shape[0]
    hb = _rms(x_ref[...], ng_ref[...]).astype(BF16)

    def proj(lo, hi):
        return jnp.dot(hb, w_ref[:, lo:hi], preferred_element_type=F32)

    cos = cos_ref[...]
    sa = sa_ref[...]
    sb = sb_ref[...]

    def head_norm_rope(z, gain):
        ss = jnp.dot((z * z).astype(BF16), bd_ref[...], preferred_element_type=F32)
        zn = z * lax.rsqrt(ss * (1.0 / HEAD_DIM) + EPS) * gain
        return (zn * cos + pltpu.roll(zn, LANES - ROPE_QUARTER, 1) * sa
                + pltpu.roll(zn, ROPE_QUARTER, 1) * sb)

    zq = proj(0, Q_WIDTH)
    zkv = proj(_OFF_K, _OFF_U)

    qc, qa, qb = qcos_ref[...], qsa_ref[...], qsb_ref[...]
    for j in range(Q_WIDTH // LANES):
        zt = zq[:, j * LANES:(j + 1) * LANES].T
        for hh in range(LANES // HEAD_DIM):
            z = zt[hh * HEAD_DIM:(hh + 1) * HEAD_DIM]
            ss = jnp.sum(z * z, axis=0, keepdims=True)
            zn = z * lax.rsqrt(ss * (1.0 / HEAD_DIM) + EPS)
            up = jnp.concatenate([zn[ROPE_QUARTER:], zn[:ROPE_QUARTER]], axis=0)
            dn = jnp.concatenate([zn[-ROPE_QUARTER:], zn[:-ROPE_QUARTER]], axis=0)
            q_ref[j, hh * HEAD_DIM:(hh + 1) * HEAD_DIM, :] = (
                zn * qc + up * qa + dn * qb).astype(BF16)

    zu = proj(_OFF_U, _OFF_VS)

    kr = head_norm_rope(zkv[:, :KV_WIDTH], kg_ref[...])
    first = _lane_lt64(kr.shape)
    k_ref[0] = jnp.where(first, kr, 0.0).astype(BF16)
    k_ref[1] = jnp.where(first, 0.0, kr).astype(BF16)

    vt = zkv[:, KV_WIDTH:].T
    ones = jnp.ones((VT_ROWS - HEAD_DIM, tk), BF16)
    for g in range(N_KV_HEADS):
        for r in range(rows // tk):
            vt_ref[g, r, :HEAD_DIM, :] = vt[g * HEAD_DIM:(g + 1) * HEAD_DIM,
                                            r * tk:(r + 1) * tk].astype(BF16)
            vt_ref[g, r, HEAD_DIM:, :] = ones

    zvs = proj(_OFF_VS, _OFF_GA)
    u = _gelu(zu)
    zga = proj(_OFF_GA, _OFF_GB)

    vs = _rms(_gelu(zvs), sgg_ref[...]).astype(BF16)
    zgb = proj(_OFF_GB, IN_WIDTH)
    sga_ref[...] = jax.nn.sigmoid(zga).astype(BF16)
    pair = 2 * CHUNK
    lt64 = _lane_lt64((CHUNK, 2 * LANES))
    sgu_rows = []
    for c in range(rows // pair):
        r0 = c * pair
        slabs0, slabs1 = [], []
        for s in range(SG_WIDTH // LANES):
            cols = slice(s * LANES, (s + 1) * LANES)
            rhs = jnp.concatenate([vs[r0:r0 + CHUNK, cols], vs[r0 + CHUNK:r0 + pair, cols]], axis=1)
            oa = jnp.dot(sgw_ref[2 * s], rhs, preferred_element_type=F32)
            ob = jnp.dot(sgw_ref[2 * s + 1], rhs, preferred_element_type=F32)
            bias = sgb_ref[:, cols]
            sp = jnp.where(lt64, oa, ob) + jnp.concatenate([bias, bias], axis=1)
            slabs0.append(u[r0:r0 + CHUNK, cols] * sp[:, :LANES])
            slabs1.append(u[r0 + CHUNK:r0 + pair, cols] * sp[:, LANES:])
        sgu_rows.append(jnp.concatenate(slabs0, axis=1))
        sgu_rows.append(jnp.concatenate(slabs1, axis=1))
    sgu = jnp.concatenate(sgu_rows, axis=0).astype(BF16)
    sgb_gate = jax.nn.sigmoid(zgb)
    yb = jnp.dot(sgu, wb_ref[...], preferred_element_type=F32)
    ybg_ref[...] = (sgb_gate * yb).astype(BF16)


def _in_proj(x, layer, p, seq_len):
    n = x.shape[0]
    tok, _, tk = _tiles(seq_len)
    n_pos = seq_len // tok
    const2 = lambda i: (0, 0)
    layer3 = lambda i: (layer, 0, 0)
    blocks = (tok * D_MODEL * 4 + D_MODEL * IN_WIDTH * 2 + SG_WIDTH * D_MODEL * 2
              + tok * (Q_WIDTH + 2 * LANES + 2 * VT_ROWS + 2 * D_MODEL) * 2)
    return pl.pallas_call(
        functools.partial(_in_proj_kernel, tk=tk),
        grid=(n // tok,),
        in_specs=[
            pl.BlockSpec((tok, D_MODEL), lambda i: (i, 0)),
            pl.BlockSpec((None, 1, D_MODEL), layer3),
            pl.BlockSpec((None, D_MODEL, IN_WIDTH), layer3),
            pl.BlockSpec((None, HEAD_DIM, tok), lambda i: (layer, 0, i % n_pos)),
            pl.BlockSpec((None, HEAD_DIM, tok), lambda i: (layer, 0, i % n_pos)),
            pl.BlockSpec((None, HEAD_DIM, tok), lambda i: (layer, 0, i % n_pos)),
            pl.BlockSpec((None, 1, LANES), layer3),
            pl.BlockSpec((tok, LANES), lambda i: (i % n_pos, 0)),
            pl.BlockSpec((tok, LANES), lambda i: (i % n_pos, 0)),
            pl.BlockSpec((tok, LANES), lambda i: (i % n_pos, 0)),
            pl.BlockSpec((LANES, LANES), const2),
            pl.BlockSpec((None, 1, SG_WIDTH), layer3),
            pl.BlockSpec((None, SG_GROUPS, CHUNK, CHUNK), lambda i: (layer, 0, 0, 0)),
            pl.BlockSpec((None, CHUNK, SG_WIDTH), layer3),
            pl.BlockSpec((None, SG_WIDTH, D_MODEL), layer3),
        ],
        out_specs=[
            pl.BlockSpec((Q_WIDTH // LANES, LANES, tok), lambda i: (0, 0, i)),
            pl.BlockSpec((N_KV_HEADS, tok, LANES), lambda i: (0, i, 0)),
            pl.BlockSpec((N_KV_HEADS, tok // tk, VT_ROWS, tk), lambda i: (0, i, 0, 0)),
            pl.BlockSpec((tok, D_MODEL), lambda i: (i, 0)),
            pl.BlockSpec((tok, D_MODEL), lambda i: (i, 0)),
        ],
        out_shape=[
            jax.ShapeDtypeStruct((Q_WIDTH // LANES, LANES, n), BF16),
            jax.ShapeDtypeStruct((N_KV_HEADS, n, LANES), BF16),
            jax.ShapeDtypeStruct((N_KV_HEADS, n // tk, VT_ROWS, tk), BF16),
            jax.ShapeDtypeStruct((n, D_MODEL), BF16),
            jax.ShapeDtypeStruct((n, D_MODEL), BF16),
        ],
        compiler_params=pltpu.CompilerParams(
            dimension_semantics=("arbitrary",), vmem_limit_bytes=_vmem_limit(blocks)),
        name="in_proj",
    )(x, p["attn_norm_g"], p["w_in"], p["q_cos"], p["q_sa"], p["q_sb"], p["k_norm_g"],
      p["rope_cos"], p["rope_sa"], p["rope_sb"], p["head_ones"],
      p["sg_norm_g"], p["sg_w"], p["sg_bias"], p["w_branch_b"])


def _attn_mix_kernel(q_ref, qn_ref, k_ref, vt_ref, sga_ref, ybg_ref, x_ref, wa_ref, wm_ref,
                     o_ref, acc_ref, m_ref, s0_ref, s1_ref, bm0_ref, bm1_ref, *, tk):
    tq = q_ref.shape[2]
    n_kv = k_ref.shape[1] // tk
    acc_ref[...] = jnp.zeros(acc_ref.shape, F32)
    m_ref[...] = jnp.full(m_ref.shape, NEG_BIG, F32)

    def scores(blk, s_ref, bm_ref, h, qt_ref):
        g, j = divmod(h, Q_PER_KV)
        start = pl.multiple_of(blk * tk, tk)
        kb = k_ref[g, pl.ds(start, tk), :]
        s = jnp.dot(kb, qt_ref[j], preferred_element_type=F32)
        s_ref[h] = s
        bm_ref[h:h + 1, :] = jnp.max(s, axis=0, keepdims=True)

    def accumulate(blk, s_ref, bm_ref, h):
        g = h // Q_PER_KV
        vb = vt_ref[g, blk]
        m_old = m_ref[h:h + 1, :]
        m_new = jnp.maximum(m_old, bm_ref[h:h + 1, :])
        alpha = jnp.exp2(m_old - m_new)
        pt = jnp.exp2(s_ref[h] - m_new).astype(BF16)
        pv = jnp.dot(vb, pt, preferred_element_type=F32)
        acc_ref[h] = alpha * acc_ref[h] + pv
        m_ref[h:h + 1, :] = m_new

    def step(score_blk, score_refs, acc_blk, acc_refs, qt_ref=q_ref):
        for h in range(N_Q_HEADS):
            if score_blk is not None:
                scores(score_blk, *score_refs, h, qt_ref)
            if acc_blk is not None:
                accumulate(acc_blk, *acc_refs, h)

    even = (s0_ref, bm0_ref)
    odd = (s1_ref, bm1_ref)

    @pl.when(pl.program_id(1) == 0)
    def _():
        step(0, even, None, None)

    def kv_pair(t, carry):
        step(2 * t + 1, odd, 2 * t, even)
        step(2 * t + 2, even, 2 * t + 1, odd)
        return carry

    lax.fori_loop(0, n_kv // 2 - 1, kv_pair, 0)
    step(n_kv - 1, odd, n_kv - 2, even)
    step(0, even, n_kv - 1, odd, qn_ref)

    _attn_epilogue(acc_ref, sga_ref, ybg_ref, x_ref, wa_ref, wm_ref, o_ref)


def _attn_epilogue(acc_ref, sga_ref, ybg_ref, x_ref, wa_ref, wm_ref, o_ref):
    heads = []
    for h in range(N_Q_HEADS):
        a = acc_ref[h]
        heads.append(a[:HEAD_DIM] * (1.0 / a[HEAD_DIM:HEAD_DIM + 1]))
    ot = jnp.concatenate(heads, axis=0)
    ya = jnp.dot(ot.T.astype(BF16), wa_ref[...], preferred_element_type=F32)
    mix = sga_ref[...].astype(F32) * ya + ybg_ref[...].astype(F32)
    o_ref[...] = x_ref[...] + jnp.dot(mix.astype(BF16), wm_ref[...],
                                      preferred_element_type=F32)


SCORE_AHEAD = 2
KV_BLOCKS_PER_ITER = 8


def _attn_mix_bounded_kernel(q_ref, k_ref, vt_ref, sga_ref, ybg_ref, x_ref, wa_ref, wm_ref,
                             o_ref, acc_ref, *, tk):
    n_kv = k_ref.shape[1] // tk
    acc_ref[...] = jnp.zeros(acc_ref.shape, F32)

    def qk(blk, h):
        g, j = divmod(h, Q_PER_KV)
        start = pl.multiple_of(blk * tk, tk)
        kb = k_ref[g, pl.ds(start, tk), :]
        return jnp.dot(kb, q_ref[j], preferred_element_type=F32)

    def pv(blk, h, s):
        vb = vt_ref[h // Q_PER_KV, blk]
        pt = jnp.exp2(s).astype(BF16)
        acc_ref[h] = acc_ref[h] + jnp.dot(vb, pt, preferred_element_type=F32)

    per_iter = math.gcd(n_kv, KV_BLOCKS_PER_ITER)

    def kv_group(t, carry):
        items = [(per_iter * t + d, h) for d in range(per_iter) for h in range(N_Q_HEADS)]
        pending = [qk(*items[n]) for n in range(SCORE_AHEAD)]
        for n in range(len(items)):
            if n + SCORE_AHEAD < len(items):
                pending.append(qk(*items[n + SCORE_AHEAD]))
            pv(*items[n], pending.pop(0))
        return carry

    lax.fori_loop(0, n_kv // per_iter, kv_group, 0)
    _attn_epilogue(acc_ref, sga_ref, ybg_ref, x_ref, wa_ref, wm_ref, o_ref)


def _attn_mix(x, q, kp, vt, sga, ybg, layer, p, batch, seq_len, bounded):
    _, tq, tk = _tiles(seq_len)
    n_kv = seq_len // tk
    n_q = seq_len // tq
    row = lambda b, i: (b * n_q + i, 0)
    q_spec = pl.BlockSpec((Q_WIDTH // LANES, LANES, tq), lambda b, i: (0, 0, b * n_q + i))
    q_next = pl.BlockSpec((Q_WIDTH // LANES, LANES, tq),
                          lambda b, i: (0, 0, b * n_q + jnp.minimum(i + 1, n_q - 1)))
    common = [
        pl.BlockSpec((N_KV_HEADS, seq_len, LANES), lambda b, i: (0, b, 0)),
        pl.BlockSpec((N_KV_HEADS, n_kv, VT_ROWS, tk), lambda b, i: (0, b, 0, 0)),
        pl.BlockSpec((tq, D_MODEL), row),
        pl.BlockSpec((tq, D_MODEL), row),
        pl.BlockSpec((tq, D_MODEL), row),
        pl.BlockSpec((None, Q_WIDTH, D_MODEL), lambda b, i: (layer, 0, 0)),
        pl.BlockSpec((None, D_MODEL, D_MODEL), lambda b, i: (layer, 0, 0)),
    ]
    blocks = (tq * (2 * Q_WIDTH + 2 * D_MODEL) * 2 + 2 * tq * D_MODEL * 4
              + N_KV_HEADS * seq_len * (LANES + VT_ROWS) * 2
              + (Q_WIDTH + D_MODEL) * D_MODEL * 2)
    acc = pltpu.VMEM((N_Q_HEADS, VT_ROWS, tq), F32)
    if bounded:
        body, q_specs, q_args = _attn_mix_bounded_kernel, [q_spec], (q,)
        scratch_shapes = [acc]
        scratch = N_Q_HEADS * tq * 4 * VT_ROWS
    else:
        body, q_specs, q_args = _attn_mix_kernel, [q_spec, q_next], (q, q)
        scratch_shapes = [
            acc,
            pltpu.VMEM((N_Q_HEADS, tq), F32),
            pltpu.VMEM((N_Q_HEADS, tk, tq), F32),
            pltpu.VMEM((N_Q_HEADS, tk, tq), F32),
            pltpu.VMEM((N_Q_HEADS, tq), F32),
            pltpu.VMEM((N_Q_HEADS, tq), F32),
        ]
        scratch = N_Q_HEADS * tq * 4 * (VT_ROWS + 2 * tk + 3)
    return pl.pallas_call(
        functools.partial(body, tk=tk),
        grid=(batch, n_q),
        in_specs=q_specs + common,
        out_specs=pl.BlockSpec((tq, D_MODEL), row),
        out_shape=jax.ShapeDtypeStruct(x.shape, F32),
        scratch_shapes=scratch_shapes,
        compiler_params=pltpu.CompilerParams(
            dimension_semantics=("arbitrary", "arbitrary"),
            vmem_limit_bytes=_vmem_limit(blocks, scratch)),
        name="attn_mix_bounded" if bounded else "attn_mix",
    )(*q_args, kp, vt, sga, ybg, x, p["w_branch_a"], p["w_mix_out"])


FFN_COLS = 512
DOWN_LAG = 2


def _conv_ffn_kernel(xp_ref, x_ref, xn_ref, ng_ref, wu_ref, cw_ref, cb_ref, wd_ref, fg_ref,
                     o_ref, a_ref, *, final):
    i = pl.program_id(1)
    rows = x_ref.shape[0]
    x = x_ref[...]
    xp = jnp.where(i > 0, xp_ref[...], 0.0)
    xn = jnp.where(i < pl.num_programs(1) - 1, xn_ref[...], 0.0)
    hb = _rms(jnp.concatenate([xp, x, xn], axis=0), ng_ref[...]).astype(BF16)

    n_slab = FFN_COLS // LANES

    def up(lo):
        a = jnp.dot(hb, wu_ref[:, lo:lo + FFN_COLS], preferred_element_type=F32)
        for j in range(n_slab):
            a_ref[lo // LANES + j] = a[:, j * LANES:(j + 1) * LANES]

    def conv(lo):
        outs = []
        for j in range(n_slab):
            slab = lo // LANES + j
            cols = slice(lo + j * LANES, lo + (j + 1) * LANES)
            w = cw_ref[:, cols]
            outs.append(a_ref[slab, HALO - 1:HALO - 1 + rows, :] * w[0:1]
                        + a_ref[slab, HALO:HALO + rows, :] * w[1:2]
                        + a_ref[slab, HALO + 1:HALO + 1 + rows, :] * w[2:3] + cb_ref[:, cols])
        return jnp.concatenate(outs, axis=1)

    def gate(c):
        return (_gelu(conv(c * FFN_COLS)) * conv(D_FF + c * FFN_COLS)).astype(BF16)

    def down(g, c):
        return jnp.dot(g, wd_ref[c * FFN_COLS:(c + 1) * FFN_COLS, :], preferred_element_type=F32)

    def up_pair(c):
        up(c * FFN_COLS)
        up(D_FF + c * FFN_COLS)

    n_chunk = D_FF // FFN_COLS
    y = x
    gates = {}
    for c in range(n_chunk + DOWN_LAG + 1):
        if c < n_chunk:
            up_pair(c)
        if 0 <= c - DOWN_LAG - 1 < n_chunk:
            y = y + down(gates.pop(c - DOWN_LAG - 1), c - DOWN_LAG - 1)
        if 0 <= c - 1 < n_chunk:
            gates[c - 1] = gate(c - 1)
    o_ref[...] = _rms(y, fg_ref[...]) if final else y


def _conv_ffn(x, layer, p, batch, seq_len, final):
    tok, _, _ = _tiles(seq_len)
    n_t = seq_len // tok
    per = tok // HALO
    n_halo = seq_len // HALO
    row = lambda b, i: (b * n_t + i, 0)
    prev = lambda b, i: (b * n_halo + jnp.maximum(i * per - 1, 0), 0)
    nxt = lambda b, i: (b * n_halo + jnp.minimum((i + 1) * per, n_halo - 1), 0)
    layer3 = lambda b, i: (layer, 0, 0)
    blocks = 2 * tok * D_MODEL * 4 + 3 * D_MODEL * D_FF * 2
    return pl.pallas_call(
        functools.partial(_conv_ffn_kernel, final=final),
        grid=(batch, n_t),
        in_specs=[
            pl.BlockSpec((HALO, D_MODEL), prev),
            pl.BlockSpec((tok, D_MODEL), row),
            pl.BlockSpec((HALO, D_MODEL), nxt),
            pl.BlockSpec((None, 1, D_MODEL), layer3),
            pl.BlockSpec((None, D_MODEL, 2 * D_FF), layer3),
            pl.BlockSpec((None, CONV_W, 2 * D_FF), layer3),
            pl.BlockSpec((None, 1, 2 * D_FF), layer3),
            pl.BlockSpec((None, D_FF, D_MODEL), layer3),
            pl.BlockSpec((1, D_MODEL), lambda b, i: (0, 0)),
        ],
        out_specs=pl.BlockSpec((tok, D_MODEL), row),
        out_shape=jax.ShapeDtypeStruct(x.shape, F32),
        scratch_shapes=[pltpu.VMEM((2 * D_FF // LANES, tok + 2 * HALO, LANES), F32)],
        compiler_params=pltpu.CompilerParams(
            dimension_semantics=("arbitrary", "arbitrary"),
            vmem_limit_bytes=_vmem_limit(blocks, 2 * D_FF * (tok + 2 * HALO) * 4)),
        name="conv_ffn",
    )(x, x, x, p["ffn_norm_g"], p["w_up"], p["conv_w"], p["conv_b"], p["w_down"],
      p["final_norm_g"])


def _rope_tables(seq_len):
    t = jnp.arange(seq_len)
    half = HEAD_DIM // 2
    freq = ROPE_THETA ** (-jnp.arange(0, half, 2, dtype=F32) / half)
    ang_r = (t // GRID_W).astype(F32)[:, None] * freq[None, :]
    ang_c = (t % GRID_W).astype(F32)[:, None] * freq[None, :]
    cr, sr, cc, sc = jnp.cos(ang_r), jnp.sin(ang_r), jnp.cos(ang_c), jnp.sin(ang_c)
    zero = jnp.zeros_like(sr)
    per_head = lambda parts: jnp.tile(jnp.concatenate(parts, axis=1), (1, LANES // HEAD_DIM))
    cos = per_head([cr, cr, cc, cc])
    sa = per_head([-sr, zero, -sc, zero])
    sb = per_head([zero, sr, zero, sc])
    return cos, sa, sb


def _prepare_params(attn_norm_g, w_in, q_norm_g, k_norm_g, sg_norm_g, sg_w, sg_b,
                    w_branch_a, w_branch_b, w_mix_out, ffn_norm_g, w_up, conv_w, conv_b,
                    w_down, final_norm_g):
    depth = w_in.shape[0]
    order = np.concatenate([
        np.arange(h * HEAD_DIM, (h + 1) * HEAD_DIM)
        for j in range(Q_PER_KV) for h in (j, Q_PER_KV + j)])
    w_in_p = jnp.concatenate([w_in[:, :, order], w_in[:, :, Q_WIDTH:]], axis=2).astype(BF16)
    two_heads = lambda g: jnp.tile(g, (1, LANES // HEAD_DIM)).reshape(depth, 1, LANES)
    blk = np.arange(LANES) // HEAD_DIM
    return dict(
        attn_norm_g=attn_norm_g.reshape(depth, 1, D_MODEL),
        w_in=w_in_p,
        q_norm_g=q_norm_g,
        score_bound=_score_bound(q_norm_g, k_norm_g),
        k_norm_g=two_heads(k_norm_g),
        head_ones=jnp.asarray(blk[:, None] == blk[None, :], BF16),
        sg_norm_g=sg_norm_g.reshape(depth, 1, SG_WIDTH),
        sg_w=sg_w.astype(BF16),
        sg_bias=jnp.repeat(jnp.swapaxes(sg_b, 1, 2), SG_GROUP_DIM, axis=2),
        w_branch_a=w_branch_a.astype(BF16),
        w_branch_b=w_branch_b.astype(BF16),
        w_mix_out=w_mix_out.astype(BF16),
        ffn_norm_g=ffn_norm_g.reshape(depth, 1, D_MODEL),
        w_up=w_up.astype(BF16),
        conv_w=conv_w,
        conv_b=conv_b.reshape(depth, 1, 2 * D_FF),
        w_down=w_down.astype(BF16),
        final_norm_g=final_norm_g.reshape(1, D_MODEL),
    )


def _score_bound(q_norm_g, k_norm_g):
    rounding_margin = 1.02
    gq = jnp.max(jnp.abs(q_norm_g), axis=1)
    gk = jnp.max(jnp.abs(k_norm_g), axis=1)
    return HEAD_DIM * Q_SCALE * gq * gk * rounding_margin


def _trunk(x3, p):
    batch, seq_len, _ = x3.shape
    depth = p["w_in"].shape[0]
    tok, tq, tk = _tiles(seq_len)
    assert seq_len % tok == 0 and seq_len % tq == 0 and seq_len % (2 * tk) == 0
    assert tok % (2 * CHUNK) == 0 and tok % tk == 0 and tok % HALO == 0
    p = dict(p)
    p["rope_cos"], p["rope_sa"], p["rope_sb"] = _rope_tables(seq_len)
    g = p["q_norm_g"][:, :, None] * Q_SCALE
    head = lambda t: t[:, :HEAD_DIM].T[None]
    p["q_cos"] = head(p["rope_cos"]) * g
    p["q_sa"] = head(p["rope_sa"]) * jnp.roll(g, -ROPE_QUARTER, axis=1)
    p["q_sb"] = head(p["rope_sb"]) * jnp.roll(g, ROPE_QUARTER, axis=1)
    x = x3.reshape(batch * seq_len, D_MODEL)
    for layer in range(depth):
        q, kp, vt, sga, ybg = _in_proj(x, layer, p, seq_len)
        attn = lambda bounded: functools.partial(
            _attn_mix, layer=layer, p=p, batch=batch, seq_len=seq_len, bounded=bounded)
        x = lax.cond(p["score_bound"][layer] <= EXP2_SAFE_BOUND, attn(True), attn(False),
                     x, q, kp, vt, sga, ybg)
        x = _conv_ffn(x, layer, p, batch, seq_len, final=(layer == depth - 1))
    return x.reshape(batch, seq_len, D_MODEL)


def kernel(x_prompt, x_sample, attn_norm_g, w_in, q_norm_g, k_norm_g, sg_norm_g, sg_w, sg_b,
           w_branch_a, w_branch_b, w_mix_out, ffn_norm_g, w_up, conv_w, conv_b, w_down,
           final_norm_g):
    p = _prepare_params(attn_norm_g, w_in, q_norm_g, k_norm_g, sg_norm_g, sg_w, sg_b,
                        w_branch_a, w_branch_b, w_mix_out, ffn_norm_g, w_up, conv_w, conv_b,
                        w_down, final_norm_g)
    return (_trunk(x_prompt, p), _trunk(x_sample, p))
```

```python
import functools
import math

import jax
import jax.numpy as jnp
import numpy as np
from jax import lax
from jax.experimental import pallas as pl
from jax.experimental.pallas import tpu as pltpu

D_MODEL = 1024
GRID_W = 64
HEAD_DIM = 64
N_Q_HEADS = 8
N_KV_HEADS = 2
Q_PER_KV = N_Q_HEADS // N_KV_HEADS
Q_WIDTH = N_Q_HEADS * HEAD_DIM
KV_WIDTH = N_KV_HEADS * HEAD_DIM
SG_GROUPS = 8
SG_GROUP_DIM = 64
SG_WIDTH = SG_GROUPS * SG_GROUP_DIM
CHUNK = 128
D_FF = 2048
CONV_W = 3
ROPE_THETA = 10000.0
EPS = 1e-6
IN_WIDTH = Q_WIDTH + 2 * KV_WIDTH + 2 * SG_WIDTH + 2 * D_MODEL

LANES = 128
F32_SUBLANES = 8
BF16_SUBLANES = 16
V7X_VMEM_BYTES = 64 * 1024 * 1024

_OFF_K = Q_WIDTH
_OFF_V = _OFF_K + KV_WIDTH
_OFF_U = _OFF_V + KV_WIDTH
_OFF_VS = _OFF_U + SG_WIDTH
_OFF_GA = _OFF_VS + SG_WIDTH
_OFF_GB = _OFF_GA + D_MODEL

VT_ROWS = HEAD_DIM + BF16_SUBLANES
ROPE_QUARTER = HEAD_DIM // 4
NEG_BIG = -1e30
Q_SCALE = math.log2(math.e) / math.sqrt(HEAD_DIM)
HALO = F32_SUBLANES
EXP2_SAFE_BOUND = 60.0

F32 = jnp.float32
BF16 = jnp.bfloat16


def _tiles(seq_len):
    tok = min(512, seq_len)
    tq = min(512, seq_len)
    tk = min(256, seq_len // 2)
    return tok, tq, tk


VMEM_RESERVE_BYTES = 4 << 20
VMEM_TEMP_BYTES = 24 << 20


def _vmem_limit(block_bytes, scratch_bytes=0):
    want = 2 * block_bytes + scratch_bytes + VMEM_TEMP_BYTES
    return int(min(V7X_VMEM_BYTES - VMEM_RESERVE_BYTES, want))


def _rms(x, g):
    return x * lax.rsqrt(jnp.mean(x * x, axis=-1, keepdims=True) + EPS) * g


def _gelu(x):
    c = math.sqrt(2.0 / math.pi)
    half = 0.5 * x
    return half + half * jnp.tanh(x * (c + (c * 0.044715) * (x * x)))


def _lane_lt64(shape):
    lane = lax.broadcasted_iota(jnp.int32, shape, len(shape) - 1)
    return (lane % LANES) < HEAD_DIM


def _in_proj_kernel(x_ref, ng_ref, w_ref, qcos_ref, qsa_ref, qsb_ref, kg_ref, cos_ref, sa_ref,
                    sb_ref, bd_ref, sgg_ref, sgw_ref, sgb_ref, wb_ref,
                    q_ref, k_ref, vt_ref, sga_ref, ybg_ref, *, tk):
    rows = x_ref.shape[0]
    hb = _rms(x_ref[...], ng_ref[...]).astype(BF16)

    def proj(lo, hi):
        return jnp.dot(hb, w_ref[:, lo:hi], preferred_element_type=F32)

    cos = cos_ref[...]
    sa = sa_ref[...]
    sb = sb_ref[...]

    def head_norm_rope(z, gain):
        ss = jnp.dot((z * z).astype(BF16), bd_ref[...], preferred_element_type=F32)
        zn = z * lax.rsqrt(ss * (1.0 / HEAD_DIM) + EPS) * gain
        return (zn * cos + pltpu.roll(zn, LANES - ROPE_QUARTER, 1) * sa
                + pltpu.roll(zn, ROPE_QUARTER, 1) * sb)

    zq = proj(0, Q_WIDTH)
    zkv = proj(_OFF_K, _OFF_U)

    qc, qa, qb = qcos_ref[...], qsa_ref[...], qsb_ref[...]
    for j in range(Q_WIDTH // LANES):
        zt = zq[:, j * LANES:(j + 1) * LANES].T
        for hh in range(LANES // HEAD_DIM):
            z = zt[hh * HEAD_DIM:(hh + 1) * HEAD_DIM]
            ss = jnp.sum(z * z, axis=0, keepdims=True)
            zn = z * lax.rsqrt(ss * (1.0 / HEAD_DIM) + EPS)
            up = jnp.concatenate([zn[ROPE_QUARTER:], zn[:ROPE_QUARTER]], axis=0)
            dn = jnp.concatenate([zn[-ROPE_QUARTER:], zn[:-ROPE_QUARTER]], axis=0)
            q_ref[j, hh * HEAD_DIM:(hh + 1) * HEAD_DIM, :] = (
                zn * qc + up * qa + dn * qb).astype(BF16)

    zu = proj(_OFF_U, _OFF_VS)

    kr = head_norm_rope(zkv[:, :KV_WIDTH], kg_ref[...])
    first = _lane_lt64(kr.shape)
    k_ref[0] = jnp.where(first, kr, 0.0).astype(BF16)
    k_ref[1] = jnp.where(first, 0.0, kr).astype(BF16)

    vt = zkv[:, KV_WIDTH:].T
    ones = jnp.ones((VT_ROWS - HEAD_DIM, tk), BF16)
    for g in range(N_KV_HEADS):
        for r in range(rows // tk):
            vt_ref[g, r, :HEAD_DIM, :] = vt[g * HEAD_DIM:(g + 1) * HEAD_DIM,
                                            r * tk:(r + 1) * tk].astype(BF16)
            vt_ref[g, r, HEAD_DIM:, :] = ones

    zvs = proj(_OFF_VS, _OFF_GA)
    u = _gelu(zu)
    zga = proj(_OFF_GA, _OFF_GB)

    vs = _rms(_gelu(zvs), sgg_ref[...]).astype(BF16)
    zgb = proj(_OFF_GB, IN_WIDTH)
    sga_ref[...] = jax.nn.sigmoid(zga).astype(BF16)
    pair = 2 * CHUNK
    lt64 = _lane_lt64((CHUNK, 2 * LANES))
    sgu_rows = []
    for c in range(rows // pair):
        r0 = c * pair
        slabs0, slabs1 = [], []
        for s in range(SG_WIDTH // LANES):
            cols = slice(s * LANES, (s + 1) * LANES)
            rhs = jnp.concatenate([vs[r0:r0 + CHUNK, cols], vs[r0 + CHUNK:r0 + pair, cols]], axis=1)
            o2 = jnp.dot(sgw_ref[s], rhs, preferred_element_type=F32)
            oa, ob = o2[:CHUNK], o2[CHUNK:]
            bias = sgb_ref[:, cols]
            sp = jnp.where(lt64, oa, ob) + jnp.concatenate([bias, bias], axis=1)
            slabs0.append(u[r0:r0 + CHUNK, cols] * sp[:, :LANES])
            slabs1.append(u[r0 + CHUNK:r0 + pair, cols] * sp[:, LANES:])
        sgu_rows.append(jnp.concatenate(slabs0, axis=1))
        sgu_rows.append(jnp.concatenate(slabs1, axis=1))
    sgu = jnp.concatenate(sgu_rows, axis=0).astype(BF16)
    sgb_gate = jax.nn.sigmoid(zgb)
    yb = jnp.dot(sgu, wb_ref[...], preferred_element_type=F32)
    ybg_ref[...] = (sgb_gate * yb).astype(BF16)


def _in_proj(x, layer, p, seq_len):
    n = x.shape[0]
    tok, _, tk = _tiles(seq_len)
    n_pos = seq_len // tok
    const2 = lambda i: (0, 0)
    layer3 = lambda i: (layer, 0, 0)
    blocks = (tok * D_MODEL * 4 + D_MODEL * IN_WIDTH * 2 + SG_WIDTH * D_MODEL * 2
              + tok * (Q_WIDTH + 2 * LANES + 2 * VT_ROWS + 2 * D_MODEL) * 2)
    return pl.pallas_call(
        functools.partial(_in_proj_kernel, tk=tk),
        grid=(n // tok,),
        in_specs=[
            pl.BlockSpec((tok, D_MODEL), lambda i: (i, 0)),
            pl.BlockSpec((None, 1, D_MODEL), layer3),
            pl.BlockSpec((None, D_MODEL, IN_WIDTH), layer3),
            pl.BlockSpec((None, HEAD_DIM, tok), lambda i: (layer, 0, i % n_pos)),
            pl.BlockSpec((None, HEAD_DIM, tok), lambda i: (layer, 0, i % n_pos)),
            pl.BlockSpec((None, HEAD_DIM, tok), lambda i: (layer, 0, i % n_pos)),
            pl.BlockSpec((None, 1, LANES), layer3),
            pl.BlockSpec((tok, LANES), lambda i: (i % n_pos, 0)),
            pl.BlockSpec((tok, LANES), lambda i: (i % n_pos, 0)),
            pl.BlockSpec((tok, LANES), lambda i: (i % n_pos, 0)),
            pl.BlockSpec((LANES, LANES), const2),
            pl.BlockSpec((None, 1, SG_WIDTH), layer3),
            pl.BlockSpec((None, SG_GROUPS // 2, 2 * CHUNK, CHUNK), lambda i: (layer, 0, 0, 0)),
            pl.BlockSpec((None, CHUNK, SG_WIDTH), layer3),
            pl.BlockSpec((None, SG_WIDTH, D_MODEL), layer3),
        ],
        out_specs=[
            pl.BlockSpec((Q_WIDTH // LANES, LANES, tok), lambda i: (0, 0, i)),
            pl.BlockSpec((N_KV_HEADS, tok, LANES), lambda i: (0, i, 0)),
            pl.BlockSpec((N_KV_HEADS, tok // tk, VT_ROWS, tk), lambda i: (0, i, 0, 0)),
            pl.BlockSpec((tok, D_MODEL), lambda i: (i, 0)),
            pl.BlockSpec((tok, D_MODEL), lambda i: (i, 0)),
        ],
        out_shape=[
            jax.ShapeDtypeStruct((Q_WIDTH // LANES, LANES, n), BF16),
            jax.ShapeDtypeStruct((N_KV_HEADS, n, LANES), BF16),
            jax.ShapeDtypeStruct((N_KV_HEADS, n // tk, VT_ROWS, tk), BF16),
            jax.ShapeDtypeStruct((n, D_MODEL), BF16),
            jax.ShapeDtypeStruct((n, D_MODEL), BF16),
        ],
        compiler_params=pltpu.CompilerParams(
            dimension_semantics=("arbitrary",), vmem_limit_bytes=_vmem_limit(blocks)),
        name="in_proj",
    )(x, p["attn_norm_g"], p["w_in"], p["q_cos"], p["q_sa"], p["q_sb"], p["k_norm_g"],
      p["rope_cos"], p["rope_sa"], p["rope_sb"], p["head_ones"],
      p["sg_norm_g"], p["sg_w"], p["sg_bias"], p["w_branch_b"])


def _attn_mix_kernel(q_ref, qn_ref, k_ref, vt_ref, sga_ref, ybg_ref, x_ref, wa_ref, wm_ref,
                     o_ref, acc_ref, m_ref, s0_ref, s1_ref, bm0_ref, bm1_ref, *, tk):
    tq = q_ref.shape[2]
    n_kv = k_ref.shape[1] // tk
    acc_ref[...] = jnp.zeros(acc_ref.shape, F32)
    m_ref[...] = jnp.full(m_ref.shape, NEG_BIG, F32)

    def scores(blk, s_ref, bm_ref, h, qt_ref):
        g, j = divmod(h, Q_PER_KV)
        start = pl.multiple_of(blk * tk, tk)
        kb = k_ref[g, pl.ds(start, tk), :]
        s = jnp.dot(kb, qt_ref[j], preferred_element_type=F32)
        s_ref[h] = s
        bm_ref[h:h + 1, :] = jnp.max(s, axis=0, keepdims=True)

    def accumulate(blk, s_ref, bm_ref, h):
        g = h // Q_PER_KV
        vb = vt_ref[g, blk]
        m_old = m_ref[h:h + 1, :]
        m_new = jnp.maximum(m_old, bm_ref[h:h + 1, :])
        alpha = jnp.exp2(m_old - m_new)
        pt = jnp.exp2(s_ref[h] - m_new).astype(BF16)
        pv = jnp.dot(vb, pt, preferred_element_type=F32)
        acc_ref[h] = alpha * acc_ref[h] + pv
        m_ref[h:h + 1, :] = m_new

    def step(score_blk, score_refs, acc_blk, acc_refs, qt_ref=q_ref):
        for h in range(N_Q_HEADS):
            if score_blk is not None:
                scores(score_blk, *score_refs, h, qt_ref)
            if acc_blk is not None:
                accumulate(acc_blk, *acc_refs, h)

    even = (s0_ref, bm0_ref)
    odd = (s1_ref, bm1_ref)

    @pl.when(pl.program_id(1) == 0)
    def _():
        step(0, even, None, None)

    def kv_pair(t, carry):
        step(2 * t + 1, odd, 2 * t, even)
        step(2 * t + 2, even, 2 * t + 1, odd)
        return carry

    lax.fori_loop(0, n_kv // 2 - 1, kv_pair, 0)
    step(n_kv - 1, odd, n_kv - 2, even)
    step(0, even, n_kv - 1, odd, qn_ref)

    _attn_epilogue(acc_ref, sga_ref, ybg_ref, x_ref, wa_ref, wm_ref, o_ref)


def _attn_epilogue(acc_ref, sga_ref, ybg_ref, x_ref, wa_ref, wm_ref, o_ref):
    heads = []
    for h in range(N_Q_HEADS):
        a = acc_ref[h]
        heads.append(a[:HEAD_DIM] * (1.0 / a[HEAD_DIM:HEAD_DIM + 1]))
    ot = jnp.concatenate(heads, axis=0)
    ya = jnp.dot(ot.T.astype(BF16), wa_ref[...], preferred_element_type=F32)
    mix = sga_ref[...].astype(F32) * ya + ybg_ref[...].astype(F32)
    o_ref[...] = x_ref[...] + jnp.dot(mix.astype(BF16), wm_ref[...],
                                      preferred_element_type=F32)


SCORE_AHEAD = 2
KV_BLOCKS_PER_ITER = 8


def _attn_mix_bounded_kernel(q_ref, k_ref, vt_ref, sga_ref, ybg_ref, x_ref, wa_ref, wm_ref,
                             o_ref, acc_ref, *, tk):
    n_kv = k_ref.shape[1] // tk
    acc_ref[...] = jnp.zeros(acc_ref.shape, F32)

    def qk(blk, h):
        g, j = divmod(h, Q_PER_KV)
        start = pl.multiple_of(blk * tk, tk)
        kb = k_ref[g, pl.ds(start, tk), :]
        return jnp.dot(kb, q_ref[j], preferred_element_type=F32)

    def pv(blk, h, s):
        vb = vt_ref[h // Q_PER_KV, blk]
        pt = jnp.exp2(s).astype(BF16)
        acc_ref[h] = acc_ref[h] + jnp.dot(vb, pt, preferred_element_type=F32)

    per_iter = math.gcd(n_kv, KV_BLOCKS_PER_ITER)

    def kv_group(t, carry):
        items = [(per_iter * t + d, h) for d in range(per_iter) for h in range(N_Q_HEADS)]
        pending = [qk(*items[n]) for n in range(SCORE_AHEAD)]
        for n in range(len(items)):
            if n + SCORE_AHEAD < len(items):
                pending.append(qk(*items[n + SCORE_AHEAD]))
            pv(*items[n], pending.pop(0))
        return carry

    lax.fori_loop(0, n_kv // per_iter, kv_group, 0)
    _attn_epilogue(acc_ref, sga_ref, ybg_ref, x_ref, wa_ref, wm_ref, o_ref)


def _attn_mix(x, q, kp, vt, sga, ybg, layer, p, batch, seq_len, bounded):
    _, tq, tk = _tiles(seq_len)
    n_kv = seq_len // tk
    n_q = seq_len // tq
    row = lambda b, i: (b * n_q + i, 0)
    q_spec = pl.BlockSpec((Q_WIDTH // LANES, LANES, tq), lambda b, i: (0, 0, b * n_q + i))
    q_next = pl.BlockSpec((Q_WIDTH // LANES, LANES, tq),
                          lambda b, i: (0, 0, b * n_q + jnp.minimum(i + 1, n_q - 1)))
    common = [
        pl.BlockSpec((N_KV_HEADS, seq_len, LANES), lambda b, i: (0, b, 0)),
        pl.BlockSpec((N_KV_HEADS, n_kv, VT_ROWS, tk), lambda b, i: (0, b, 0, 0)),
        pl.BlockSpec((tq, D_MODEL), row),
        pl.BlockSpec((tq, D_MODEL), row),
        pl.BlockSpec((tq, D_MODEL), row),
        pl.BlockSpec((None, Q_WIDTH, D_MODEL), lambda b, i: (layer, 0, 0)),
        pl.BlockSpec((None, D_MODEL, D_MODEL), lambda b, i: (layer, 0, 0)),
    ]
    blocks = (tq * (2 * Q_WIDTH + 2 * D_MODEL) * 2 + 2 * tq * D_MODEL * 4
              + N_KV_HEADS * seq_len * (LANES + VT_ROWS) * 2
              + (Q_WIDTH + D_MODEL) * D_MODEL * 2)
    acc = pltpu.VMEM((N_Q_HEADS, VT_ROWS, tq), F32)
    if bounded:
        body, q_specs, q_args = _attn_mix_bounded_kernel, [q_spec], (q,)
        scratch_shapes = [acc]
        scratch = N_Q_HEADS * tq * 4 * VT_ROWS
    else:
        body, q_specs, q_args = _attn_mix_kernel, [q_spec, q_next], (q, q)
        scratch_shapes = [
            acc,
            pltpu.VMEM((N_Q_HEADS, tq), F32),
            pltpu.VMEM((N_Q_HEADS, tk, tq), F32),
            pltpu.VMEM((N_Q_HEADS, tk, tq), F32),
            pltpu.VMEM((N_Q_HEADS, tq), F32),
            pltpu.VMEM((N_Q_HEADS, tq), F32),
        ]
        scratch = N_Q_HEADS * tq * 4 * (VT_ROWS + 2 * tk + 3)
    return pl.pallas_call(
        functools.partial(body, tk=tk),
        grid=(batch, n_q),
        in_specs=q_specs + common,
        out_specs=pl.BlockSpec((tq, D_MODEL), row),
        out_shape=jax.ShapeDtypeStruct(x.shape, F32),
        scratch_shapes=scratch_shapes,
        compiler_params=pltpu.CompilerParams(
            dimension_semantics=("arbitrary", "arbitrary"),
            vmem_limit_bytes=_vmem_limit(blocks, scratch)),
        name="attn_mix_bounded" if bounded else "attn_mix",
    )(*q_args, kp, vt, sga, ybg, x, p["w_branch_a"], p["w_mix_out"])


FFN_COLS = 512
DOWN_LAG = 2


def _conv_ffn_kernel(xp_ref, x_ref, xn_ref, ng_ref, wu_ref, cw_ref, cb_ref, wd_ref, fg_ref,
                     o_ref, a_ref, *, final):
    i = pl.program_id(1)
    rows = x_ref.shape[0]
    x = x_ref[...]
    xp = jnp.where(i > 0, xp_ref[...], 0.0)
    xn = jnp.where(i < pl.num_programs(1) - 1, xn_ref[...], 0.0)
    hb = _rms(jnp.concatenate([xp, x, xn], axis=0), ng_ref[...]).astype(BF16)

    n_slab = FFN_COLS // LANES

    def up(lo):
        a = jnp.dot(hb, wu_ref[:, lo:lo + FFN_COLS], preferred_element_type=F32)
        for j in range(n_slab):
            a_ref[lo // LANES + j] = a[:, j * LANES:(j + 1) * LANES]

    def conv(lo):
        outs = []
        for j in range(n_slab):
            slab = lo // LANES + j
            cols = slice(lo + j * LANES, lo + (j + 1) * LANES)
            w = cw_ref[:, cols]
            outs.append(a_ref[slab, HALO - 1:HALO - 1 + rows, :] * w[0:1]
                        + a_ref[slab, HALO:HALO + rows, :] * w[1:2]
                        + a_ref[slab, HALO + 1:HALO + 1 + rows, :] * w[2:3] + cb_ref[:, cols])
        return jnp.concatenate(outs, axis=1)

    def gate(c):
        return (_gelu(conv(c * FFN_COLS)) * conv(D_FF + c * FFN_COLS)).astype(BF16)

    def down(g, c):
        return jnp.dot(g, wd_ref[c * FFN_COLS:(c + 1) * FFN_COLS, :], preferred_element_type=F32)

    def up_pair(c):
        up(c * FFN_COLS)
        up(D_FF + c * FFN_COLS)

    n_chunk = D_FF // FFN_COLS
    y = x
    gates = {}
    for c in range(n_chunk + DOWN_LAG + 1):
        if c < n_chunk:
            up_pair(c)
        if 0 <= c - DOWN_LAG - 1 < n_chunk:
            y = y + down(gates.pop(c - DOWN_LAG - 1), c - DOWN_LAG - 1)
        if 0 <= c - 1 < n_chunk:
            gates[c - 1] = gate(c - 1)
    o_ref[...] = _rms(y, fg_ref[...]) if final else y


def _conv_ffn(x, layer, p, batch, seq_len, final):
    tok, _, _ = _tiles(seq_len)
    n_t = seq_len // tok
    per = tok // HALO
    n_halo = seq_len // HALO
    row = lambda b, i: (b * n_t + i, 0)
    prev = lambda b, i: (b * n_halo + jnp.maximum(i * per - 1, 0), 0)
    nxt = lambda b, i: (b * n_halo + jnp.minimum((i + 1) * per, n_halo - 1), 0)
    layer3 = lambda b, i: (layer, 0, 0)
    blocks = 2 * tok * D_MODEL * 4 + 3 * D_MODEL * D_FF * 2
    return pl.pallas_call(
        functools.partial(_conv_ffn_kernel, final=final),
        grid=(batch, n_t),
        in_specs=[
            pl.BlockSpec((HALO, D_MODEL), prev),
            pl.BlockSpec((tok, D_MODEL), row),
            pl.BlockSpec((HALO, D_MODEL), nxt),
            pl.BlockSpec((None, 1, D_MODEL), layer3),
            pl.BlockSpec((None, D_MODEL, 2 * D_FF), layer3),
            pl.BlockSpec((None, CONV_W, 2 * D_FF), layer3),
            pl.BlockSpec((None, 1, 2 * D_FF), layer3),
            pl.BlockSpec((None, D_FF, D_MODEL), layer3),
            pl.BlockSpec((1, D_MODEL), lambda b, i: (0, 0)),
        ],
        out_specs=pl.BlockSpec((tok, D_MODEL), row),
        out_shape=jax.ShapeDtypeStruct(x.shape, F32),
        scratch_shapes=[pltpu.VMEM((2 * D_FF // LANES, tok + 2 * HALO, LANES), F32)],
        compiler_params=pltpu.CompilerParams(
            dimension_semantics=("arbitrary", "arbitrary"),
            vmem_limit_bytes=_vmem_limit(blocks, 2 * D_FF * (tok + 2 * HALO) * 4)),
        name="conv_ffn",
    )(x, x, x, p["ffn_norm_g"], p["w_up"], p["conv_w"], p["conv_b"], p["w_down"],
      p["final_norm_g"])


def _rope_tables(seq_len):
    t = jnp.arange(seq_len)
    half = HEAD_DIM // 2
    freq = ROPE_THETA ** (-jnp.arange(0, half, 2, dtype=F32) / half)
    ang_r = (t // GRID_W).astype(F32)[:, None] * freq[None, :]
    ang_c = (t % GRID_W).astype(F32)[:, None] * freq[None, :]
    cr, sr, cc, sc = jnp.cos(ang_r), jnp.sin(ang_r), jnp.cos(ang_c), jnp.sin(ang_c)
    zero = jnp.zeros_like(sr)
    per_head = lambda parts: jnp.tile(jnp.concatenate(parts, axis=1), (1, LANES // HEAD_DIM))
    cos = per_head([cr, cr, cc, cc])
    sa = per_head([-sr, zero, -sc, zero])
    sb = per_head([zero, sr, zero, sc])
    return cos, sa, sb


def _prepare_params(attn_norm_g, w_in, q_norm_g, k_norm_g, sg_norm_g, sg_w, sg_b,
                    w_branch_a, w_branch_b, w_mix_out, ffn_norm_g, w_up, conv_w, conv_b,
                    w_down, final_norm_g):
    depth = w_in.shape[0]
    order = np.concatenate([
        np.arange(h * HEAD_DIM, (h + 1) * HEAD_DIM)
        for j in range(Q_PER_KV) for h in (j, Q_PER_KV + j)])
    w_in_p = jnp.concatenate([w_in[:, :, order], w_in[:, :, Q_WIDTH:]], axis=2).astype(BF16)
    two_heads = lambda g: jnp.tile(g, (1, LANES // HEAD_DIM)).reshape(depth, 1, LANES)
    blk = np.arange(LANES) // HEAD_DIM
    return dict(
        attn_norm_g=attn_norm_g.reshape(depth, 1, D_MODEL),
        w_in=w_in_p,
        q_norm_g=q_norm_g,
        score_bound=_score_bound(q_norm_g, k_norm_g),
        k_norm_g=two_heads(k_norm_g),
        head_ones=jnp.asarray(blk[:, None] == blk[None, :], BF16),
        sg_norm_g=sg_norm_g.reshape(depth, 1, SG_WIDTH),
        sg_w=sg_w.astype(BF16).reshape(depth, SG_GROUPS // 2, 2 * CHUNK, CHUNK),
        sg_bias=jnp.repeat(jnp.swapaxes(sg_b, 1, 2), SG_GROUP_DIM, axis=2),
        w_branch_a=w_branch_a.astype(BF16),
        w_branch_b=w_branch_b.astype(BF16),
        w_mix_out=w_mix_out.astype(BF16),
        ffn_norm_g=ffn_norm_g.reshape(depth, 1, D_MODEL),
        w_up=w_up.astype(BF16),
        conv_w=conv_w,
        conv_b=conv_b.reshape(depth, 1, 2 * D_FF),
        w_down=w_down.astype(BF16),
        final_norm_g=final_norm_g.reshape(1, D_MODEL),
    )


def _score_bound(q_norm_g, k_norm_g):
    rounding_margin = 1.02
    gq = jnp.max(jnp.abs(q_norm_g), axis=1)
    gk = jnp.max(jnp.abs(k_norm_g), axis=1)
    return HEAD_DIM * Q_SCALE * gq * gk * rounding_margin


def _trunk(x3, p):
    batch, seq_len, _ = x3.shape
    depth = p["w_in"].shape[0]
    tok, tq, tk = _tiles(seq_len)
    assert seq_len % tok == 0 and seq_len % tq == 0 and seq_len % (2 * tk) == 0
    assert tok % (2 * CHUNK) == 0 and tok % tk == 0 and tok % HALO == 0
    p = dict(p)
    p["rope_cos"], p["rope_sa"], p["rope_sb"] = _rope_tables(seq_len)
    g = p["q_norm_g"][:, :, None] * Q_SCALE
    head = lambda t: t[:, :HEAD_DIM].T[None]
    p["q_cos"] = head(p["rope_cos"]) * g
    p["q_sa"] = head(p["rope_sa"]) * jnp.roll(g, -ROPE_QUARTER, axis=1)
    p["q_sb"] = head(p["rope_sb"]) * jnp.roll(g, ROPE_QUARTER, axis=1)
    x = x3.reshape(batch * seq_len, D_MODEL)
    for layer in range(depth):
        q, kp, vt, sga, ybg = _in_proj(x, layer, p, seq_len)
        attn = lambda bounded: functools.partial(
            _attn_mix, layer=layer, p=p, batch=batch, seq_len=seq_len, bounded=bounded)
        x = lax.cond(p["score_bound"][layer] <= EXP2_SAFE_BOUND, attn(True), attn(False),
                     x, q, kp, vt, sga, ybg)
        x = _conv_ffn(x, layer, p, batch, seq_len, final=(layer == depth - 1))
    return x.reshape(batch, seq_len, D_MODEL)


def kernel(x_prompt, x_sample, attn_norm_g, w_in, q_norm_g, k_norm_g, sg_norm_g, sg_w, sg_b,
           w_branch_a, w_branch_b, w_mix_out, ffn_norm_g, w_up, conv_w, conv_b, w_down,
           final_norm_g):
    p = _prepare_params(attn_norm_g, w_in, q_norm_g, k_norm_g, sg_norm_g, sg_w, sg_b,
                        w_branch_a, w_branch_b, w_mix_out, ffn_norm_g, w_up, conv_w, conv_b,
                        w_down, final_norm_g)
    return (_trunk(x_prompt, p), _trunk(x_sample, p))
```
